```python
import math
import jax, jax.numpy as jnp
from jax import lax
import numpy as np

D_MODEL = 2048
BATCH = 4
SEQ = 4096
DEPTH = 1

MEM_LEN = 256
LN_EPS = 1e-5
GLA_HEADS = 4
GLA_DV = (D_MODEL // 2) // GLA_HEADS
GLA_DK = GLA_DV // 2
GLA_GATE_RANK = 16
GLA_TAU = 16.0
GLA_CHUNK = 64
DIL_HD = 128
DIL_HEADS = (D_MODEL // 2) // DIL_HD
DIL_PATTERNS = ((128, 1), (512, 4), (2048, 16))
ROPE_THETA = 500000.0
ROPE_DIMS = DIL_HD // 4
CA_HEADS = 4
CA_HD = D_MODEL // CA_HEADS
D_FF = 5504
CONV_W = 3
DEEPNORM_ALPHA = (2.0 * DEPTH) ** 0.25
DEEPNORM_BETA = (8.0 * DEPTH) ** -0.25
IN_WIDTHS = (GLA_HEADS * GLA_DK, GLA_HEADS * GLA_DK, GLA_HEADS * GLA_DV, GLA_HEADS * GLA_DV,
             GLA_GATE_RANK, DIL_HEADS * DIL_HD, DIL_HEADS * DIL_HD, DIL_HEADS * DIL_HD)
IN_COLS = sum(IN_WIDTHS)
MIX_WIDTH = GLA_HEADS * GLA_DV + DIL_HEADS * DIL_HD

kernel_name = "hybrid_gla_dilated_attn_deepnorm_layer"


def split_cols(h, widths):
    outs, start = [], 0
    for w in widths:
        outs.append(h[..., start:start + w])
        start += w
    return outs


def layer_norm(x, g, b):
    xf = x.astype(jnp.float32)
    mu = jnp.mean(xf, axis=-1, keepdims=True)
    var = jnp.mean(jnp.square(xf - mu), axis=-1, keepdims=True)
    y = (xf - mu) * lax.rsqrt(var + LN_EPS)
    return (y * g.astype(jnp.float32) + b.astype(jnp.float32)).astype(x.dtype)


def partial_rotary(t, cos, sin):
    half = ROPE_DIMS // 2
    t1 = t[..., :half].astype(jnp.float32)
    t2 = t[..., half:ROPE_DIMS].astype(jnp.float32)
    rot = jnp.concatenate([t1 * cos - t2 * sin, t2 * cos + t1 * sin], axis=-1)
    return jnp.concatenate([rot.astype(t.dtype), t[..., ROPE_DIMS:]], axis=-1)


def gla_chunked(q, k, v, log_g):
    B, S, H, dk = q.shape
    dv = v.shape[-1]
    C = GLA_CHUNK
    N = S // C

    def chunk(t):
        return t.astype(jnp.float32).reshape(B, N, C, H, -1).transpose(0, 3, 1, 2, 4)

    qc = chunk(q) * (dk ** -0.5)
    kc, vc, gc = chunk(k), chunk(v), chunk(log_g)
    b = lax.cumsum(gc, axis=3)
    b_last = b[:, :, :, -1:, :]
    q_in = qc * jnp.exp(b)
    k_in = kc * jnp.exp(-b)
    k_end = kc * jnp.exp(b_last - b)
    causal = jnp.tril(jnp.ones((C, C), dtype=bool))
    A = jnp.where(causal, jnp.einsum('bhnik,bhnjk->bhnij', q_in, k_in), 0.0)
    o = jnp.einsum('bhnij,bhnjv->bhniv', A, vc)
    dS = jnp.einsum('bhnjk,bhnjv->bhnkv', k_end, vc)
    decay = jnp.exp(b_last[:, :, :, 0, :])

    def step(state, inp):
        dec, ds = inp
        return dec[..., None] * state + ds, state

    _, s_before = lax.scan(step, jnp.zeros((B, H, dk, dv), jnp.float32),
                           (jnp.moveaxis(decay, 2, 0), jnp.moveaxis(dS, 2, 0)))
    s_before = jnp.moveaxis(s_before, 0, 2)
    o = o + jnp.einsum('bhnik,bhnkv->bhniv', q_in, s_before)
    return o.transpose(0, 2, 3, 1, 4).reshape(B, S, H, dv)


def dilated_branch(q, k, v, window, dilation):
    B, S, H, hd = q.shape
    L = S // dilation
    band = window // dilation
    nb = -(-L // band)
    Lp = nb * band

    def to_sub(t):
        t = t.reshape(B, L, dilation, H, hd).transpose(0, 2, 3, 1, 4)
        t = jnp.pad(t, ((0, 0), (0, 0), (0, 0), (0, Lp - L), (0, 0)))
        return t.reshape(B, dilation, H, nb, band, hd)

    def with_prev(t):
        prev = jnp.pad(t[:, :, :, :-1], ((0, 0), (0, 0), (0, 0), (1, 0), (0, 0), (0, 0)))
        return jnp.concatenate([prev, t], axis=4)

    qb = to_sub(q)
    kk = with_prev(to_sub(k))
    vv = with_prev(to_sub(v))
    s = jnp.einsum('bdhnqc,bdhnkc->bdhnqk', qb, kk).astype(jnp.float32)
    qi = jnp.arange(band)[:, None] + band
    kj = jnp.arange(2 * band)[None, :]
    diff = qi - kj
    blk = jnp.arange(nb)[:, None, None]
    mask = (diff >= 0) & (diff <= band) & ((blk > 0) | (kj >= band))
    s = jnp.where(mask, s, -jnp.inf)
    m = jnp.max(s, axis=-1, keepdims=True)
    p = jnp.exp(s - m)
    den = jnp.sum(p, axis=-1, keepdims=True)
    o = jnp.einsum('bdhnqk,bdhnkc->bdhnqc', p.astype(v.dtype), vv).astype(jnp.float32) / den
    lse = (m + jnp.log(den))[..., 0]
    o = o.reshape(B, dilation, H, Lp, hd)[:, :, :, :L].transpose(0, 3, 1, 2, 4).reshape(B, S, H, hd)
    lse = lse.reshape(B, dilation, H, Lp)[:, :, :, :L].transpose(0, 3, 1, 2).reshape(B, S, H)
    return o, lse


def dilated_attention(q, k, v):
    outs, lses = [], []
    for window, dilation in DIL_PATTERNS:
        o, lse = dilated_branch(q, k, v, window, dilation)
        outs.append(o)
        lses.append(lse)
    w = jax.nn.softmax(jnp.stack(lses, axis=0), axis=0)
    return jnp.sum(w[..., None] * jnp.stack(outs, axis=0), axis=0)


def causal_dwconv(u, w, b):
    S = u.shape[1]
    up = jnp.pad(u, ((0, 0), (CONV_W - 1, 0), (0, 0)))
    y = b + w[0] * up[:, 0:S]
    for i in range(1, CONV_W):
        y = y + w[i] * up[:, i:i + S]
    return y


def setup_inputs(seed: int = 0) -> dict:
    key = jax.random.key(seed)
    ks = jax.random.split(key, 24)
    f32 = jnp.float32
    nrm = lambda k, shape, std: jax.random.normal(k, shape, f32) * std
    beta = DEEPNORM_BETA
    x = jax.random.normal(ks[0], (BATCH, SEQ, D_MODEL), f32)
    mem = jax.random.normal(ks[1], (BATCH, MEM_LEN, D_MODEL), f32)
    offs = jax.random.randint(ks[2], (BATCH, 1), 0, 4096, dtype=jnp.int32)
    positions = offs + jnp.arange(SEQ, dtype=jnp.int32)[None, :]
    col_scale = jnp.concatenate([jnp.full((w,), beta if i in (2, 7) else 1.0, f32)
                                 for i, w in enumerate(IN_WIDTHS)])
    w_in = nrm(ks[3], (DEPTH, D_MODEL, IN_COLS), D_MODEL ** -0.5) * col_scale
    gla_gate_w2 = nrm(ks[4], (DEPTH, GLA_GATE_RANK, GLA_HEADS * GLA_DK), GLA_GATE_RANK ** -0.5)
    gla_gate_b = nrm(ks[5], (DEPTH, GLA_HEADS * GLA_DK), 0.01)
    gla_norm_g = 1.0 + nrm(ks[6], (DEPTH, GLA_HEADS * GLA_DV), 0.02)
    w_out = nrm(ks[7], (DEPTH, MIX_WIDTH, D_MODEL), MIX_WIDTH ** -0.5) * beta
    ln1_g = 1.0 + nrm(ks[8], (DEPTH, D_MODEL), 0.02)
    ln1_b = nrm(ks[9], (DEPTH, D_MODEL), 0.02)
    ca_wq = nrm(ks[10], (DEPTH, D_MODEL, D_MODEL), D_MODEL ** -0.5)
    kv_scale = jnp.concatenate([jnp.ones((D_MODEL,), f32), jnp.full((D_MODEL,), beta, f32)])
    ca_wkv = nrm(ks[11], (DEPTH, D_MODEL, 2 * D_MODEL), D_MODEL ** -0.5) * kv_scale
    ca_wo = nrm(ks[12], (DEPTH, D_MODEL, D_MODEL), D_MODEL ** -0.5) * beta
    ln2_g = 1.0 + nrm(ks[13], (DEPTH, D_MODEL), 0.02)
    ln2_b = nrm(ks[14], (DEPTH, D_MODEL), 0.02)
    ffn_w_in = nrm(ks[15], (DEPTH, D_MODEL, 2 * D_FF), D_MODEL ** -0.5) * beta
    ffn_conv_w = nrm(ks[16], (DEPTH, CONV_W, 2 * D_FF), CONV_W ** -0.5)
    ffn_conv_b = nrm(ks[17], (DEPTH, 2 * D_FF), 0.02)
    ffn_w_out = nrm(ks[18], (DEPTH, D_FF, D_MODEL), D_FF ** -0.5) * beta
    ln3_g = 1.0 + nrm(ks[19], (DEPTH, D_MODEL), 0.02)
    ln3_b = nrm(ks[20], (DEPTH, D_MODEL), 0.02)
    return {"x": x, "mem": mem, "positions": positions, "w_in": w_in,
            "gla_gate_w2": gla_gate_w2, "gla_gate_b": gla_gate_b, "gla_norm_g": gla_norm_g,
            "w_out": w_out, "ln1_g": ln1_g, "ln1_b": ln1_b,
            "ca_wq": ca_wq, "ca_wkv": ca_wkv, "ca_wo": ca_wo, "ln2_g": ln2_g, "ln2_b": ln2_b,
            "ffn_w_in": ffn_w_in, "ffn_conv_w": ffn_conv_w, "ffn_conv_b": ffn_conv_b,
            "ffn_w_out": ffn_w_out, "ln3_g": ln3_g, "ln3_b": ln3_b}


def reference(x, mem, positions, w_in, gla_gate_w2, gla_gate_b, gla_norm_g, w_out, ln1_g, ln1_b,
              ca_wq, ca_wkv, ca_wo, ln2_g, ln2_b, ffn_w_in, ffn_conv_w, ffn_conv_b, ffn_w_out,
              ln3_g, ln3_b):
    B, S, D = x.shape
    M = mem.shape[1]
    inv_freq = ROPE_THETA ** (-jnp.arange(0, ROPE_DIMS, 2, dtype=jnp.float32) / ROPE_DIMS)
    ang = positions.astype(jnp.float32)[..., None] * inv_freq
    cos = jnp.cos(ang)[:, :, None, :]
    sin = jnp.sin(ang)[:, :, None, :]

    for l in range(DEPTH):
        h = x @ w_in[l]
        qg, kg, vg, rg, glr, qd, kd, vd = split_cols(h, IN_WIDTHS)
        log_g = jax.nn.log_sigmoid((glr @ gla_gate_w2[l] + gla_gate_b[l]).astype(jnp.float32)) / GLA_TAU
        o_g = gla_chunked(qg.reshape(B, S, GLA_HEADS, GLA_DK), kg.reshape(B, S, GLA_HEADS, GLA_DK),
                          vg.reshape(B, S, GLA_HEADS, GLA_DV), log_g.reshape(B, S, GLA_HEADS, GLA_DK))
        mu = jnp.mean(o_g, axis=-1, keepdims=True)
        var = jnp.mean(jnp.square(o_g - mu), axis=-1, keepdims=True)
        o_g = ((o_g - mu) * lax.rsqrt(var + LN_EPS)).reshape(B, S, GLA_HEADS * GLA_DV)
        o_g = (o_g * gla_norm_g[l].astype(jnp.float32) * jax.nn.silu(rg.astype(jnp.float32))).astype(x.dtype)
        qd = partial_rotary(qd.reshape(B, S, DIL_HEADS, DIL_HD) * (DIL_HD ** -0.5), cos, sin)
        kd = partial_rotary(kd.reshape(B, S, DIL_HEADS, DIL_HD), cos, sin)
        o_d = dilated_attention(qd, kd, vd.reshape(B, S, DIL_HEADS, DIL_HD))
        o_d = o_d.reshape(B, S, DIL_HEADS * DIL_HD).astype(x.dtype)
        mix = jnp.concatenate([o_g, o_d], axis=-1) @ w_out[l]
        x = layer_norm(DEEPNORM_ALPHA * x + mix, ln1_g[l], ln1_b[l])

        q = (x @ ca_wq[l]).reshape(B, S, CA_HEADS, CA_HD)
        mk, mv = split_cols(mem @ ca_wkv[l], (D_MODEL, D_MODEL))
        mk = mk.reshape(B, M, CA_HEADS, CA_HD)
        mv = mv.reshape(B, M, CA_HEADS, CA_HD)
        s = jnp.einsum('bshc,bmhc->bhsm', q, mk).astype(jnp.float32) * (CA_HD ** -0.5)
        p = jax.nn.softmax(s, axis=-1).astype(mv.dtype)
        o_c = jnp.einsum('bhsm,bmhc->bshc', p, mv).reshape(B, S, D_MODEL)
        x = layer_norm(DEEPNORM_ALPHA * x + o_c @ ca_wo[l], ln2_g[l], ln2_b[l])

        u = causal_dwconv(x @ ffn_w_in[l], ffn_conv_w[l], ffn_conv_b[l])
        gate, up = split_cols(u, (D_FF, D_FF))
        f = (jax.nn.silu(gate) * up) @ ffn_w_out[l]
        x = layer_norm(DEEPNORM_ALPHA * x + f, ln3_g[l], ln3_b[l])
    return x
```

```python
import functools

import jax
import jax.numpy as jnp
from jax import lax
from jax.experimental import pallas as pl
from jax.experimental.pallas import tpu as pltpu

F32 = jnp.float32
BF16 = jnp.bfloat16

LANES = 128
LN_EPS = 1e-5
GLA_HEADS = 4
GLA_DK = 128
GLA_DV = 256
GLA_GATE_RANK = 16
GLA_TAU = 16.0
GLA_CHUNK = 64
DIL_HD = 128
DIL_HEADS = 8
DIL_PATTERNS = ((128, 1), (512, 4), (2048, 16))
DIL_BAND = 128
ROPE_THETA = 500000.0
ROPE_DIMS = 32
ROPE_HALF = ROPE_DIMS // 2
CA_HEADS = 4
CONV_W = 3
FFN_CHUNK = 512
FFN_HALO = 16
VMEM_LIMIT = 56 * 1024 * 1024


def _cparams(*sem):
    return pltpu.CompilerParams(dimension_semantics=sem, vmem_limit_bytes=VMEM_LIMIT)


def _dot(a, b):
    return jnp.dot(a, b, preferred_element_type=F32)


def _dot_nt(a, b):
    return lax.dot_general(a, b, (((1,), (1,)), ((), ())), preferred_element_type=F32)


def _dot_tn(a, b):
    return lax.dot_general(a, b, (((0,), (0,)), ((), ())), preferred_element_type=F32)


def _layer_norm(y, g, b):
    mu = jnp.mean(y, axis=-1, keepdims=True)
    d = y - mu
    var = jnp.mean(d * d, axis=-1, keepdims=True)
    return d * lax.rsqrt(var + LN_EPS) * g + b


def _sigmoid(x):
    return 1.0 / (1.0 + jnp.exp(-x))


def _rope_kernel(pos_ref, inv_ref, cos_ref, sa_ref, sb_ref):
    ang = pos_ref[...] * inv_ref[...]
    lane = lax.broadcasted_iota(jnp.int32, ang.shape, 1)
    c = jnp.cos(ang)
    s = jnp.sin(ang)
    cos_ref[...] = jnp.where(lane < ROPE_DIMS, c, 1.0)
    sa_ref[...] = jnp.where(lane < ROPE_HALF, -s, 0.0)
    sb_ref[...] = jnp.where(lane < ROPE_HALF, 0.0, jnp.where(lane < ROPE_DIMS, s, 0.0))


def _rope_tables(pos_col, inv_row, bs):
    t = pos_col.shape[0]
    out = jax.ShapeDtypeStruct((t, LANES), F32)
    spec = pl.BlockSpec((bs, LANES), lambda i: (i, 0))
    return pl.pallas_call(
        _rope_kernel,
        grid=(t // bs,),
        in_specs=[pl.BlockSpec((bs, 1), lambda i: (i, 0)),
                  pl.BlockSpec((1, LANES), lambda i: (0, 0))],
        out_specs=[spec, spec, spec],
        out_shape=[out, out, out],
        compiler_params=_cparams("parallel"),
        name="rope_tables",
    )(pos_col, inv_row)


def _rope_heads(acc, cos, sa, sb, scale, h_ref):
    for hh in range(acc.shape[1] // LANES):
        t = acc[:, hh * LANES:(hh + 1) * LANES]
        if scale is not None:
            t = t * scale
        out = t * cos + pltpu.roll(t, LANES - ROPE_HALF, 1) * sa + pltpu.roll(t, ROPE_HALF, 1) * sb
        h_ref[:, hh * LANES:(hh + 1) * LANES] = out.astype(h_ref.dtype)


def _inproj_kernel(x_ref, w_ref, wg_ref, cos_ref, sa_ref, sb_ref, h_ref, glr_ref, xb_ref):
    n = pl.program_id(1)

    @pl.when(n == 0)
    def _():
        xb_ref[...] = x_ref[...].astype(BF16)
        glr_ref[...] = _dot(xb_ref[...], wg_ref[...])

    acc = _dot(xb_ref[...], w_ref[...])
    half = acc.shape[1] // 2

    @pl.when(n == 0)
    def _():
        h_ref[:, :half] = (acc[:, :half] * (GLA_DK ** -0.5)).astype(h_ref.dtype)
        h_ref[:, half:] = acc[:, half:].astype(h_ref.dtype)

    @pl.when((n == 1) | (n == 2) | (n == 5))
    def _():
        h_ref[...] = acc.astype(h_ref.dtype)

    @pl.when(n == 3)
    def _():
        _rope_heads(acc, cos_ref[...], sa_ref[...], sb_ref[...], DIL_HD ** -0.5, h_ref)

    @pl.when(n == 4)
    def _():
        _rope_heads(acc, cos_ref[...], sa_ref[...], sb_ref[...], None, h_ref)


def _inproj(x2d, wcat, wglr, cosf, sina, sinb, bm):
    t, d = x2d.shape
    ncols = wcat.shape[1]
    bn = 1024
    tab = pl.BlockSpec((bm, LANES), lambda i, j: (i, 0))
    return pl.pallas_call(
        _inproj_kernel,
        grid=(t // bm, ncols // bn),
        in_specs=[pl.BlockSpec((bm, d), lambda i, j: (i, 0)),
                  pl.BlockSpec((d, bn), lambda i, j: (0, j)),
                  pl.BlockSpec((d, LANES), lambda i, j: (0, 0)),
                  tab, tab, tab],
        out_specs=[pl.BlockSpec((bm, bn), lambda i, j: (i, j)),
                   pl.BlockSpec((bm, LANES), lambda i, j: (i, 0))],
        out_shape=[jax.ShapeDtypeStruct((t, ncols), BF16),
                   jax.ShapeDtypeStruct((t, LANES), F32)],
        scratch_shapes=[pltpu.VMEM((bm, d), BF16)],
        compiler_params=_cparams("parallel", "arbitrary"),
        name="in_projection",
    )(x2d, wcat, wglr, cosf, sina, sinb)


def _split3(v):
    hi = v.astype(BF16)
    r1 = v - hi.astype(F32)
    mid = r1.astype(BF16)
    lo = (r1 - mid.astype(F32)).astype(BF16)
    return hi, mid, lo


def _gla_kernel(q_ref, k_ref, v_ref, r_ref, glr_ref, w2_ref, gb_ref, ng_ref, o_ref, st_ref):
    c = GLA_CHUNK
    sb = q_ref.shape[1]
    grp = 4 * c

    @pl.when(pl.program_id(2) == 0)
    def _():
        st_ref[...] = jnp.zeros_like(st_ref)

    z = _dot(glr_ref[0].astype(BF16), w2_ref[...]) + gb_ref[...]
    lg = (jnp.minimum(z, 0.0) - jnp.log1p(jnp.exp(-jnp.abs(z)))) / GLA_TAU

    row = lax.broadcasted_iota(jnp.int32, (2 * grp, grp), 0)
    col = lax.broadcasted_iota(jnp.int32, (2 * grp, grp), 1)
    rr = jnp.where(row < grp, row, row - grp)
    shift = c.bit_length() - 1
    same = (rr >> shift) == (col >> shift)
    lower = jnp.where(same & (col <= rr), 1.0, 0.0)
    upper = jnp.where(same & (col > rr), 1.0, 0.0)
    lu = jnp.where(row < grp, lower, upper).astype(BF16)
    b_parts, e_parts = [], []
    for g0 in range(0, sb, grp):
        pieces = jnp.concatenate(_split3(lg[g0:g0 + grp]), axis=1)
        res = _dot(lu, pieces)
        tot = res[:, :LANES] + res[:, LANES:2 * LANES] + res[:, 2 * LANES:]
        b_parts.append(tot[:grp])
        e_parts.append(tot[grp:])
    b = jnp.concatenate(b_parts, axis=0)
    brest = jnp.concatenate(e_parts, axis=0)

    qf = q_ref[0].astype(F32)
    kf = k_ref[0].astype(F32)
    q_in = (qf * jnp.exp(b)).astype(BF16)
    k_in = (kf * jnp.exp(-b)).astype(BF16)
    k_end = (kf * jnp.exp(brest)).astype(BF16)
    v = v_ref[0]

    ci = lax.broadcasted_iota(jnp.int32, (c, c), 0)
    cj = lax.broadcasted_iota(jnp.int32, (c, c), 1)
    causal = cj <= ci

    st = st_ref[...]
    outs = []
    for i in range(sb // c):
        rows = slice(i * c, (i + 1) * c)
        a = jnp.where(causal, _dot_nt(q_in[rows], k_in[rows]), 0.0).astype(BF16)
        o = _dot(a, v[rows]) + _dot_nt(q_in[rows], st.astype(BF16))
        outs.append(o)
        decay = jnp.exp(b[i * c + c - 1:i * c + c, :])
        st = st * decay + _dot_tn(v[rows], k_end[rows])
    st_ref[...] = st

    o = jnp.concatenate(outs, axis=0)
    mu = jnp.mean(o, axis=-1, keepdims=True)
    d = o - mu
    var = jnp.mean(d * d, axis=-1, keepdims=True)
    rg = r_ref[0].astype(F32)
    y = d * lax.rsqrt(var + LN_EPS) * ng_ref[...] * (rg * _sigmoid(rg))
    o_ref[0] = y.astype(o_ref.dtype)


def _gla(h3, glr3, w2p, gb, ng, sb):
    bsz, s, _ = h3.shape
    kb = GLA_HEADS * GLA_DK // LANES
    vb = 2 * GLA_HEADS * GLA_DK // GLA_DV
    rb = vb + GLA_HEADS
    return pl.pallas_call(
        _gla_kernel,
        grid=(bsz, GLA_HEADS, s // sb),
        in_specs=[pl.BlockSpec((1, sb, GLA_DK), lambda b, h, j: (b, j, h)),
                  pl.BlockSpec((1, sb, GLA_DK), lambda b, h, j: (b, j, kb + h)),
                  pl.BlockSpec((1, sb, GLA_DV), lambda b, h, j: (b, j, vb + h)),
                  pl.BlockSpec((1, sb, GLA_DV), lambda b, h, j: (b, j, rb + h)),
                  pl.BlockSpec((1, sb, LANES), lambda b, h, j: (b, j, 0)),
                  pl.BlockSpec((LANES, GLA_DK), lambda b, h, j: (0, h)),
                  pl.BlockSpec((1, GLA_DK), lambda b, h, j: (0, h)),
                  pl.BlockSpec((1, GLA_DV), lambda b, h, j: (0, h))],
        out_specs=pl.BlockSpec((1, sb, GLA_DV), lambda b, h, j: (b, j, h)),
        out_shape=jax.ShapeDtypeStruct((bsz, s, GLA_HEADS * GLA_DV), BF16),
        scratch_shapes=[pltpu.VMEM((GLA_DV, GLA_DK), F32)],
        compiler_params=_cparams("parallel", "parallel", "arbitrary"),
        name="gla",
    )(h3, h3, h3, h3, glr3, w2p, gb, ng)


def _dil_kernel(q_ref, k_ref, v_ref, o_ref, qf, kf, vf, ob, lb):
    s = q_ref.shape[1]
    band = DIL_BAND
    qf[...] = q_ref[0].astype(F32)
    kf[...] = k_ref[0].astype(F32)
    vf[...] = v_ref[0].astype(F32)

    qi = lax.broadcasted_iota(jnp.int32, (band, 2 * band), 0)
    kj = lax.broadcasted_iota(jnp.int32, (band, 2 * band), 1)
    allowed = (kj >= qi) & (kj <= qi + band)
    bias = jnp.where(allowed, 0.0, -jnp.inf).astype(F32)
    bias0 = jnp.where(allowed & (kj >= band), 0.0, -jnp.inf).astype(F32)

    for p, (window, dil) in enumerate(DIL_PATTERNS):
        assert window // dil == band
        nb = s // dil // band
        span = band * dil

        def rows(start, dil=dil):
            if dil == 1:
                return pl.ds(start, band)
            return pl.ds(start, band, stride=dil)

        def block(t, carry, p=p, dil=dil, nb=nb, span=span, rows=rows):
            r = t // nb
            n = t - r * nb
            start = r + n * span
            prev = jnp.maximum(start - span, r)
            qb = qf[rows(start), :].astype(BF16)
            k2 = jnp.concatenate([kf[rows(prev), :], kf[rows(start), :]], axis=0).astype(BF16)
            v2 = jnp.concatenate([vf[rows(prev), :], vf[rows(start), :]], axis=0).astype(BF16)
            sc = _dot_nt(qb, k2) + jnp.where(n == 0, bias0, bias)
            m = jnp.max(sc, axis=-1, keepdims=True)
            e = jnp.exp(sc - m)
            den = jnp.sum(e, axis=-1, keepdims=True)
            o = _dot(e.astype(BF16), v2) / den
            ob[p, rows(start), :] = o
            lb[p, rows(start), :] = jnp.broadcast_to(m + jnp.log(den), (band, LANES))
            return carry

        lax.fori_loop(0, dil * nb, block, 0)

    mb = 512

    def merge(i, carry):
        rs = pl.ds(pl.multiple_of(i * mb, mb), mb)
        l0, l1, l2 = lb[0, rs, :], lb[1, rs, :], lb[2, rs, :]
        m = jnp.maximum(jnp.maximum(l0, l1), l2)
        e0, e1, e2 = jnp.exp(l0 - m), jnp.exp(l1 - m), jnp.exp(l2 - m)
        num = e0 * ob[0, rs, :] + e1 * ob[1, rs, :] + e2 * ob[2, rs, :]
        o_ref[0, rs, :] = (num / (e0 + e1 + e2)).astype(o_ref.dtype)
        return carry

    lax.fori_loop(0, s // mb, merge, 0)


def _dil(h3):
    bsz, s, _ = h3.shape
    qb = (2 * GLA_HEADS * GLA_DK + 2 * GLA_HEADS * GLA_DV) // DIL_HD
    kb = qb + DIL_HEADS
    vb = kb + DIL_HEADS
    npat = len(DIL_PATTERNS)
    return pl.pallas_call(
        _dil_kernel,
        grid=(bsz, DIL_HEADS),
        in_specs=[pl.BlockSpec((1, s, DIL_HD), lambda b, h: (b, 0, qb + h)),
                  pl.BlockSpec((1, s, DIL_HD), lambda b, h: (b, 0, kb + h)),
                  pl.BlockSpec((1, s, DIL_HD), lambda b, h: (b, 0, vb + h))],
        out_specs=pl.BlockSpec((1, s, DIL_HD), lambda b, h: (b, 0, h)),
        out_shape=jax.ShapeDtypeStruct((bsz, s, DIL_HEADS * DIL_HD), BF16),
        scratch_shapes=[pltpu.VMEM((s, DIL_HD), F32), pltpu.VMEM((s, DIL_HD), F32),
                        pltpu.VMEM((s, DIL_HD), F32),
                        pltpu.VMEM((npat, s, DIL_HD), F32), pltpu.VMEM((npat, s, LANES), F32)],
        compiler_params=_cparams("parallel", "parallel"),
        name="dilated_attention",
    )(h3, h3, h3)


def _outproj_kernel(og_ref, od_ref, wa_ref, wb_ref, x_ref, g_ref, b_ref, o_ref, *, alpha):
    acc = _dot(og_ref[...], wa_ref[...]) + _dot(od_ref[...], wb_ref[...])
    o_ref[...] = _layer_norm(alpha * x_ref[...] + acc, g_ref[...], b_ref[...])


def _resident(shape):
    return pl.BlockSpec(shape, lambda *_: (0,) * len(shape), pipeline_mode=pl.Buffered(1))


def _outproj(og, od, wa, wb, x2d, g, b, bm, alpha):
    t, d = x2d.shape
    ka, kb = og.shape[1], od.shape[1]
    return pl.pallas_call(
        functools.partial(_outproj_kernel, alpha=alpha),
        grid=(t // bm,),
        in_specs=[pl.BlockSpec((bm, ka), lambda i: (i, 0)),
                  pl.BlockSpec((bm, kb), lambda i: (i, 0)),
                  _resident((ka, d)), _resident((kb, d)),
                  pl.BlockSpec((bm, d), lambda i: (i, 0)),
                  _resident((1, d)), _resident((1, d))],
        out_specs=pl.BlockSpec((bm, d), lambda i: (i, 0)),
        out_shape=jax.ShapeDtypeStruct((t, d), F32),
        compiler_params=_cparams("parallel"),
        name="out_projection_ln1",
    )(og, od, wa, wb, x2d, g, b)


def _memkv_kernel(m_ref, w_ref, o_ref):
    o_ref[...] = _dot(m_ref[...].astype(BF16), w_ref[...]).astype(o_ref.dtype)


def _memkv(mem2d, wkv, bn):
    t, d = mem2d.shape
    n = wkv.shape[1]
    return pl.pallas_call(
        _memkv_kernel,
        grid=(n // bn,),
        in_specs=[pl.BlockSpec((t, d), lambda j: (0, 0)),
                  pl.BlockSpec((d, bn), lambda j: (0, j))],
        out_specs=pl.BlockSpec((t, bn), lambda j: (0, j)),
        out_shape=jax.ShapeDtypeStruct((t, n), BF16),
        compiler_params=_cparams("parallel"),
        name="memory_kv_projection",
    )(mem2d, wkv)


def _cross_kernel(x_ref, wq_ref, kv_ref, wo_ref, g_ref, b_ref, o_ref, oc_ref, *, alpha):
    d = x_ref.shape[1]
    hd = d // CA_HEADS
    x = x_ref[...]
    q = _dot(x.astype(BF16), wq_ref[...]).astype(BF16)
    for h in range(CA_HEADS):
        cols = slice(h * hd, (h + 1) * hd)
        mk = kv_ref[0, :, cols]
        mv = kv_ref[0, :, d + h * hd:d + (h + 1) * hd]
        sc = _dot_nt(q[:, cols], mk) * (hd ** -0.5)
        m = jnp.max(sc, axis=-1, keepdims=True)
        e = jnp.exp(sc - m)
        p = e / jnp.sum(e, axis=-1, keepdims=True)
        oc_ref[:, cols] = _dot(p.astype(BF16), mv).astype(BF16)
    y = alpha * x + _dot(oc_ref[...], wo_ref[...])
    o_ref[...] = _layer_norm(y, g_ref[...], b_ref[...])


def _cross(x1, wq, kv3, wo, g, b, bm, seq, alpha):
    t, d = x1.shape
    m = kv3.shape[1]
    per = seq // bm
    return pl.pallas_call(
        functools.partial(_cross_kernel, alpha=alpha),
        grid=(t // bm,),
        in_specs=[pl.BlockSpec((bm, d), lambda i: (i, 0)),
                  _resident((d, d)),
                  pl.BlockSpec((1, m, 2 * d), lambda i: (i // per, 0, 0)),
                  _resident((d, d)),
                  _resident((1, d)), _resident((1, d))],
        out_specs=pl.BlockSpec((bm, d), lambda i: (i, 0)),
        out_shape=jax.ShapeDtypeStruct((t, d), F32),
        scratch_shapes=[pltpu.VMEM((bm, d), BF16)],
        compiler_params=_cparams("parallel"),
        name="cross_attention_ln2",
    )(x1, wq, kv3, wo, g, b)


def _causal_conv(u, cw, cb, bm):
    h = FFN_HALO
    y = cb + cw[0:1, :] * u[h - 2:h - 2 + bm]
    y = y + cw[1:2, :] * u[h - 1:h - 1 + bm]
    return y + cw[2:3, :] * u[h:h + bm]


def _ffn_kernel(x_ref, xp_ref, wg_ref, wu_ref, cwg_ref, cwu_ref, cbg_ref, cbu_ref, w2_ref,
                g_ref, b_ref, o_ref, xb_ref, acc_ref, *, alpha, per):
    i = pl.program_id(0)
    f = pl.program_id(1)
    bm = x_ref.shape[0]

    @pl.when(f == 0)
    def _():
        halo = jnp.where(i % per == 0, 0.0, xp_ref[...])
        xb_ref[:FFN_HALO, :] = halo.astype(BF16)
        xb_ref[FFN_HALO:, :] = x_ref[...].astype(BF16)
        acc_ref[...] = jnp.zeros_like(acc_ref)

    xb = xb_ref[...]
    gate = _causal_conv(_dot(xb, wg_ref[...]), cwg_ref[...], cbg_ref[...], bm)
    up = _causal_conv(_dot(xb, wu_ref[...]), cwu_ref[...], cbu_ref[...], bm)
    a = (gate * _sigmoid(gate) * up).astype(BF16)
    acc_ref[...] += _dot(a, w2_ref[...])

    @pl.when(f == pl.num_programs(1) - 1)
    def _():
        o_ref[...] = _layer_norm(alpha * x_ref[...] + acc_ref[...], g_ref[...], b_ref[...])


def _ffn(x2, wg, wu, cwg, cwu, cbg, cbu, w2, g, b, bm, seq, alpha):
    t, d = x2.shape
    dff = wg.shape[1]
    bf = FFN_CHUNK
    per = seq // bm
    hb = bm // FFN_HALO
    return pl.pallas_call(
        functools.partial(_ffn_kernel, alpha=alpha, per=per),
        grid=(t // bm, dff // bf),
        in_specs=[pl.BlockSpec((bm, d), lambda i, f: (i, 0)),
                  pl.BlockSpec((FFN_HALO, d), lambda i, f: (jnp.maximum(i * hb - 1, 0), 0)),
                  pl.BlockSpec((d, bf), lambda i, f: (0, f)),
                  pl.BlockSpec((d, bf), lambda i, f: (0, f)),
                  pl.BlockSpec((CONV_W, bf), lambda i, f: (0, f)),
                  pl.BlockSpec((CONV_W, bf), lambda i, f: (0, f)),
                  pl.BlockSpec((1, bf), lambda i, f: (0, f)),
                  pl.BlockSpec((1, bf), lambda i, f: (0, f)),
                  pl.BlockSpec((bf, d), lambda i, f: (f, 0)),
                  pl.BlockSpec((1, d), lambda i, f: (0, 0)),
                  pl.BlockSpec((1, d), lambda i, f: (0, 0))],
        out_specs=pl.BlockSpec((bm, d), lambda i, f: (i, 0)),
        out_shape=jax.ShapeDtypeStruct((t, d), F32),
        scratch_shapes=[pltpu.VMEM((bm + FFN_HALO, d), BF16), pltpu.VMEM((bm, d), F32)],
        compiler_params=_cparams("parallel", "arbitrary"),
        name="conv_ffn_ln3",
    )(x2, x2, wg, wu, cwg, cwu, cbg, cbu, w2, g, b)


def _pad_cols(a, n):
    return jnp.pad(a, ((0, 0), (0, n - a.shape[1])))


def kernel(x, mem, positions, w_in, gla_gate_w2, gla_gate_b, gla_norm_g, w_out, ln1_g, ln1_b,
           ca_wq, ca_wkv, ca_wo, ln2_g, ln2_b, ffn_w_in, ffn_conv_w, ffn_conv_b, ffn_w_out,
           ln3_g, ln3_b):
    bsz, seq, d = x.shape
    depth = w_in.shape[0]
    t = bsz * seq
    alpha = (2.0 * depth) ** 0.25
    d_ff = ffn_w_out.shape[1]
    dff_pad = -(-d_ff // FFN_CHUNK) * FFN_CHUNK

    nqk = GLA_HEADS * GLA_DK
    nv = GLA_HEADS * GLA_DV
    c_glr = 2 * nqk + 2 * nv
    c_dil = c_glr + GLA_GATE_RANK

    half = ROPE_HALF
    inv_freq = ROPE_THETA ** (-jnp.arange(0, ROPE_DIMS, 2, dtype=F32) / ROPE_DIMS)
    inv_row = jnp.concatenate([inv_freq, inv_freq, jnp.zeros((LANES - 2 * half,), F32)])[None, :]
    pos_col = positions.astype(F32).reshape(t, 1)
    cosf, sina, sinb = _rope_tables(pos_col, inv_row, min(t, 2048))

    x2d = x.reshape(t, d)
    for l in range(depth):
        wl = w_in[l]
        wcat = jnp.concatenate([wl[:, :c_glr], wl[:, c_dil:]], axis=1).astype(BF16)
        wglr = _pad_cols(wl[:, c_glr:c_dil], LANES).astype(BF16)
        h, glr = _inproj(x2d, wcat, wglr, cosf, sina, sinb, min(t, 1024))
        h3 = h.reshape(bsz, seq, h.shape[1])

        w2p = jnp.pad(gla_gate_w2[l], ((0, LANES - GLA_GATE_RANK), (0, 0))).astype(BF16)
        og = _gla(h3, glr.reshape(bsz, seq, LANES), w2p, gla_gate_b[l][None, :],
                  gla_norm_g[l][None, :], min(seq, 512))
        od = _dil(h3)

        wo = w_out[l].astype(BF16)
        x1 = _outproj(og.reshape(t, nv), od.reshape(t, DIL_HEADS * DIL_HD), wo[:nv], wo[nv:],
                      x2d, ln1_g[l][None, :], ln1_b[l][None, :], min(t, 512), alpha)

        kv = _memkv(mem.reshape(-1, d), ca_wkv[l].astype(BF16), 1024)
        x2 = _cross(x1, ca_wq[l].astype(BF16), kv.reshape(bsz, -1, 2 * d), ca_wo[l].astype(BF16),
                    ln2_g[l][None, :], ln2_b[l][None, :], min(seq, 512), seq, alpha)

        w1 = ffn_w_in[l]
        cw = ffn_conv_w[l]
        cb = ffn_conv_b[l][None, :]
        x2d = _ffn(x2,
                   _pad_cols(w1[:, :d_ff], dff_pad).astype(BF16),
                   _pad_cols(w1[:, d_ff:], dff_pad).astype(BF16),
                   _pad_cols(cw[:, :d_ff], dff_pad), _pad_cols(cw[:, d_ff:], dff_pad),
                   _pad_cols(cb[:, :d_ff], dff_pad), _pad_cols(cb[:, d_ff:], dff_pad),
                   jnp.pad(ffn_w_out[l], ((0, dff_pad - d_ff), (0, 0))).astype(BF16),
                   ln3_g[l][None, :], ln3_b[l][None, :], min(seq, 512), seq, alpha)
    return x2d.reshape(bsz, seq, d)
```

```python
import functools

import jax
import jax.numpy as jnp
from jax import lax
from jax.experimental import pallas as pl
from jax.experimental.pallas import tpu as pltpu

F32 = jnp.float32
BF16 = jnp.bfloat16

LANES = 128
LN_EPS = 1e-5
GLA_HEADS = 4
GLA_DK = 128
GLA_DV = 256
GLA_GATE_RANK = 16
GLA_TAU = 16.0
GLA_CHUNK = 64
DIL_HD = 128
DIL_HEADS = 8
DIL_PATTERNS = ((128, 1), (512, 4), (2048, 16))
DIL_BAND = 128
DIL_UNROLL = 4
ROPE_THETA = 500000.0
ROPE_DIMS = 32
ROPE_HALF = ROPE_DIMS // 2
CA_HEADS = 4
CONV_W = 3
INPROJ_SUB = 256
FFN_CHUNK = 512
FFN_SUB = 256
FFN_HALO = 16
VMEM_LIMIT = 56 * 1024 * 1024


def _cparams(*sem):
    return pltpu.CompilerParams(dimension_semantics=sem, vmem_limit_bytes=VMEM_LIMIT)


def _dot(a, b):
    return jnp.dot(a, b, preferred_element_type=F32)


def _dot_nt(a, b):
    return lax.dot_general(a, b, (((1,), (1,)), ((), ())), preferred_element_type=F32)


def _dot_tn(a, b):
    return lax.dot_general(a, b, (((0,), (0,)), ((), ())), preferred_element_type=F32)


def _layer_norm(y, g, b):
    mu = jnp.mean(y, axis=-1, keepdims=True)
    d = y - mu
    var = jnp.mean(d * d, axis=-1, keepdims=True)
    return d * lax.rsqrt(var + LN_EPS) * g + b


def _sigmoid(x):
    return 1.0 / (1.0 + jnp.exp(-x))


def _rope_kernel(pos_ref, inv_ref, cos_ref, sa_ref, sb_ref):
    ang = pos_ref[...] * inv_ref[...]
    lane = lax.broadcasted_iota(jnp.int32, ang.shape, 1)
    c = jnp.cos(ang)
    s = jnp.sin(ang)
    cos_ref[...] = jnp.where(lane < ROPE_DIMS, c, 1.0)
    sa_ref[...] = jnp.where(lane < ROPE_HALF, -s, 0.0)
    sb_ref[...] = jnp.where(lane < ROPE_HALF, 0.0, jnp.where(lane < ROPE_DIMS, s, 0.0))


def _rope_tables(pos_col, inv_row, bs):
    t = pos_col.shape[0]
    out = jax.ShapeDtypeStruct((t, LANES), F32)
    spec = pl.BlockSpec((bs, LANES), lambda i: (i, 0))
    return pl.pallas_call(
        _rope_kernel,
        grid=(t // bs,),
        in_specs=[pl.BlockSpec((bs, 1), lambda i: (i, 0)),
                  pl.BlockSpec((1, LANES), lambda i: (0, 0))],
        out_specs=[spec, spec, spec],
        out_shape=[out, out, out],
        compiler_params=_cparams("parallel"),
        name="rope_tables",
    )(pos_col, inv_row)


def _rope(t, cos, sa, sb):
    return t * cos + pltpu.roll(t, LANES - ROPE_HALF, 1) * sa + pltpu.roll(t, ROPE_HALF, 1) * sb


def _inproj_kernel(x_ref, w_ref, wg_ref, cos_ref, sa_ref, sb_ref, h_ref, glr_ref, xb_ref):
    n = pl.program_id(1)
    bn = w_ref.shape[1]

    def project(epilogue):
        for c0 in range(0, bn, INPROJ_SUB):
            acc = _dot(xb_ref[...], w_ref[:, c0:c0 + INPROJ_SUB])
            h_ref[:, c0:c0 + INPROJ_SUB] = epilogue(acc, c0).astype(h_ref.dtype)

    def rope(scale):
        def epilogue(acc, c0):
            cos, sa, sb = cos_ref[...], sa_ref[...], sb_ref[...]
            heads = [acc[:, j:j + LANES] for j in range(0, acc.shape[1], LANES)]
            if scale is not None:
                heads = [t * scale for t in heads]
            return jnp.concatenate([_rope(t, cos, sa, sb) for t in heads], axis=1)
        return epilogue

    @pl.when(n == 0)
    def _():
        xb_ref[...] = x_ref[...].astype(BF16)
        glr_ref[...] = _dot(xb_ref[...], wg_ref[...])
        project(lambda acc, c0: acc * (GLA_DK ** -0.5) if c0 < bn // 2 else acc)

    @pl.when((n == 1) | (n == 2) | (n == 5))
    def _():
        project(lambda acc, c0: acc)

    @pl.when(n == 3)
    def _():
        project(rope(DIL_HD ** -0.5))

    @pl.when(n == 4)
    def _():
        project(rope(None))


def _inproj(x2d, wcat, wglr, cosf, sina, sinb, bm):
    t, d = x2d.shape
    ncols = wcat.shape[1]
    bn = 1024
    tab = pl.BlockSpec((bm, LANES), lambda i, j: (i, 0))
    return pl.pallas_call(
        _inproj_kernel,
        grid=(t // bm, ncols // bn),
        in_specs=[pl.BlockSpec((bm, d), lambda i, j: (i, 0)),
                  pl.BlockSpec((d, bn), lambda i, j: (0, j)),
                  pl.BlockSpec((d, LANES), lambda i, j: (0, 0)),
                  tab, tab, tab],
        out_specs=[pl.BlockSpec((bm, bn), lambda i, j: (i, j)),
                   pl.BlockSpec((bm, LANES), lambda i, j: (i, 0))],
        out_shape=[jax.ShapeDtypeStruct((t, ncols), BF16),
                   jax.ShapeDtypeStruct((t, LANES), F32)],
        scratch_shapes=[pltpu.VMEM((bm, d), BF16)],
        compiler_params=_cparams("parallel", "arbitrary"),
        name="in_projection",
    )(x2d, wcat, wglr, cosf, sina, sinb)


def _split3(v):
    hi = v.astype(BF16)
    r1 = v - hi.astype(F32)
    mid = r1.astype(BF16)
    lo = (r1 - mid.astype(F32)).astype(BF16)
    return hi, mid, lo


def _gla_kernel(q_ref, k_ref, v_ref, r_ref, glr_ref, w2_ref, gb_ref, ng_ref, o_ref, st_ref):
    c = GLA_CHUNK
    sb = q_ref.shape[1]
    grp = 4 * c

    @pl.when(pl.program_id(2) == 0)
    def _():
        st_ref[...] = jnp.zeros_like(st_ref)

    z = _dot(glr_ref[0].astype(BF16), w2_ref[...]) + gb_ref[...]
    lg = (jnp.minimum(z, 0.0) - jnp.log1p(jnp.exp(-jnp.abs(z)))) / GLA_TAU

    row = lax.broadcasted_iota(jnp.int32, (2 * grp, grp), 0)
    col = lax.broadcasted_iota(jnp.int32, (2 * grp, grp), 1)
    rr = jnp.where(row < grp, row, row - grp)
    shift = c.bit_length() - 1
    same = (rr >> shift) == (col >> shift)
    lower = jnp.where(same & (col <= rr), 1.0, 0.0)
    upper = jnp.where(same & (col > rr), 1.0, 0.0)
    lu = jnp.where(row < grp, lower, upper).astype(BF16)
    b_parts, e_parts = [], []
    for g0 in range(0, sb, grp):
        pieces = jnp.concatenate(_split3(lg[g0:g0 + grp]), axis=1)
        res = _dot(lu, pieces)
        tot = res[:, :LANES] + res[:, LANES:2 * LANES] + res[:, 2 * LANES:]
        b_parts.append(tot[:grp])
        e_parts.append(tot[grp:])
    b = jnp.concatenate(b_parts, axis=0)
    brest = jnp.concatenate(e_parts, axis=0)

    qf = q_ref[0].astype(F32)
    kf = k_ref[0].astype(F32)
    q_in = (qf * jnp.exp(b)).astype(BF16)
    k_in = (kf * jnp.exp(-b)).astype(BF16)
    k_end = (kf * jnp.exp(brest)).astype(BF16)
    v = v_ref[0]

    ci = lax.broadcasted_iota(jnp.int32, (c, c), 0)
    cj = lax.broadcasted_iota(jnp.int32, (c, c), 1)
    causal = cj <= ci

    st = st_ref[...]
    outs = []
    for i in range(sb // c):
        rows = slice(i * c, (i + 1) * c)
        a = jnp.where(causal, _dot_nt(q_in[rows], k_in[rows]), 0.0).astype(BF16)
        o = _dot(a, v[rows]) + _dot_nt(q_in[rows], st.astype(BF16))
        outs.append(o)
        decay = jnp.exp(b[i * c + c - 1:i * c + c, :])
        st = st * decay + _dot_tn(v[rows], k_end[rows])
    st_ref[...] = st

    o = jnp.concatenate(outs, axis=0)
    mu = jnp.mean(o, axis=-1, keepdims=True)
    d = o - mu
    var = jnp.mean(d * d, axis=-1, keepdims=True)
    rg = r_ref[0].astype(F32)
    y = d * lax.rsqrt(var + LN_EPS) * ng_ref[...] * (rg * _sigmoid(rg))
    o_ref[0] = y.astype(o_ref.dtype)


def _gla(h3, glr3, w2p, gb, ng, sb):
    bsz, s, _ = h3.shape
    kb = GLA_HEADS * GLA_DK // LANES
    vb = 2 * GLA_HEADS * GLA_DK // GLA_DV
    rb = vb + GLA_HEADS
    return pl.pallas_call(
        _gla_kernel,
        grid=(bsz, GLA_HEADS, s // sb),
        in_specs=[pl.BlockSpec((1, sb, GLA_DK), lambda b, h, j: (b, j, h)),
                  pl.BlockSpec((1, sb, GLA_DK), lambda b, h, j: (b, j, kb + h)),
                  pl.BlockSpec((1, sb, GLA_DV), lambda b, h, j: (b, j, vb + h)),
                  pl.BlockSpec((1, sb, GLA_DV), lambda b, h, j: (b, j, rb + h)),
                  pl.BlockSpec((1, sb, LANES), lambda b, h, j: (b, j, 0)),
                  pl.BlockSpec((LANES, GLA_DK), lambda b, h, j: (0, h)),
                  pl.BlockSpec((1, GLA_DK), lambda b, h, j: (0, h)),
                  pl.BlockSpec((1, GLA_DV), lambda b, h, j: (0, h))],
        out_specs=pl.BlockSpec((1, sb, GLA_DV), lambda b, h, j: (b, j, h)),
        out_shape=jax.ShapeDtypeStruct((bsz, s, GLA_HEADS * GLA_DV), BF16),
        scratch_shapes=[pltpu.VMEM((GLA_DV, GLA_DK), F32)],
        compiler_params=_cparams("parallel", "parallel", "arbitrary"),
        name="gla",
    )(h3, h3, h3, h3, glr3, w2p, gb, ng)


def _dil_kernel(q_ref, k_ref, v_ref, o_ref, qf, kf, vf, qg, kg, vg, qc, kc, vc, ob, db, mx,
                scb, eb):
    s = q_ref.shape[1]
    band = DIL_BAND
    unroll = DIL_UNROLL
    nblk = s // band
    qf[...] = q_ref[0].astype(F32)
    kf[...] = k_ref[0].astype(F32)
    vf[...] = v_ref[0].astype(F32)
    kc[:band, :] = jnp.zeros((band, DIL_HD), BF16)
    vc[:band, :] = jnp.zeros((band, DIL_HD), BF16)

    qi = lax.broadcasted_iota(jnp.int32, (band, 2 * band), 0)
    kj = lax.broadcasted_iota(jnp.int32, (band, 2 * band), 1)
    allowed = (kj >= qi) & (kj <= qi + band)
    bias = jnp.where(allowed, 0.0, -jnp.inf).astype(F32)
    bias0 = jnp.where(allowed & (kj >= band), 0.0, -jnp.inf).astype(F32)
    ones = jnp.ones((2 * band, LANES), BF16)

    for p, (window, dil) in enumerate(DIL_PATTERNS):
        assert window // dil == band
        cls = s // dil
        nb = cls // band
        span = band * dil
        assert nblk % unroll == 0 and (nb % unroll == 0 or unroll % nb == 0)

        if dil == 1:
            qc[...] = q_ref[0]
            kc[band:, :] = k_ref[0]
            vc[band:, :] = v_ref[0]
        else:
            prev = DIL_PATTERNS[p - 1][1]
            step = dil // prev
            assert step * prev == dil and step in (2, 4)
            keep = p + 1 < len(DIL_PATTERNS)
            srcs, dsts = ((qf, kf, vf), (qg, kg, vg)) if p % 2 == 1 else ((qg, kg, vg), (qf, kf, vf))
            for r in range(dil):
                rows = pl.ds((r % prev) * (s // prev) + r // prev, cls, stride=step)
                for src, dst, cm, off in zip(srcs, dsts, (qc, kc, vc), (0, band, band)):
                    x = src[rows, :]
                    if keep:
                        dst[r * cls:(r + 1) * cls, :] = x
                    cm[off + r * cls:off + (r + 1) * cls, :] = x.astype(BF16)

        def out_rows(g, lo=0, cnt=band, dil=dil, nb=nb, span=span):
            start = g // nb + (g % nb) * span + lo * dil
            return pl.ds(start, cnt) if dil == 1 else pl.ds(start, cnt, stride=dil)

        def scores(t, slot):
            for u in range(unroll):
                g = t * unroll + u
                k2 = kc[g * band:(g + 2) * band, :]
                scb[slot, u * band:(u + 1) * band, :] = _dot_nt(qc[g * band:(g + 1) * band, :], k2)

        def softmax(t, slot, p=p, nb=nb, out_rows=out_rows):
            for u in range(unroll):
                g = t * unroll + u
                bb = bias0 if g % nb == 0 else bias
                rows = slice(u * band, (u + 1) * band)
                m = jnp.max(scb[slot, rows, :] + bb, axis=-1, keepdims=True)
                mx[p, out_rows(g), :] = jnp.broadcast_to(m, (band, LANES))
                for half in range(2):
                    cols = slice(half * band, (half + 1) * band)
                    eb[slot, rows, cols] = jnp.exp(scb[slot, rows, cols] + bb[:, cols] - m).astype(BF16)

        def values(t, slot, p=p, out_rows=out_rows):
            for u in range(unroll):
                g = t * unroll + u
                v2 = vc[g * band:(g + 2) * band, :]
                oe = _dot(eb[slot, u * band:(u + 1) * band, :],
                          jnp.concatenate([v2, ones], axis=1))
                ob[p, out_rows(g), :] = oe[:, :DIL_HD]
                db[p, out_rows(g), :] = oe[:, DIL_HD:]

        ngrp = nblk // unroll
        for t in range(ngrp + 2):
            if t < ngrp:
                scores(t, t % 2)
            if 1 <= t <= ngrp:
                softmax(t - 1, (t - 1) % 2)
            if t >= 2:
                values(t - 2, t % 2)

    mb = 512

    def merge(i, carry):
        rs = pl.ds(pl.multiple_of(i * mb, mb), mb)
        m0, m1, m2 = mx[0, rs, :], mx[1, rs, :], mx[2, rs, :]
        m = jnp.maximum(jnp.maximum(m0, m1), m2)
        e0, e1, e2 = jnp.exp(m0 - m), jnp.exp(m1 - m), jnp.exp(m2 - m)
        num = e0 * ob[0, rs, :] + e1 * ob[1, rs, :] + e2 * ob[2, rs, :]
        den = e0 * db[0, rs, :] + e1 * db[1, rs, :] + e2 * db[2, rs, :]
        o_ref[0, rs, :] = (num / den).astype(o_ref.dtype)
        return carry

    lax.fori_loop(0, s // mb, merge, 0)


def _dil(h3):
    bsz, s, _ = h3.shape
    qb = (2 * GLA_HEADS * GLA_DK + 2 * GLA_HEADS * GLA_DV) // DIL_HD
    kb = qb + DIL_HEADS
    vb = kb + DIL_HEADS
    npat = len(DIL_PATTERNS)
    return pl.pallas_call(
        _dil_kernel,
        grid=(bsz, DIL_HEADS),
        in_specs=[pl.BlockSpec((1, s, DIL_HD), lambda b, h: (b, 0, qb + h)),
                  pl.BlockSpec((1, s, DIL_HD), lambda b, h: (b, 0, kb + h)),
                  pl.BlockSpec((1, s, DIL_HD), lambda b, h: (b, 0, vb + h))],
        out_specs=pl.BlockSpec((1, s, DIL_HD), lambda b, h: (b, 0, h)),
        out_shape=jax.ShapeDtypeStruct((bsz, s, DIL_HEADS * DIL_HD), BF16),
        scratch_shapes=[pltpu.VMEM((s, DIL_HD), F32)] * 6 + [
                        pltpu.VMEM((s, DIL_HD), BF16), pltpu.VMEM((s + DIL_BAND, DIL_HD), BF16),
                        pltpu.VMEM((s + DIL_BAND, DIL_HD), BF16),
                        pltpu.VMEM((npat, s, DIL_HD), F32), pltpu.VMEM((npat, s, LANES), F32),
                        pltpu.VMEM((npat, s, LANES), F32),
                        pltpu.VMEM((2, DIL_UNROLL * DIL_BAND, 2 * DIL_BAND), F32),
                        pltpu.VMEM((2, DIL_UNROLL * DIL_BAND, 2 * DIL_BAND), BF16)],
        compiler_params=_cparams("parallel", "parallel"),
        name="dilated_attention",
    )(h3, h3, h3)


def _outproj_kernel(og_ref, od_ref, wa_ref, wb_ref, x_ref, g_ref, b_ref, o_ref, *, alpha):
    acc = _dot(og_ref[...], wa_ref[...]) + _dot(od_ref[...], wb_ref[...])
    o_ref[...] = _layer_norm(alpha * x_ref[...] + acc, g_ref[...], b_ref[...])


def _resident(shape):
    return pl.BlockSpec(shape, lambda *_: (0,) * len(shape), pipeline_mode=pl.Buffered(1))


def _outproj(og, od, wa, wb, x2d, g, b, bm, alpha):
    t, d = x2d.shape
    ka, kb = og.shape[1], od.shape[1]
    return pl.pallas_call(
        functools.partial(_outproj_kernel, alpha=alpha),
        grid=(t // bm,),
        in_specs=[pl.BlockSpec((bm, ka), lambda i: (i, 0)),
                  pl.BlockSpec((bm, kb), lambda i: (i, 0)),
                  _resident((ka, d)), _resident((kb, d)),
                  pl.BlockSpec((bm, d), lambda i: (i, 0)),
                  _resident((1, d)), _resident((1, d))],
        out_specs=pl.BlockSpec((bm, d), lambda i: (i, 0)),
        out_shape=jax.ShapeDtypeStruct((t, d), F32),
        compiler_params=_cparams("parallel"),
        name="out_projection_ln1",
    )(og, od, wa, wb, x2d, g, b)


def _memkv_kernel(m_ref, w_ref, o_ref):
    o_ref[...] = _dot(m_ref[...].astype(BF16), w_ref[...]).astype(o_ref.dtype)


def _memkv(mem2d, wkv, bn):
    t, d = mem2d.shape
    n = wkv.shape[1]
    return pl.pallas_call(
        _memkv_kernel,
        grid=(n // bn,),
        in_specs=[pl.BlockSpec((t, d), lambda j: (0, 0)),
                  pl.BlockSpec((d, bn), lambda j: (0, j))],
        out_specs=pl.BlockSpec((t, bn), lambda j: (0, j)),
        out_shape=jax.ShapeDtypeStruct((t, n), BF16),
        compiler_params=_cparams("parallel"),
        name="memory_kv_projection",
    )(mem2d, wkv)


def _cross_kernel(x_ref, wq_ref, kv_ref, wo_ref, g_ref, b_ref, o_ref, oc_ref, *, alpha):
    d = x_ref.shape[1]
    hd = d // CA_HEADS
    x = x_ref[...]
    q = _dot(x.astype(BF16), wq_ref[...]).astype(BF16)
    for h in range(CA_HEADS):
        cols = slice(h * hd, (h + 1) * hd)
        mk = kv_ref[0, :, cols]
        mv = kv_ref[0, :, d + h * hd:d + (h + 1) * hd]
        sc = _dot_nt(q[:, cols], mk) * (hd ** -0.5)
        m = jnp.max(sc, axis=-1, keepdims=True)
        e = jnp.exp(sc - m)
        p = e / jnp.sum(e, axis=-1, keepdims=True)
        oc_ref[:, cols] = _dot(p.astype(BF16), mv).astype(BF16)
    y = alpha * x + _dot(oc_ref[...], wo_ref[...])
    o_ref[...] = _layer_norm(y, g_ref[...], b_ref[...])


def _cross(x1, wq, kv3, wo, g, b, bm, seq, alpha):
    t, d = x1.shape
    m = kv3.shape[1]
    per = seq // bm
    return pl.pallas_call(
        functools.partial(_cross_kernel, alpha=alpha),
        grid=(t // bm,),
        in_specs=[pl.BlockSpec((bm, d), lambda i: (i, 0)),
                  _resident((d, d)),
                  pl.BlockSpec((1, m, 2 * d), lambda i: (i // per, 0, 0)),
                  _resident((d, d)),
                  _resident((1, d)), _resident((1, d))],
        out_specs=pl.BlockSpec((bm, d), lambda i: (i, 0)),
        out_shape=jax.ShapeDtypeStruct((t, d), F32),
        scratch_shapes=[pltpu.VMEM((bm, d), BF16)],
        compiler_params=_cparams("parallel"),
        name="cross_attention_ln2",
    )(x1, wq, kv3, wo, g, b)


def _causal_conv(u, cw, cb, bm):
    h = FFN_HALO
    y = cb + cw[0:1, :] * u[h - 2:h - 2 + bm]
    y = y + cw[1:2, :] * u[h - 1:h - 1 + bm]
    return y + cw[2:3, :] * u[h:h + bm]


def _ffn_kernel(x_ref, xp_ref, wg_ref, wu_ref, cwg_ref, cwu_ref, cbg_ref, cbu_ref, w2_ref,
                g_ref, b_ref, o_ref, xb_ref, *, alpha, per):
    i = pl.program_id(0)
    f = pl.program_id(1)
    bm = x_ref.shape[0]

    @pl.when(f == 0)
    def _():
        halo = jnp.where(i % per == 0, 0.0, xp_ref[...])
        xb_ref[:FFN_HALO, :] = halo.astype(BF16)
        xb_ref[FFN_HALO:, :] = x_ref[...].astype(BF16)
        o_ref[...] = jnp.zeros_like(o_ref)

    xb = xb_ref[...]
    acts = []
    for c0 in range(0, wg_ref.shape[1], FFN_SUB):
        cols = slice(c0, c0 + FFN_SUB)
        gate = _causal_conv(_dot(xb, wg_ref[:, cols]), cwg_ref[:, cols], cbg_ref[:, cols], bm)
        up = _causal_conv(_dot(xb, wu_ref[:, cols]), cwu_ref[:, cols], cbu_ref[:, cols], bm)
        acts.append((gate * _sigmoid(gate) * up).astype(BF16))
    act = jnp.concatenate(acts, axis=1)
    half = o_ref.shape[1] // 2
    for n0 in (0, half):
        o_ref[:, n0:n0 + half] += _dot(act, w2_ref[:, n0:n0 + half])

    @pl.when(f == pl.num_programs(1) - 1)
    def _():
        o_ref[...] = _layer_norm(alpha * x_ref[...] + o_ref[...], g_ref[...], b_ref[...])


def _ffn(x2, wg, wu, cwg, cwu, cbg, cbu, w2, g, b, bm, seq, alpha):
    t, d = x2.shape
    dff = wg.shape[1]
    bf = FFN_CHUNK
    per = seq // bm
    hb = bm // FFN_HALO
    return pl.pallas_call(
        functools.partial(_ffn_kernel, alpha=alpha, per=per),
        grid=(t // bm, dff // bf),
        in_specs=[pl.BlockSpec((bm, d), lambda i, f: (i, 0), pipeline_mode=pl.Buffered(1)),
                  pl.BlockSpec((FFN_HALO, d), lambda i, f: (jnp.maximum(i * hb - 1, 0), 0)),
                  pl.BlockSpec((d, bf), lambda i, f: (0, f)),
                  pl.BlockSpec((d, bf), lambda i, f: (0, f)),
                  pl.BlockSpec((CONV_W, bf), lambda i, f: (0, f)),
                  pl.BlockSpec((CONV_W, bf), lambda i, f: (0, f)),
                  pl.BlockSpec((1, bf), lambda i, f: (0, f)),
                  pl.BlockSpec((1, bf), lambda i, f: (0, f)),
                  pl.BlockSpec((bf, d), lambda i, f: (f, 0)),
                  pl.BlockSpec((1, d), lambda i, f: (0, 0)),
                  pl.BlockSpec((1, d), lambda i, f: (0, 0))],
        out_specs=pl.BlockSpec((bm, d), lambda i, f: (i, 0)),
        out_shape=jax.ShapeDtypeStruct((t, d), F32),
        scratch_shapes=[pltpu.VMEM((bm + FFN_HALO, d), BF16)],
        compiler_params=_cparams("parallel", "arbitrary"),
        name="conv_ffn_ln3",
    )(x2, x2, wg, wu, cwg, cwu, cbg, cbu, w2, g, b)


def _pad_cols(a, n):
    return jnp.pad(a, ((0, 0), (0, n - a.shape[1])))


def kernel(x, mem, positions, w_in, gla_gate_w2, gla_gate_b, gla_norm_g, w_out, ln1_g, ln1_b,
           ca_wq, ca_wkv, ca_wo, ln2_g, ln2_b, ffn_w_in, ffn_conv_w, ffn_conv_b, ffn_w_out,
           ln3_g, ln3_b):
    bsz, seq, d = x.shape
    depth = w_in.shape[0]
    t = bsz * seq
    alpha = (2.0 * depth) ** 0.25
    d_ff = ffn_w_out.shape[1]
    dff_pad = -(-d_ff // FFN_CHUNK) * FFN_CHUNK

    nqk = GLA_HEADS * GLA_DK
    nv = GLA_HEADS * GLA_DV
    c_glr = 2 * nqk + 2 * nv
    c_dil = c_glr + GLA_GATE_RANK

    half = ROPE_HALF
    inv_freq = ROPE_THETA ** (-jnp.arange(0, ROPE_DIMS, 2, dtype=F32) / ROPE_DIMS)
    inv_row = jnp.concatenate([inv_freq, inv_freq, jnp.zeros((LANES - 2 * half,), F32)])[None, :]
    pos_col = positions.astype(F32).reshape(t, 1)
    cosf, sina, sinb = _rope_tables(pos_col, inv_row, min(t, 2048))

    x2d = x.reshape(t, d)
    for l in range(depth):
        wl = w_in[l]
        wcat = jnp.concatenate([wl[:, :c_glr], wl[:, c_dil:]], axis=1).astype(BF16)
        wglr = _pad_cols(wl[:, c_glr:c_dil], LANES).astype(BF16)
        h, glr = _inproj(x2d, wcat, wglr, cosf, sina, sinb, min(t, 1024))
        h3 = h.reshape(bsz, seq, h.shape[1])

        w2p = jnp.pad(gla_gate_w2[l], ((0, LANES - GLA_GATE_RANK), (0, 0))).astype(BF16)
        og = _gla(h3, glr.reshape(bsz, seq, LANES), w2p, gla_gate_b[l][None, :],
                  gla_norm_g[l][None, :], min(seq, 512))
        od = _dil(h3)

        wo = w_out[l].astype(BF16)
        x1 = _outproj(og.reshape(t, nv), od.reshape(t, DIL_HEADS * DIL_HD), wo[:nv], wo[nv:],
                      x2d, ln1_g[l][None, :], ln1_b[l][None, :], min(t, 512), alpha)

        kv = _memkv(mem.reshape(-1, d), ca_wkv[l].astype(BF16), 1024)
        x2 = _cross(x1, ca_wq[l].astype(BF16), kv.reshape(bsz, -1, 2 * d), ca_wo[l].astype(BF16),
                    ln2_g[l][None, :], ln2_b[l][None, :], min(seq, 512), seq, alpha)

        w1 = ffn_w_in[l]
        cw = ffn_conv_w[l]
        cb = ffn_conv_b[l][None, :]
        x2d = _ffn(x2,
                   _pad_cols(w1[:, :d_ff], dff_pad).astype(BF16),
                   _pad_cols(w1[:, d_ff:], dff_pad).astype(BF16),
                   _pad_cols(cw[:, :d_ff], dff_pad), _pad_cols(cw[:, d_ff:], dff_pad),
                   _pad_cols(cb[:, :d_ff], dff_pad), _pad_cols(cb[:, d_ff:], dff_pad),
                   jnp.pad(ffn_w_out[l], ((0, dff_pad - d_ff), (0, 0))).astype(BF16),
                   ln3_g[l][None, :], ln3_b[l][None, :], min(seq, 1024), seq, alpha)
    return x2d.reshape(bsz, seq, d)
```

```python
import functools

import jax
import jax.numpy as jnp
from jax import lax
from jax.experimental import pallas as pl
from jax.experimental.pallas import tpu as pltpu

F32 = jnp.float32
BF16 = jnp.bfloat16

LANES = 128
LN_EPS = 1e-5
GLA_HEADS = 4
GLA_DK = 128
GLA_DV = 256
GLA_GATE_RANK = 16
GLA_TAU = 16.0
GLA_CHUNK = 64
DIL_HD = 128
DIL_HEADS = 8
DIL_PATTERNS = ((128, 1), (512, 4), (2048, 16))
DIL_BAND = 128
DIL_UNROLL = 4
ROPE_THETA = 500000.0
ROPE_DIMS = 32
ROPE_HALF = ROPE_DIMS // 2
CA_HEADS = 4
CONV_W = 3
INPROJ_BLOCK = 1024
INPROJ_SUB = 256
FFN_CHUNK = 512
FFN_SUB = 256
FFN_HALO = 16
VMEM_LIMIT = 56 * 1024 * 1024


def _cparams(*sem):
    return pltpu.CompilerParams(dimension_semantics=sem, vmem_limit_bytes=VMEM_LIMIT)


def _dot(a, b):
    return jnp.dot(a, b, preferred_element_type=F32)


def _dot_nt(a, b):
    return lax.dot_general(a, b, (((1,), (1,)), ((), ())), preferred_element_type=F32)


def _dot_tn(a, b):
    return lax.dot_general(a, b, (((0,), (0,)), ((), ())), preferred_element_type=F32)


def _layer_norm(y, g, b):
    mu = jnp.mean(y, axis=-1, keepdims=True)
    d = y - mu
    var = jnp.mean(d * d, axis=-1, keepdims=True)
    return d * lax.rsqrt(var + LN_EPS) * g + b


def _sigmoid(x):
    return 1.0 / (1.0 + jnp.exp(-x))


def _rope_kernel(pos_ref, inv_ref, cos_ref, sa_ref, sb_ref):
    ang = pos_ref[...] * inv_ref[...]
    lane = lax.broadcasted_iota(jnp.int32, ang.shape, 1)
    c = jnp.cos(ang)
    s = jnp.sin(ang)
    cos_ref[...] = jnp.where(lane < ROPE_DIMS, c, 1.0)
    sa_ref[...] = jnp.where(lane < ROPE_HALF, -s, 0.0)
    sb_ref[...] = jnp.where(lane < ROPE_HALF, 0.0, jnp.where(lane < ROPE_DIMS, s, 0.0))


def _rope_tables(pos_col, inv_row, bs):
    t = pos_col.shape[0]
    out = jax.ShapeDtypeStruct((t, LANES), F32)
    spec = pl.BlockSpec((bs, LANES), lambda i: (i, 0))
    return pl.pallas_call(
        _rope_kernel,
        grid=(t // bs,),
        in_specs=[pl.BlockSpec((bs, 1), lambda i: (i, 0)),
                  pl.BlockSpec((1, LANES), lambda i: (0, 0))],
        out_specs=[spec, spec, spec],
        out_shape=[out, out, out],
        compiler_params=_cparams("parallel"),
        name="rope_tables",
    )(pos_col, inv_row)


def _rope(t, cos, sa, sb):
    return t * cos + pltpu.roll(t, LANES - ROPE_HALF, 1) * sa + pltpu.roll(t, ROPE_HALF, 1) * sb


def _inproj_kernel(x_ref, w_ref, wg_ref, cos_ref, sa_ref, sb_ref, h_ref, glr_ref, xb_ref):
    bn = INPROJ_BLOCK
    xb_ref[...] = x_ref[...].astype(BF16)
    glr_ref[...] = _dot(xb_ref[...], wg_ref[...])

    def rope(scale):
        def epilogue(acc, c0):
            cos, sa, sb = cos_ref[...], sa_ref[...], sb_ref[...]
            heads = [acc[:, j:j + LANES] for j in range(0, acc.shape[1], LANES)]
            if scale is not None:
                heads = [t * scale for t in heads]
            return jnp.concatenate([_rope(t, cos, sa, sb) for t in heads], axis=1)
        return epilogue

    def plain(acc, c0):
        return acc

    epilogues = (lambda acc, c0: acc * (GLA_DK ** -0.5) if c0 < bn // 2 else acc,
                 plain, plain, rope(DIL_HD ** -0.5), rope(None), plain)
    assert len(epilogues) * bn == w_ref.shape[1]

    for n, epilogue in enumerate(epilogues):
        for c0 in range(0, bn, INPROJ_SUB):
            cols = slice(n * bn + c0, n * bn + c0 + INPROJ_SUB)
            h_ref[:, cols] = epilogue(_dot(xb_ref[...], w_ref[:, cols]), c0).astype(h_ref.dtype)


def _inproj(x2d, wcat, wglr, cosf, sina, sinb, bm):
    t, d = x2d.shape
    ncols = wcat.shape[1]
    tab = pl.BlockSpec((bm, LANES), lambda i: (i, 0))
    return pl.pallas_call(
        _inproj_kernel,
        grid=(t // bm,),
        in_specs=[pl.BlockSpec((bm, d), lambda i: (i, 0)),
                  _resident((d, ncols)), _resident((d, LANES)), tab, tab, tab],
        out_specs=[pl.BlockSpec((bm, ncols), lambda i: (i, 0)),
                   pl.BlockSpec((bm, LANES), lambda i: (i, 0))],
        out_shape=[jax.ShapeDtypeStruct((t, ncols), BF16),
                   jax.ShapeDtypeStruct((t, LANES), F32)],
        scratch_shapes=[pltpu.VMEM((bm, d), BF16)],
        compiler_params=_cparams("parallel"),
        name="in_projection",
    )(x2d, wcat, wglr, cosf, sina, sinb)


def _split3(v):
    hi = v.astype(BF16)
    r1 = v - hi.astype(F32)
    mid = r1.astype(BF16)
    lo = (r1 - mid.astype(F32)).astype(BF16)
    return hi, mid, lo


def _gla_kernel(q_ref, k_ref, v_ref, r_ref, glr_ref, w2_ref, gb_ref, ng_ref, o_ref, st_ref):
    c = GLA_CHUNK
    sb = q_ref.shape[1]
    grp = 4 * c

    @pl.when(pl.program_id(2) == 0)
    def _():
        st_ref[...] = jnp.zeros_like(st_ref)

    z = _dot(glr_ref[0].astype(BF16), w2_ref[...]) + gb_ref[...]
    lg = (jnp.minimum(z, 0.0) - jnp.log1p(jnp.exp(-jnp.abs(z)))) / GLA_TAU

    row = lax.broadcasted_iota(jnp.int32, (2 * grp, grp), 0)
    col = lax.broadcasted_iota(jnp.int32, (2 * grp, grp), 1)
    rr = jnp.where(row < grp, row, row - grp)
    shift = c.bit_length() - 1
    same = (rr >> shift) == (col >> shift)
    lower = jnp.where(same & (col <= rr), 1.0, 0.0)
    upper = jnp.where(same & (col > rr), 1.0, 0.0)
    lu = jnp.where(row < grp, lower, upper).astype(BF16)
    b_parts, e_parts = [], []
    for g0 in range(0, sb, grp):
        pieces = jnp.concatenate(_split3(lg[g0:g0 + grp]), axis=1)
        res = _dot(lu, pieces)
        tot = res[:, :LANES] + res[:, LANES:2 * LANES] + res[:, 2 * LANES:]
        b_parts.append(tot[:grp])
        e_parts.append(tot[grp:])
    b = jnp.concatenate(b_parts, axis=0)
    brest = jnp.concatenate(e_parts, axis=0)

    qf = q_ref[0].astype(F32)
    kf = k_ref[0].astype(F32)
    q_in = (qf * jnp.exp(b)).astype(BF16)
    k_in = (kf * jnp.exp(-b)).astype(BF16)
    k_end = (kf * jnp.exp(brest)).astype(BF16)
    v = v_ref[0]

    ci = lax.broadcasted_iota(jnp.int32, (c, c), 0)
    cj = lax.broadcasted_iota(jnp.int32, (c, c), 1)
    causal = cj <= ci

    st = st_ref[...]
    outs = []
    for i in range(sb // c):
        rows = slice(i * c, (i + 1) * c)
        a = jnp.where(causal, _dot_nt(q_in[rows], k_in[rows]), 0.0).astype(BF16)
        o = _dot(a, v[rows]) + _dot_nt(q_in[rows], st.astype(BF16))
        outs.append(o)
        decay = jnp.exp(b[i * c + c - 1:i * c + c, :])
        st = st * decay + _dot_tn(v[rows], k_end[rows])
    st_ref[...] = st

    o = jnp.concatenate(outs, axis=0)
    mu = jnp.mean(o, axis=-1, keepdims=True)
    d = o - mu
    var = jnp.mean(d * d, axis=-1, keepdims=True)
    rg = r_ref[0].astype(F32)
    y = d * lax.rsqrt(var + LN_EPS) * ng_ref[...] * (rg * _sigmoid(rg))
    o_ref[0] = y.astype(o_ref.dtype)


def _gla(h3, glr3, w2p, gb, ng, sb):
    bsz, s, _ = h3.shape
    kb = GLA_HEADS * GLA_DK // LANES
    vb = 2 * GLA_HEADS * GLA_DK // GLA_DV
    rb = vb + GLA_HEADS
    return pl.pallas_call(
        _gla_kernel,
        grid=(bsz, GLA_HEADS, s // sb),
        in_specs=[pl.BlockSpec((1, sb, GLA_DK), lambda b, h, j: (b, j, h)),
                  pl.BlockSpec((1, sb, GLA_DK), lambda b, h, j: (b, j, kb + h)),
                  pl.BlockSpec((1, sb, GLA_DV), lambda b, h, j: (b, j, vb + h)),
                  pl.BlockSpec((1, sb, GLA_DV), lambda b, h, j: (b, j, rb + h)),
                  pl.BlockSpec((1, sb, LANES), lambda b, h, j: (b, j, 0)),
                  pl.BlockSpec((LANES, GLA_DK), lambda b, h, j: (0, h)),
                  pl.BlockSpec((1, GLA_DK), lambda b, h, j: (0, h)),
                  pl.BlockSpec((1, GLA_DV), lambda b, h, j: (0, h))],
        out_specs=pl.BlockSpec((1, sb, GLA_DV), lambda b, h, j: (b, j, h)),
        out_shape=jax.ShapeDtypeStruct((bsz, s, GLA_HEADS * GLA_DV), BF16),
        scratch_shapes=[pltpu.VMEM((GLA_DV, GLA_DK), F32)],
        compiler_params=_cparams("parallel", "parallel", "arbitrary"),
        name="gla",
    )(h3, h3, h3, h3, glr3, w2p, gb, ng)


def _dil_kernel(q_ref, k_ref, v_ref, o_ref, qf, kf, vf, qg, kg, vg, qc, kc, vc, ob, db, mx,
                scb, eb):
    s = q_ref.shape[1]
    band = DIL_BAND
    unroll = DIL_UNROLL
    nblk = s // band
    qf[...] = q_ref[0].astype(F32)
    kf[...] = k_ref[0].astype(F32)
    vf[...] = v_ref[0].astype(F32)
    kc[:band, :] = jnp.zeros((band, DIL_HD), BF16)
    vc[:band, :] = jnp.zeros((band, DIL_HD), BF16)

    qi = lax.broadcasted_iota(jnp.int32, (band, 2 * band), 0)
    kj = lax.broadcasted_iota(jnp.int32, (band, 2 * band), 1)
    allowed = (kj >= qi) & (kj <= qi + band)
    bias = jnp.where(allowed, 0.0, -jnp.inf).astype(F32)
    bias0 = jnp.where(allowed & (kj >= band), 0.0, -jnp.inf).astype(F32)
    ones = jnp.ones((2 * band, LANES), BF16)

    for p, (window, dil) in enumerate(DIL_PATTERNS):
        assert window // dil == band
        cls = s // dil
        nb = cls // band
        span = band * dil
        assert nblk % unroll == 0 and (nb % unroll == 0 or unroll % nb == 0)

        if dil == 1:
            qc[...] = q_ref[0]
            kc[band:, :] = k_ref[0]
            vc[band:, :] = v_ref[0]
        else:
            prev = DIL_PATTERNS[p - 1][1]
            step = dil // prev
            assert step * prev == dil and step in (2, 4)
            keep = p + 1 < len(DIL_PATTERNS)
            srcs, dsts = ((qf, kf, vf), (qg, kg, vg)) if p % 2 == 1 else ((qg, kg, vg), (qf, kf, vf))
            for r in range(dil):
                rows = pl.ds((r % prev) * (s // prev) + r // prev, cls, stride=step)
                for src, dst, cm, off in zip(srcs, dsts, (qc, kc, vc), (0, band, band)):
                    x = src[rows, :]
                    if keep:
                        dst[r * cls:(r + 1) * cls, :] = x
                    cm[off + r * cls:off + (r + 1) * cls, :] = x.astype(BF16)

        def out_rows(g, lo=0, cnt=band, dil=dil, nb=nb, span=span):
            start = g // nb + (g % nb) * span + lo * dil
            return pl.ds(start, cnt) if dil == 1 else pl.ds(start, cnt, stride=dil)

        def scores(t, slot):
            for u in range(unroll):
                g = t * unroll + u
                k2 = kc[g * band:(g + 2) * band, :]
                scb[slot, u * band:(u + 1) * band, :] = _dot_nt(qc[g * band:(g + 1) * band, :], k2)

        def softmax(t, slot, p=p, nb=nb, out_rows=out_rows):
            for u in range(unroll):
                g = t * unroll + u
                bb = bias0 if g % nb == 0 else bias
                rows = slice(u * band, (u + 1) * band)
                m = jnp.max(scb[slot, rows, :] + bb, axis=-1, keepdims=True)
                mx[p, out_rows(g), :] = jnp.broadcast_to(m, (band, LANES))
                for half in range(2):
                    cols = slice(half * band, (half + 1) * band)
                    eb[slot, rows, cols] = jnp.exp(scb[slot, rows, cols] + bb[:, cols] - m).astype(BF16)

        def values(t, slot, p=p, out_rows=out_rows):
            for u in range(unroll):
                g = t * unroll + u
                v2 = vc[g * band:(g + 2) * band, :]
                oe = _dot(eb[slot, u * band:(u + 1) * band, :],
                          jnp.concatenate([v2, ones], axis=1))
                ob[p, out_rows(g), :] = oe[:, :DIL_HD]
                db[p, out_rows(g), :] = oe[:, DIL_HD:]

        ngrp = nblk // unroll
        for t in range(ngrp + 2):
            if t < ngrp:
                scores(t, t % 2)
            if 1 <= t <= ngrp:
                softmax(t - 1, (t - 1) % 2)
            if t >= 2:
                values(t - 2, t % 2)

    mb = 512

    def merge(i, carry):
        rs = pl.ds(pl.multiple_of(i * mb, mb), mb)
        m0, m1, m2 = mx[0, rs, :], mx[1, rs, :], mx[2, rs, :]
        m = jnp.maximum(jnp.maximum(m0, m1), m2)
        e0, e1, e2 = jnp.exp(m0 - m), jnp.exp(m1 - m), jnp.exp(m2 - m)
        num = e0 * ob[0, rs, :] + e1 * ob[1, rs, :] + e2 * ob[2, rs, :]
        den = e0 * db[0, rs, :] + e1 * db[1, rs, :] + e2 * db[2, rs, :]
        o_ref[0, rs, :] = (num / den).astype(o_ref.dtype)
        return carry

    lax.fori_loop(0, s // mb, merge, 0)


def _dil(h3):
    bsz, s, _ = h3.shape
    qb = (2 * GLA_HEADS * GLA_DK + 2 * GLA_HEADS * GLA_DV) // DIL_HD
    kb = qb + DIL_HEADS
    vb = kb + DIL_HEADS
    npat = len(DIL_PATTERNS)
    return pl.pallas_call(
        _dil_kernel,
        grid=(bsz, DIL_HEADS),
        in_specs=[pl.BlockSpec((1, s, DIL_HD), lambda b, h: (b, 0, qb + h)),
                  pl.BlockSpec((1, s, DIL_HD), lambda b, h: (b, 0, kb + h)),
                  pl.BlockSpec((1, s, DIL_HD), lambda b, h: (b, 0, vb + h))],
        out_specs=pl.BlockSpec((1, s, DIL_HD), lambda b, h: (b, 0, h)),
        out_shape=jax.ShapeDtypeStruct((bsz, s, DIL_HEADS * DIL_HD), BF16),
        scratch_shapes=[pltpu.VMEM((s, DIL_HD), F32)] * 6 + [
                        pltpu.VMEM((s, DIL_HD), BF16), pltpu.VMEM((s + DIL_BAND, DIL_HD), BF16),
                        pltpu.VMEM((s + DIL_BAND, DIL_HD), BF16),
                        pltpu.VMEM((npat, s, DIL_HD), F32), pltpu.VMEM((npat, s, LANES), F32),
                        pltpu.VMEM((npat, s, LANES), F32),
                        pltpu.VMEM((2, DIL_UNROLL * DIL_BAND, 2 * DIL_BAND), F32),
                        pltpu.VMEM((2, DIL_UNROLL * DIL_BAND, 2 * DIL_BAND), BF16)],
        compiler_params=_cparams("parallel", "parallel"),
        name="dilated_attention",
    )(h3, h3, h3)


def _outproj_kernel(og_ref, od_ref, wa_ref, wb_ref, x_ref, g_ref, b_ref, o_ref, *, alpha):
    acc = _dot(og_ref[...], wa_ref[...]) + _dot(od_ref[...], wb_ref[...])
    o_ref[...] = _layer_norm(alpha * x_ref[...] + acc, g_ref[...], b_ref[...])


def _resident(shape):
    return pl.BlockSpec(shape, lambda *_: (0,) * len(shape), pipeline_mode=pl.Buffered(1))


def _outproj(og, od, wa, wb, x2d, g, b, bm, alpha):
    t, d = x2d.shape
    ka, kb = og.shape[1], od.shape[1]
    return pl.pallas_call(
        functools.partial(_outproj_kernel, alpha=alpha),
        grid=(t // bm,),
        in_specs=[pl.BlockSpec((bm, ka), lambda i: (i, 0)),
                  pl.BlockSpec((bm, kb), lambda i: (i, 0)),
                  _resident((ka, d)), _resident((kb, d)),
                  pl.BlockSpec((bm, d), lambda i: (i, 0)),
                  _resident((1, d)), _resident((1, d))],
        out_specs=pl.BlockSpec((bm, d), lambda i: (i, 0)),
        out_shape=jax.ShapeDtypeStruct((t, d), F32),
        compiler_params=_cparams("parallel"),
        name="out_projection_ln1",
    )(og, od, wa, wb, x2d, g, b)


def _memkv_kernel(m_ref, w_ref, o_ref):
    o_ref[...] = _dot(m_ref[...].astype(BF16), w_ref[...]).astype(o_ref.dtype)


def _memkv(mem2d, wkv, bn):
    t, d = mem2d.shape
    n = wkv.shape[1]
    return pl.pallas_call(
        _memkv_kernel,
        grid=(n // bn,),
        in_specs=[pl.BlockSpec((t, d), lambda j: (0, 0)),
                  pl.BlockSpec((d, bn), lambda j: (0, j))],
        out_specs=pl.BlockSpec((t, bn), lambda j: (0, j)),
        out_shape=jax.ShapeDtypeStruct((t, n), BF16),
        compiler_params=_cparams("parallel"),
        name="memory_kv_projection",
    )(mem2d, wkv)


def _cross_kernel(x_ref, wq_ref, kv_ref, wo_ref, g_ref, b_ref, o_ref, oc_ref, *, alpha):
    d = x_ref.shape[1]
    hd = d // CA_HEADS
    x = x_ref[...]
    q = _dot(x.astype(BF16), wq_ref[...]).astype(BF16)
    for h in range(CA_HEADS):
        cols = slice(h * hd, (h + 1) * hd)
        mk = kv_ref[0, :, cols]
        mv = kv_ref[0, :, d + h * hd:d + (h + 1) * hd]
        sc = _dot_nt(q[:, cols], mk) * (hd ** -0.5)
        m = jnp.max(sc, axis=-1, keepdims=True)
        e = jnp.exp(sc - m)
        p = e / jnp.sum(e, axis=-1, keepdims=True)
        oc_ref[:, cols] = _dot(p.astype(BF16), mv).astype(BF16)
    y = alpha * x + _dot(oc_ref[...], wo_ref[...])
    o_ref[...] = _layer_norm(y, g_ref[...], b_ref[...])


def _cross(x1, wq, kv3, wo, g, b, bm, seq, alpha):
    t, d = x1.shape
    m = kv3.shape[1]
    per = seq // bm
    return pl.pallas_call(
        functools.partial(_cross_kernel, alpha=alpha),
        grid=(t // bm,),
        in_specs=[pl.BlockSpec((bm, d), lambda i: (i, 0)),
                  _resident((d, d)),
                  pl.BlockSpec((1, m, 2 * d), lambda i: (i // per, 0, 0)),
                  _resident((d, d)),
                  _resident((1, d)), _resident((1, d))],
        out_specs=pl.BlockSpec((bm, d), lambda i: (i, 0)),
        out_shape=jax.ShapeDtypeStruct((t, d), F32),
        scratch_shapes=[pltpu.VMEM((bm, d), BF16)],
        compiler_params=_cparams("parallel"),
        name="cross_attention_ln2",
    )(x1, wq, kv3, wo, g, b)


def _causal_conv(u, cw, cb, bm):
    h = FFN_HALO
    y = cb + cw[0:1, :] * u[h - 2:h - 2 + bm]
    y = y + cw[1:2, :] * u[h - 1:h - 1 + bm]
    return y + cw[2:3, :] * u[h:h + bm]


def _ffn_kernel(x_ref, xp_ref, cp_ref, g_ref, b_ref, wg_hbm, wu_hbm, w2_hbm, o_ref,
                xb_ref, wg_buf, wu_buf, w2_buf, sem, *, alpha, per):
    i = pl.program_id(0)
    nf = wg_hbm.shape[0]
    total = pl.num_programs(0) * nf
    bm = x_ref.shape[0]

    def copies(f, slot):
        return (pltpu.make_async_copy(wg_hbm.at[f], wg_buf.at[slot], sem.at[0, slot]),
                pltpu.make_async_copy(wu_hbm.at[f], wu_buf.at[slot], sem.at[1, slot]),
                pltpu.make_async_copy(w2_hbm.at[f], w2_buf.at[slot], sem.at[2, slot]))

    @pl.when(i == 0)
    def _():
        for cp in copies(0, 0):
            cp.start()

    halo = jnp.where(i % per == 0, 0.0, xp_ref[...])
    xb_ref[:FFN_HALO, :] = halo.astype(BF16)
    xb_ref[FFN_HALO:, :] = x_ref[...].astype(BF16)
    o_ref[...] = jnp.zeros_like(o_ref)

    def chunk(f, carry):
        c = i * nf + f
        slot = lax.rem(c, 2)
        for cp in copies(f, slot):
            cp.wait()

        @pl.when(c + 1 < total)
        def _():
            for cp in copies(lax.rem(f + 1, nf), 1 - slot):
                cp.start()

        xb = xb_ref[...]
        cv = cp_ref[f]
        acts = []
        for c0 in range(0, wg_buf.shape[2], FFN_SUB):
            cols = slice(c0, c0 + FFN_SUB)
            gate = _causal_conv(_dot(xb, wg_buf[slot, :, cols]), cv[0:3, cols], cv[3:4, cols], bm)
            up = _causal_conv(_dot(xb, wu_buf[slot, :, cols]), cv[4:7, cols], cv[7:8, cols], bm)
            acts.append((gate * _sigmoid(gate) * up).astype(BF16))
        act = jnp.concatenate(acts, axis=1)
        half = o_ref.shape[1] // 2
        for n0 in (0, half):
            o_ref[:, n0:n0 + half] += _dot(act, w2_buf[slot, :, n0:n0 + half])
        return carry

    lax.fori_loop(0, nf, chunk, 0)
    o_ref[...] = _layer_norm(alpha * x_ref[...] + o_ref[...], g_ref[...], b_ref[...])


def _ffn(x2, wg, wu, cp, w2, g, b, bm, seq, alpha):
    t, d = x2.shape
    nf, _, bf = wg.shape
    per = seq // bm
    hb = bm // FFN_HALO
    hbm = pl.BlockSpec(memory_space=pl.ANY)
    return pl.pallas_call(
        functools.partial(_ffn_kernel, alpha=alpha, per=per),
        grid=(t // bm,),
        in_specs=[pl.BlockSpec((bm, d), lambda i: (i, 0), pipeline_mode=pl.Buffered(1)),
                  pl.BlockSpec((FFN_HALO, d), lambda i: (jnp.maximum(i * hb - 1, 0), 0)),
                  _resident(cp.shape), _resident((1, d)), _resident((1, d)),
                  hbm, hbm, hbm],
        out_specs=pl.BlockSpec((bm, d), lambda i: (i, 0)),
        out_shape=jax.ShapeDtypeStruct((t, d), F32),
        scratch_shapes=[pltpu.VMEM((bm + FFN_HALO, d), BF16),
                        pltpu.VMEM((2, d, bf), BF16), pltpu.VMEM((2, d, bf), BF16),
                        pltpu.VMEM((2, bf, d), BF16), pltpu.SemaphoreType.DMA((3, 2))],
        compiler_params=_cparams("arbitrary"),
        name="conv_ffn_ln3",
    )(x2, x2, cp, g, b, wg, wu, w2)


def _pad_cols(a, n):
    return jnp.pad(a, ((0, 0), (0, n - a.shape[1])))


def kernel(x, mem, positions, w_in, gla_gate_w2, gla_gate_b, gla_norm_g, w_out, ln1_g, ln1_b,
           ca_wq, ca_wkv, ca_wo, ln2_g, ln2_b, ffn_w_in, ffn_conv_w, ffn_conv_b, ffn_w_out,
           ln3_g, ln3_b):
    bsz, seq, d = x.shape
    depth = w_in.shape[0]
    t = bsz * seq
    alpha = (2.0 * depth) ** 0.25
    d_ff = ffn_w_out.shape[1]
    dff_pad = -(-d_ff // FFN_CHUNK) * FFN_CHUNK

    nqk = GLA_HEADS * GLA_DK
    nv = GLA_HEADS * GLA_DV
    c_glr = 2 * nqk + 2 * nv
    c_dil = c_glr + GLA_GATE_RANK

    half = ROPE_HALF
    inv_freq = ROPE_THETA ** (-jnp.arange(0, ROPE_DIMS, 2, dtype=F32) / ROPE_DIMS)
    inv_row = jnp.concatenate([inv_freq, inv_freq, jnp.zeros((LANES - 2 * half,), F32)])[None, :]
    pos_col = positions.astype(F32).reshape(t, 1)
    cosf, sina, sinb = _rope_tables(pos_col, inv_row, min(t, 2048))

    x2d = x.reshape(t, d)
    for l in range(depth):
        wl = w_in[l]
        wcat = jnp.concatenate([wl[:, :c_glr], wl[:, c_dil:]], axis=1).astype(BF16)
        wglr = _pad_cols(wl[:, c_glr:c_dil], LANES).astype(BF16)
        h, glr = _inproj(x2d, wcat, wglr, cosf, sina, sinb, min(t, 512))
        h3 = h.reshape(bsz, seq, h.shape[1])

        w2p = jnp.pad(gla_gate_w2[l], ((0, LANES - GLA_GATE_RANK), (0, 0))).astype(BF16)
        og = _gla(h3, glr.reshape(bsz, seq, LANES), w2p, gla_gate_b[l][None, :],
                  gla_norm_g[l][None, :], min(seq, 512))
        od = _dil(h3)

        wo = w_out[l].astype(BF16)
        x1 = _outproj(og.reshape(t, nv), od.reshape(t, DIL_HEADS * DIL_HD), wo[:nv], wo[nv:],
                      x2d, ln1_g[l][None, :], ln1_b[l][None, :], min(t, 512), alpha)

        kv = _memkv(mem.reshape(-1, d), ca_wkv[l].astype(BF16), 1024)
        x2 = _cross(x1, ca_wq[l].astype(BF16), kv.reshape(bsz, -1, 2 * d), ca_wo[l].astype(BF16),
                    ln2_g[l][None, :], ln2_b[l][None, :], min(seq, 512), seq, alpha)

        w1 = ffn_w_in[l]
        cw = ffn_conv_w[l]
        cb = ffn_conv_b[l][None, :]
        nf = dff_pad // FFN_CHUNK

        def chunked_cols(a):
            return _pad_cols(a, dff_pad).reshape(a.shape[0], nf, FFN_CHUNK).transpose(1, 0, 2)

        conv = jnp.concatenate([cw[:, :d_ff], cb[:, :d_ff], cw[:, d_ff:], cb[:, d_ff:]], axis=0)
        w2 = jnp.pad(ffn_w_out[l], ((0, dff_pad - d_ff), (0, 0))).astype(BF16)
        x2d = _ffn(x2, chunked_cols(w1[:, :d_ff]).astype(BF16),
                   chunked_cols(w1[:, d_ff:]).astype(BF16), chunked_cols(conv),
                   w2.reshape(nf, FFN_CHUNK, d),
                   ln3_g[l][None, :], ln3_b[l][None, :], min(seq, 1024), seq, alpha)
    return x2d.reshape(bsz, seq, d)
```

```python
import functools

import jax
import jax.numpy as jnp
from jax import lax
from jax.experimental import pallas as pl
from jax.experimental.pallas import tpu as pltpu

F32 = jnp.float32
BF16 = jnp.bfloat16

LANES = 128
LN_EPS = 1e-5
GLA_HEADS = 4
GLA_DK = 128
GLA_DV = 256
GLA_GATE_RANK = 16
GLA_TAU = 16.0
GLA_CHUNK = 64
DIL_HD = 128
DIL_HEADS = 8
DIL_PATTERNS = ((128, 1), (512, 4), (2048, 16))
DIL_BAND = 128
DIL_UNROLL = 4
ROPE_THETA = 500000.0
ROPE_DIMS = 32
ROPE_HALF = ROPE_DIMS // 2
CA_HEADS = 4
CONV_W = 3
INPROJ_BLOCK = 1024
INPROJ_SUB = 256
FFN_CHUNK = 512
FFN_SUB = 256
FFN_ROWS = 512
FFN_TAIL = 8
VMEM_LIMIT = 56 * 1024 * 1024


def _cparams(*sem):
    return pltpu.CompilerParams(dimension_semantics=sem, vmem_limit_bytes=VMEM_LIMIT)


def _dot(a, b):
    return jnp.dot(a, b, preferred_element_type=F32)


def _dot_nt(a, b):
    return lax.dot_general(a, b, (((1,), (1,)), ((), ())), preferred_element_type=F32)


def _dot_tn(a, b):
    return lax.dot_general(a, b, (((0,), (0,)), ((), ())), preferred_element_type=F32)


def _layer_norm(y, g, b):
    mu = jnp.mean(y, axis=-1, keepdims=True)
    d = y - mu
    var = jnp.mean(d * d, axis=-1, keepdims=True)
    return d * lax.rsqrt(var + LN_EPS) * g + b


def _sigmoid(x):
    return 1.0 / (1.0 + jnp.exp(-x))


def _rope_kernel(pos_ref, inv_ref, cos_ref, sa_ref, sb_ref):
    ang = pos_ref[...] * inv_ref[...]
    lane = lax.broadcasted_iota(jnp.int32, ang.shape, 1)
    c = jnp.cos(ang)
    s = jnp.sin(ang)
    cos_ref[...] = jnp.where(lane < ROPE_DIMS, c, 1.0)
    sa_ref[...] = jnp.where(lane < ROPE_HALF, -s, 0.0)
    sb_ref[...] = jnp.where(lane < ROPE_HALF, 0.0, jnp.where(lane < ROPE_DIMS, s, 0.0))


def _rope_tables(pos_col, inv_row, bs):
    t = pos_col.shape[0]
    out = jax.ShapeDtypeStruct((t, LANES), F32)
    spec = pl.BlockSpec((bs, LANES), lambda i: (i, 0))
    return pl.pallas_call(
        _rope_kernel,
        grid=(t // bs,),
        in_specs=[pl.BlockSpec((bs, 1), lambda i: (i, 0)),
                  pl.BlockSpec((1, LANES), lambda i: (0, 0))],
        out_specs=[spec, spec, spec],
        out_shape=[out, out, out],
        compiler_params=_cparams("parallel"),
        name="rope_tables",
    )(pos_col, inv_row)


def _rope(t, cos, sa, sb):
    return t * cos + pltpu.roll(t, LANES - ROPE_HALF, 1) * sa + pltpu.roll(t, ROPE_HALF, 1) * sb


def _inproj_kernel(x_ref, w_ref, wg_ref, cos_ref, sa_ref, sb_ref, h_ref, glr_ref, xb_ref):
    bn = INPROJ_BLOCK
    xb_ref[...] = x_ref[...].astype(BF16)
    glr_ref[...] = _dot(xb_ref[...], wg_ref[...])

    def rope(scale):
        def epilogue(acc, c0):
            cos, sa, sb = cos_ref[...], sa_ref[...], sb_ref[...]
            heads = [acc[:, j:j + LANES] for j in range(0, acc.shape[1], LANES)]
            if scale is not None:
                heads = [t * scale for t in heads]
            return jnp.concatenate([_rope(t, cos, sa, sb) for t in heads], axis=1)
        return epilogue

    def plain(acc, c0):
        return acc

    epilogues = (lambda acc, c0: acc * (GLA_DK ** -0.5) if c0 < bn // 2 else acc,
                 plain, plain, rope(DIL_HD ** -0.5), rope(None), plain)
    assert len(epilogues) * bn == w_ref.shape[1]

    for n, epilogue in enumerate(epilogues):
        for c0 in range(0, bn, INPROJ_SUB):
            cols = slice(n * bn + c0, n * bn + c0 + INPROJ_SUB)
            h_ref[:, cols] = epilogue(_dot(xb_ref[...], w_ref[:, cols]), c0).astype(h_ref.dtype)


def _inproj(x2d, wcat, wglr, cosf, sina, sinb, bm):
    t, d = x2d.shape
    ncols = wcat.shape[1]
    tab = pl.BlockSpec((bm, LANES), lambda i: (i, 0))
    return pl.pallas_call(
        _inproj_kernel,
        grid=(t // bm,),
        in_specs=[pl.BlockSpec((bm, d), lambda i: (i, 0)),
                  _resident((d, ncols)), _resident((d, LANES)), tab, tab, tab],
        out_specs=[pl.BlockSpec((bm, ncols), lambda i: (i, 0)),
                   pl.BlockSpec((bm, LANES), lambda i: (i, 0))],
        out_shape=[jax.ShapeDtypeStruct((t, ncols), BF16),
                   jax.ShapeDtypeStruct((t, LANES), F32)],
        scratch_shapes=[pltpu.VMEM((bm, d), BF16)],
        compiler_params=_cparams("parallel"),
        name="in_projection",
    )(x2d, wcat, wglr, cosf, sina, sinb)


def _split3(v):
    hi = v.astype(BF16)
    r1 = v - hi.astype(F32)
    mid = r1.astype(BF16)
    lo = (r1 - mid.astype(F32)).astype(BF16)
    return hi, mid, lo


def _gla_kernel(q_ref, k_ref, v_ref, r_ref, glr_ref, w2_ref, gb_ref, ng_ref, o_ref, st_ref):
    c = GLA_CHUNK
    sb = q_ref.shape[1]
    grp = 4 * c

    @pl.when(pl.program_id(2) == 0)
    def _():
        st_ref[...] = jnp.zeros_like(st_ref)

    z = _dot(glr_ref[0].astype(BF16), w2_ref[...]) + gb_ref[...]
    lg = (jnp.minimum(z, 0.0) - jnp.log1p(jnp.exp(-jnp.abs(z)))) / GLA_TAU

    row = lax.broadcasted_iota(jnp.int32, (2 * grp, grp), 0)
    col = lax.broadcasted_iota(jnp.int32, (2 * grp, grp), 1)
    rr = jnp.where(row < grp, row, row - grp)
    shift = c.bit_length() - 1
    same = (rr >> shift) == (col >> shift)
    lower = jnp.where(same & (col <= rr), 1.0, 0.0)
    upper = jnp.where(same & (col > rr), 1.0, 0.0)
    lu = jnp.where(row < grp, lower, upper).astype(BF16)
    b_parts, e_parts = [], []
    for g0 in range(0, sb, grp):
        pieces = jnp.concatenate(_split3(lg[g0:g0 + grp]), axis=1)
        res = _dot(lu, pieces)
        tot = res[:, :LANES] + res[:, LANES:2 * LANES] + res[:, 2 * LANES:]
        b_parts.append(tot[:grp])
        e_parts.append(tot[grp:])
    b = jnp.concatenate(b_parts, axis=0)
    brest = jnp.concatenate(e_parts, axis=0)

    qf = q_ref[0].astype(F32)
    kf = k_ref[0].astype(F32)
    q_in = (qf * jnp.exp(b)).astype(BF16)
    k_in = (kf * jnp.exp(-b)).astype(BF16)
    k_end = (kf * jnp.exp(brest)).astype(BF16)
    v = v_ref[0]

    ci = lax.broadcasted_iota(jnp.int32, (c, c), 0)
    cj = lax.broadcasted_iota(jnp.int32, (c, c), 1)
    causal = cj <= ci

    st = st_ref[...]
    outs = []
    for i in range(sb // c):
        rows = slice(i * c, (i + 1) * c)
        a = jnp.where(causal, _dot_nt(q_in[rows], k_in[rows]), 0.0).astype(BF16)
        o = _dot(a, v[rows]) + _dot_nt(q_in[rows], st.astype(BF16))
        outs.append(o)
        decay = jnp.exp(b[i * c + c - 1:i * c + c, :])
        st = st * decay + _dot_tn(v[rows], k_end[rows])
    st_ref[...] = st

    o = jnp.concatenate(outs, axis=0)
    mu = jnp.mean(o, axis=-1, keepdims=True)
    d = o - mu
    var = jnp.mean(d * d, axis=-1, keepdims=True)
    rg = r_ref[0].astype(F32)
    y = d * lax.rsqrt(var + LN_EPS) * ng_ref[...] * (rg * _sigmoid(rg))
    o_ref[0] = y.astype(o_ref.dtype)


def _gla(h3, glr3, w2p, gb, ng, sb):
    bsz, s, _ = h3.shape
    kb = GLA_HEADS * GLA_DK // LANES
    vb = 2 * GLA_HEADS * GLA_DK // GLA_DV
    rb = vb + GLA_HEADS
    return pl.pallas_call(
        _gla_kernel,
        grid=(bsz, GLA_HEADS, s // sb),
        in_specs=[pl.BlockSpec((1, sb, GLA_DK), lambda b, h, j: (b, j, h)),
                  pl.BlockSpec((1, sb, GLA_DK), lambda b, h, j: (b, j, kb + h)),
                  pl.BlockSpec((1, sb, GLA_DV), lambda b, h, j: (b, j, vb + h)),
                  pl.BlockSpec((1, sb, GLA_DV), lambda b, h, j: (b, j, rb + h)),
                  pl.BlockSpec((1, sb, LANES), lambda b, h, j: (b, j, 0)),
                  pl.BlockSpec((LANES, GLA_DK), lambda b, h, j: (0, h)),
                  pl.BlockSpec((1, GLA_DK), lambda b, h, j: (0, h)),
                  pl.BlockSpec((1, GLA_DV), lambda b, h, j: (0, h))],
        out_specs=pl.BlockSpec((1, sb, GLA_DV), lambda b, h, j: (b, j, h)),
        out_shape=jax.ShapeDtypeStruct((bsz, s, GLA_HEADS * GLA_DV), BF16),
        scratch_shapes=[pltpu.VMEM((GLA_DV, GLA_DK), F32)],
        compiler_params=_cparams("parallel", "parallel", "arbitrary"),
        name="gla",
    )(h3, h3, h3, h3, glr3, w2p, gb, ng)


def _dil_kernel(q_ref, k_ref, v_ref, o_ref, qf, kf, vf, qg, kg, vg, qc, kc, vc, ob, db, mx,
                scb, eb):
    s = q_ref.shape[1]
    band = DIL_BAND
    unroll = DIL_UNROLL
    nblk = s // band
    qf[...] = q_ref[0].astype(F32)
    kf[...] = k_ref[0].astype(F32)
    vf[...] = v_ref[0].astype(F32)
    kc[:band, :] = jnp.zeros((band, DIL_HD), BF16)
    vc[:band, :] = jnp.zeros((band, DIL_HD), BF16)

    qi = lax.broadcasted_iota(jnp.int32, (band, 2 * band), 0)
    kj = lax.broadcasted_iota(jnp.int32, (band, 2 * band), 1)
    allowed = (kj >= qi) & (kj <= qi + band)
    bias = jnp.where(allowed, 0.0, -jnp.inf).astype(F32)
    bias0 = jnp.where(allowed & (kj >= band), 0.0, -jnp.inf).astype(F32)
    ones = jnp.ones((2 * band, LANES), BF16)

    for p, (window, dil) in enumerate(DIL_PATTERNS):
        assert window // dil == band
        cls = s // dil
        nb = cls // band
        span = band * dil
        assert nblk % unroll == 0 and (nb % unroll == 0 or unroll % nb == 0)

        if dil == 1:
            qc[...] = q_ref[0]
            kc[band:, :] = k_ref[0]
            vc[band:, :] = v_ref[0]
        else:
            prev = DIL_PATTERNS[p - 1][1]
            step = dil // prev
            assert step * prev == dil and step in (2, 4)
            keep = p + 1 < len(DIL_PATTERNS)
            srcs, dsts = ((qf, kf, vf), (qg, kg, vg)) if p % 2 == 1 else ((qg, kg, vg), (qf, kf, vf))
            for r in range(dil):
                rows = pl.ds((r % prev) * (s // prev) + r // prev, cls, stride=step)
                for src, dst, cm, off in zip(srcs, dsts, (qc, kc, vc), (0, band, band)):
                    x = src[rows, :]
                    if keep:
                        dst[r * cls:(r + 1) * cls, :] = x
                    cm[off + r * cls:off + (r + 1) * cls, :] = x.astype(BF16)

        def out_rows(g, lo=0, cnt=band, dil=dil, nb=nb, span=span):
            start = g // nb + (g % nb) * span + lo * dil
            return pl.ds(start, cnt) if dil == 1 else pl.ds(start, cnt, stride=dil)

        def scores(t, slot):
            for u in range(unroll):
                g = t * unroll + u
                k2 = kc[g * band:(g + 2) * band, :]
                scb[slot, u * band:(u + 1) * band, :] = _dot_nt(qc[g * band:(g + 1) * band, :], k2)

        def softmax(t, slot, p=p, nb=nb, out_rows=out_rows):
            for u in range(unroll):
                g = t * unroll + u
                bb = bias0 if g % nb == 0 else bias
                rows = slice(u * band, (u + 1) * band)
                m = jnp.max(scb[slot, rows, :] + bb, axis=-1, keepdims=True)
                mx[p, out_rows(g), :] = jnp.broadcast_to(m, (band, LANES))
                for half in range(2):
                    cols = slice(half * band, (half + 1) * band)
                    eb[slot, rows, cols] = jnp.exp(scb[slot, rows, cols] + bb[:, cols] - m).astype(BF16)

        def values(t, slot, p=p, out_rows=out_rows):
            for u in range(unroll):
                g = t * unroll + u
                v2 = vc[g * band:(g + 2) * band, :]
                oe = _dot(eb[slot, u * band:(u + 1) * band, :],
                          jnp.concatenate([v2, ones], axis=1))
                ob[p, out_rows(g), :] = oe[:, :DIL_HD]
                db[p, out_rows(g), :] = oe[:, DIL_HD:]

        ngrp = nblk // unroll
        for t in range(ngrp + 2):
            if t < ngrp:
                scores(t, t % 2)
            if 1 <= t <= ngrp:
                softmax(t - 1, (t - 1) % 2)
            if t >= 2:
                values(t - 2, t % 2)

    mb = 512

    def merge(i, carry):
        rs = pl.ds(pl.multiple_of(i * mb, mb), mb)
        m0, m1, m2 = mx[0, rs, :], mx[1, rs, :], mx[2, rs, :]
        m = jnp.maximum(jnp.maximum(m0, m1), m2)
        e0, e1, e2 = jnp.exp(m0 - m), jnp.exp(m1 - m), jnp.exp(m2 - m)
        num = e0 * ob[0, rs, :] + e1 * ob[1, rs, :] + e2 * ob[2, rs, :]
        den = e0 * db[0, rs, :] + e1 * db[1, rs, :] + e2 * db[2, rs, :]
        o_ref[0, rs, :] = (num / den).astype(o_ref.dtype)
        return carry

    lax.fori_loop(0, s // mb, merge, 0)


def _dil(h3):
    bsz, s, _ = h3.shape
    qb = (2 * GLA_HEADS * GLA_DK + 2 * GLA_HEADS * GLA_DV) // DIL_HD
    kb = qb + DIL_HEADS
    vb = kb + DIL_HEADS
    npat = len(DIL_PATTERNS)
    return pl.pallas_call(
        _dil_kernel,
        grid=(bsz, DIL_HEADS),
        in_specs=[pl.BlockSpec((1, s, DIL_HD), lambda b, h: (b, 0, qb + h)),
                  pl.BlockSpec((1, s, DIL_HD), lambda b, h: (b, 0, kb + h)),
                  pl.BlockSpec((1, s, DIL_HD), lambda b, h: (b, 0, vb + h))],
        out_specs=pl.BlockSpec((1, s, DIL_HD), lambda b, h: (b, 0, h)),
        out_shape=jax.ShapeDtypeStruct((bsz, s, DIL_HEADS * DIL_HD), BF16),
        scratch_shapes=[pltpu.VMEM((s, DIL_HD), F32)] * 6 + [
                        pltpu.VMEM((s, DIL_HD), BF16), pltpu.VMEM((s + DIL_BAND, DIL_HD), BF16),
                        pltpu.VMEM((s + DIL_BAND, DIL_HD), BF16),
                        pltpu.VMEM((npat, s, DIL_HD), F32), pltpu.VMEM((npat, s, LANES), F32),
                        pltpu.VMEM((npat, s, LANES), F32),
                        pltpu.VMEM((2, DIL_UNROLL * DIL_BAND, 2 * DIL_BAND), F32),
                        pltpu.VMEM((2, DIL_UNROLL * DIL_BAND, 2 * DIL_BAND), BF16)],
        compiler_params=_cparams("parallel", "parallel"),
        name="dilated_attention",
    )(h3, h3, h3)


def _outproj_kernel(og_ref, od_ref, wa_ref, wb_ref, x_ref, g_ref, b_ref, o_ref, *, alpha):
    acc = _dot(og_ref[...], wa_ref[...]) + _dot(od_ref[...], wb_ref[...])
    o_ref[...] = _layer_norm(alpha * x_ref[...] + acc, g_ref[...], b_ref[...])


def _resident(shape):
    return pl.BlockSpec(shape, lambda *_: (0,) * len(shape), pipeline_mode=pl.Buffered(1))


def _outproj(og, od, wa, wb, x2d, g, b, bm, alpha):
    t, d = x2d.shape
    ka, kb = og.shape[1], od.shape[1]
    return pl.pallas_call(
        functools.partial(_outproj_kernel, alpha=alpha),
        grid=(t // bm,),
        in_specs=[pl.BlockSpec((bm, ka), lambda i: (i, 0)),
                  pl.BlockSpec((bm, kb), lambda i: (i, 0)),
                  _resident((ka, d)), _resident((kb, d)),
                  pl.BlockSpec((bm, d), lambda i: (i, 0)),
                  _resident((1, d)), _resident((1, d))],
        out_specs=pl.BlockSpec((bm, d), lambda i: (i, 0)),
        out_shape=jax.ShapeDtypeStruct((t, d), F32),
        compiler_params=_cparams("parallel"),
        name="out_projection_ln1",
    )(og, od, wa, wb, x2d, g, b)


def _memkv_kernel(m_ref, w_ref, o_ref):
    o_ref[...] = _dot(m_ref[...].astype(BF16), w_ref[...]).astype(o_ref.dtype)


def _memkv(mem2d, wkv, bn):
    t, d = mem2d.shape
    n = wkv.shape[1]
    return pl.pallas_call(
        _memkv_kernel,
        grid=(n // bn,),
        in_specs=[pl.BlockSpec((t, d), lambda j: (0, 0)),
                  pl.BlockSpec((d, bn), lambda j: (0, j))],
        out_specs=pl.BlockSpec((t, bn), lambda j: (0, j)),
        out_shape=jax.ShapeDtypeStruct((t, n), BF16),
        compiler_params=_cparams("parallel"),
        name="memory_kv_projection",
    )(mem2d, wkv)


def _cross_kernel(x_ref, wq_ref, kv_ref, wo_ref, g_ref, b_ref, o_ref, oc_ref, *, alpha):
    d = x_ref.shape[1]
    hd = d // CA_HEADS
    x = x_ref[...]
    q = _dot(x.astype(BF16), wq_ref[...]).astype(BF16)
    for h in range(CA_HEADS):
        cols = slice(h * hd, (h + 1) * hd)
        mk = kv_ref[0, :, cols]
        mv = kv_ref[0, :, d + h * hd:d + (h + 1) * hd]
        sc = _dot_nt(q[:, cols], mk) * (hd ** -0.5)
        m = jnp.max(sc, axis=-1, keepdims=True)
        e = jnp.exp(sc - m)
        p = e / jnp.sum(e, axis=-1, keepdims=True)
        oc_ref[:, cols] = _dot(p.astype(BF16), mv).astype(BF16)
    y = alpha * x + _dot(oc_ref[...], wo_ref[...])
    o_ref[...] = _layer_norm(y, g_ref[...], b_ref[...])


def _cross(x1, wq, kv3, wo, g, b, bm, seq, alpha):
    t, d = x1.shape
    m = kv3.shape[1]
    per = seq // bm
    return pl.pallas_call(
        functools.partial(_cross_kernel, alpha=alpha),
        grid=(t // bm,),
        in_specs=[pl.BlockSpec((bm, d), lambda i: (i, 0)),
                  _resident((d, d)),
                  pl.BlockSpec((1, m, 2 * d), lambda i: (i // per, 0, 0)),
                  _resident((d, d)),
                  _resident((1, d)), _resident((1, d))],
        out_specs=pl.BlockSpec((bm, d), lambda i: (i, 0)),
        out_shape=jax.ShapeDtypeStruct((t, d), F32),
        scratch_shapes=[pltpu.VMEM((bm, d), BF16)],
        compiler_params=_cparams("parallel"),
        name="cross_attention_ln2",
    )(x1, wq, kv3, wo, g, b)


def _causal_conv(u_ref, cw, cb, bm):
    h = FFN_TAIL
    y = cb + cw[0:1, :] * u_ref[h - 2:h - 2 + bm, :]
    y = y + cw[1:2, :] * u_ref[h - 1:h - 1 + bm, :]
    return y + cw[2:3, :] * u_ref[h:h + bm, :]


def _ffn_kernel(x_ref, cp_ref, g_ref, b_ref, wg_hbm, wu_hbm, w2_hbm, o_ref,
                xb_ref, act0_ref, act1_ref, us_ref, tail_ref, wg_buf, wu_buf, w2_buf,
                sem, *, alpha, per):
    i = pl.program_id(0)
    nf = wg_hbm.shape[0]
    assert nf % 2 == 1
    total = pl.num_programs(0) * nf
    bm = x_ref.shape[0]
    bf = wg_buf.shape[2]
    c_first = i * nf
    acts = (act0_ref, act1_ref)
    assert bf % FFN_SUB == 0 and FFN_SUB % LANES == 0 and bm % FFN_ROWS == 0

    def up_copies(f, slot):
        return (pltpu.make_async_copy(wg_hbm.at[f], wg_buf.at[slot], sem.at[0, slot]),
                pltpu.make_async_copy(wu_hbm.at[f], wu_buf.at[slot], sem.at[1, slot]))

    def down_copy(f, slot):
        return pltpu.make_async_copy(w2_hbm.at[f], w2_buf.at[slot], sem.at[2, slot])

    def region_copies(f, first=False):
        slot = lax.rem(c_first + f, 2)
        for cp in up_copies(f, slot):
            cp.wait()
        if not first:
            down_copy(f - 1, 1 - slot).wait()

        @pl.when(c_first + f + 1 < total)
        def _():
            for cp in up_copies(lax.rem(f + 1, nf), 1 - slot):
                cp.start()

        down_copy(f, slot).start()
        return slot

    def up_project(f, slot, act_slot):
        cv = cp_ref[f]
        subs = range(bf // FFN_SUB)
        nslab = bf // LANES
        for s in range(2 * nslab):
            us_ref[s, :FFN_TAIL, :] = tail_ref[f, :, s * LANES:(s + 1) * LANES]
        for j in subs:
            cols = slice(j * FFN_SUB, (j + 1) * FFN_SUB)
            for r0 in range(0, bm, FFN_ROWS):
                xr = xb_ref[r0:r0 + FFN_ROWS, :]
                rows = slice(FFN_TAIL + r0, FFN_TAIL + r0 + FFN_ROWS)
                for base, w_buf in ((0, wg_buf), (nslab, wu_buf)):
                    res = _dot(xr, w_buf[slot, :, cols])
                    for k in range(FFN_SUB // LANES):
                        us_ref[base + j * (FFN_SUB // LANES) + k, rows, :] = res[:, k * LANES:(k + 1) * LANES]
        for s in range(nslab):
            cols = slice(s * LANES, (s + 1) * LANES)
            gate = _causal_conv(us_ref.at[s], cv[0:3, cols], cv[3:4, cols], bm)
            up = _causal_conv(us_ref.at[nslab + s], cv[4:7, cols], cv[7:8, cols], bm)
            acts[act_slot][:, cols] = (gate * _sigmoid(gate) * up).astype(BF16)
        for s in range(2 * nslab):
            tail_ref[f, :, s * LANES:(s + 1) * LANES] = us_ref[s, bm:bm + FFN_TAIL, :]

    def down_project(slot, act_slot):
        half = o_ref.shape[1] // 2
        for n0 in (0, half):
            o_ref[:, n0:n0 + half] += _dot(acts[act_slot][...], w2_buf[slot, :, n0:n0 + half])

    @pl.when(i == 0)
    def _():
        for cp in up_copies(0, 0):
            cp.start()

    @pl.when(i % per == 0)
    def _():
        tail_ref[...] = jnp.zeros_like(tail_ref)

    xb_ref[...] = x_ref[...].astype(BF16)
    o_ref[...] = jnp.zeros_like(o_ref)

    slot = region_copies(0, first=True)
    up_project(0, slot, 0)

    def pair(j, carry):
        for f, act_slot in ((2 * j + 1, 1), (2 * j + 2, 0)):
            slot = region_copies(f)
            up_project(f, slot, act_slot)
            down_project(1 - slot, 1 - act_slot)
        return carry

    lax.fori_loop(0, (nf - 1) // 2, pair, 0)

    last = lax.rem(c_first + nf - 1, 2)
    down_copy(nf - 1, last).wait()
    down_project(last, (nf - 1) % 2)
    o_ref[...] = _layer_norm(alpha * x_ref[...] + o_ref[...], g_ref[...], b_ref[...])


def _ffn(x2, wg, wu, cp, w2, g, b, bm, seq, alpha):
    t, d = x2.shape
    nf, _, bf = wg.shape
    hbm = pl.BlockSpec(memory_space=pl.ANY)
    return pl.pallas_call(
        functools.partial(_ffn_kernel, alpha=alpha, per=seq // bm),
        grid=(t // bm,),
        in_specs=[pl.BlockSpec((bm, d), lambda i: (i, 0), pipeline_mode=pl.Buffered(1)),
                  _resident(cp.shape), _resident((1, d)), _resident((1, d)),
                  hbm, hbm, hbm],
        out_specs=pl.BlockSpec((bm, d), lambda i: (i, 0)),
        out_shape=jax.ShapeDtypeStruct((t, d), F32),
        scratch_shapes=[pltpu.VMEM((bm, d), BF16),
                        pltpu.VMEM((bm, bf), BF16), pltpu.VMEM((bm, bf), BF16),
                        pltpu.VMEM((2 * bf // LANES, FFN_TAIL + bm, LANES), F32),
                        pltpu.VMEM((nf, FFN_TAIL, 2 * bf), F32),
                        pltpu.VMEM((2, d, bf), BF16), pltpu.VMEM((2, d, bf), BF16),
                        pltpu.VMEM((2, bf, d), BF16), pltpu.SemaphoreType.DMA((3, 2))],
        compiler_params=_cparams("arbitrary"),
        name="conv_ffn_ln3",
    )(x2, cp, g, b, wg, wu, w2)


def _pad_cols(a, n):
    return jnp.pad(a, ((0, 0), (0, n - a.shape[1])))


def kernel(x, mem, positions, w_in, gla_gate_w2, gla_gate_b, gla_norm_g, w_out, ln1_g, ln1_b,
           ca_wq, ca_wkv, ca_wo, ln2_g, ln2_b, ffn_w_in, ffn_conv_w, ffn_conv_b, ffn_w_out,
           ln3_g, ln3_b):
    bsz, seq, d = x.shape
    depth = w_in.shape[0]
    t = bsz * seq
    alpha = (2.0 * depth) ** 0.25
    d_ff = ffn_w_out.shape[1]
    dff_pad = -(-d_ff // FFN_CHUNK) * FFN_CHUNK

    nqk = GLA_HEADS * GLA_DK
    nv = GLA_HEADS * GLA_DV
    c_glr = 2 * nqk + 2 * nv
    c_dil = c_glr + GLA_GATE_RANK

    half = ROPE_HALF
    inv_freq = ROPE_THETA ** (-jnp.arange(0, ROPE_DIMS, 2, dtype=F32) / ROPE_DIMS)
    inv_row = jnp.concatenate([inv_freq, inv_freq, jnp.zeros((LANES - 2 * half,), F32)])[None, :]
    pos_col = positions.astype(F32).reshape(t, 1)
    cosf, sina, sinb = _rope_tables(pos_col, inv_row, min(t, 2048))

    x2d = x.reshape(t, d)
    for l in range(depth):
        wl = w_in[l]
        wcat = jnp.concatenate([wl[:, :c_glr], wl[:, c_dil:]], axis=1).astype(BF16)
        wglr = _pad_cols(wl[:, c_glr:c_dil], LANES).astype(BF16)
        h, glr = _inproj(x2d, wcat, wglr, cosf, sina, sinb, min(t, 512))
        h3 = h.reshape(bsz, seq, h.shape[1])

        w2p = jnp.pad(gla_gate_w2[l], ((0, LANES - GLA_GATE_RANK), (0, 0))).astype(BF16)
        og = _gla(h3, glr.reshape(bsz, seq, LANES), w2p, gla_gate_b[l][None, :],
                  gla_norm_g[l][None, :], min(seq, 512))
        od = _dil(h3)

        wo = w_out[l].astype(BF16)
        x1 = _outproj(og.reshape(t, nv), od.reshape(t, DIL_HEADS * DIL_HD), wo[:nv], wo[nv:],
                      x2d, ln1_g[l][None, :], ln1_b[l][None, :], min(t, 512), alpha)

        kv = _memkv(mem.reshape(-1, d), ca_wkv[l].astype(BF16), 1024)
        x2 = _cross(x1, ca_wq[l].astype(BF16), kv.reshape(bsz, -1, 2 * d), ca_wo[l].astype(BF16),
                    ln2_g[l][None, :], ln2_b[l][None, :], min(seq, 512), seq, alpha)

        w1 = ffn_w_in[l]
        cw = ffn_conv_w[l]
        cb = ffn_conv_b[l][None, :]
        nf = dff_pad // FFN_CHUNK

        def chunked_cols(a):
            return _pad_cols(a, dff_pad).reshape(a.shape[0], nf, FFN_CHUNK).transpose(1, 0, 2)

        conv = jnp.concatenate([cw[:, :d_ff], cb[:, :d_ff], cw[:, d_ff:], cb[:, d_ff:]], axis=0)
        w2 = jnp.pad(ffn_w_out[l], ((0, dff_pad - d_ff), (0, 0))).astype(BF16)
        x2d = _ffn(x2, chunked_cols(w1[:, :d_ff]).astype(BF16),
                   chunked_cols(w1[:, d_ff:]).astype(BF16), chunked_cols(conv),
                   w2.reshape(nf, FFN_CHUNK, d),
                   ln3_g[l][None, :], ln3_b[l][None, :], min(seq, 1024), seq, alpha)
    return x2d.reshape(bsz, seq, d)
```

```python
import functools

import jax
import jax.numpy as jnp
from jax import lax
from jax.experimental import pallas as pl
from jax.experimental.pallas import tpu as pltpu

F32 = jnp.float32
BF16 = jnp.bfloat16

LANES = 128
LN_EPS = 1e-5
GLA_HEADS = 4
GLA_DK = 128
GLA_DV = 256
GLA_GATE_RANK = 16
GLA_TAU = 16.0
GLA_CHUNK = 64
DIL_HD = 128
DIL_HEADS = 8
DIL_PATTERNS = ((128, 1), (512, 4), (2048, 16))
DIL_BAND = 128
DIL_UNROLL = 4
ROPE_THETA = 500000.0
ROPE_DIMS = 32
ROPE_HALF = ROPE_DIMS // 2
CA_HEADS = 4
CONV_W = 3
INPROJ_BLOCK = 1024
INPROJ_SUB = 256
FFN_CHUNK = 512
FFN_SUB = 256
FFN_ROWS = 512
FFN_TAIL = 8
VMEM_LIMIT = 56 * 1024 * 1024


def _cparams(*sem):
    return pltpu.CompilerParams(dimension_semantics=sem, vmem_limit_bytes=VMEM_LIMIT)


def _dot(a, b):
    return jnp.dot(a, b, preferred_element_type=F32)


def _dot_nt(a, b):
    return lax.dot_general(a, b, (((1,), (1,)), ((), ())), preferred_element_type=F32)


def _dot_tn(a, b):
    return lax.dot_general(a, b, (((0,), (0,)), ((), ())), preferred_element_type=F32)


def _layer_norm(y, g, b):
    mu = jnp.mean(y, axis=-1, keepdims=True)
    d = y - mu
    var = jnp.mean(d * d, axis=-1, keepdims=True)
    return d * lax.rsqrt(var + LN_EPS) * g + b


def _sigmoid(x):
    return 1.0 / (1.0 + jnp.exp(-x))


def _rope_kernel(pos_ref, inv_ref, cos_ref, sa_ref, sb_ref):
    ang = pos_ref[...] * inv_ref[...]
    lane = lax.broadcasted_iota(jnp.int32, ang.shape, 1)
    c = jnp.cos(ang)
    s = jnp.sin(ang)
    cos_ref[...] = jnp.where(lane < ROPE_DIMS, c, 1.0)
    sa_ref[...] = jnp.where(lane < ROPE_HALF, -s, 0.0)
    sb_ref[...] = jnp.where(lane < ROPE_HALF, 0.0, jnp.where(lane < ROPE_DIMS, s, 0.0))


def _rope_tables(pos_col, inv_row, bs):
    t = pos_col.shape[0]
    out = jax.ShapeDtypeStruct((t, LANES), F32)
    spec = pl.BlockSpec((bs, LANES), lambda i: (i, 0))
    return pl.pallas_call(
        _rope_kernel,
        grid=(t // bs,),
        in_specs=[pl.BlockSpec((bs, 1), lambda i: (i, 0)),
                  pl.BlockSpec((1, LANES), lambda i: (0, 0))],
        out_specs=[spec, spec, spec],
        out_shape=[out, out, out],
        compiler_params=_cparams("parallel"),
        name="rope_tables",
    )(pos_col, inv_row)


def _rope(t, cos, sa, sb):
    return t * cos + pltpu.roll(t, LANES - ROPE_HALF, 1) * sa + pltpu.roll(t, ROPE_HALF, 1) * sb


def _inproj_kernel(x_ref, w_ref, wg_ref, cos_ref, sa_ref, sb_ref, h_ref, glr_ref, xb_ref):
    bn = INPROJ_BLOCK
    xb_ref[...] = x_ref[...].astype(BF16)
    glr_ref[...] = _dot(xb_ref[...], wg_ref[...])

    def rope(scale):
        def epilogue(acc, c0):
            cos, sa, sb = cos_ref[...], sa_ref[...], sb_ref[...]
            heads = [acc[:, j:j + LANES] for j in range(0, acc.shape[1], LANES)]
            if scale is not None:
                heads = [t * scale for t in heads]
            return jnp.concatenate([_rope(t, cos, sa, sb) for t in heads], axis=1)
        return epilogue

    def plain(acc, c0):
        return acc

    epilogues = (lambda acc, c0: acc * (GLA_DK ** -0.5) if c0 < bn // 2 else acc,
                 plain, plain, rope(DIL_HD ** -0.5), rope(None), plain)
    assert len(epilogues) * bn == w_ref.shape[1]

    for n, epilogue in enumerate(epilogues):
        for c0 in range(0, bn, INPROJ_SUB):
            cols = slice(n * bn + c0, n * bn + c0 + INPROJ_SUB)
            h_ref[:, cols] = epilogue(_dot(xb_ref[...], w_ref[:, cols]), c0).astype(h_ref.dtype)


def _inproj(x2d, wcat, wglr, cosf, sina, sinb, bm):
    t, d = x2d.shape
    ncols = wcat.shape[1]
    tab = pl.BlockSpec((bm, LANES), lambda i: (i, 0))
    return pl.pallas_call(
        _inproj_kernel,
        grid=(t // bm,),
        in_specs=[pl.BlockSpec((bm, d), lambda i: (i, 0)),
                  _resident((d, ncols)), _resident((d, LANES)), tab, tab, tab],
        out_specs=[pl.BlockSpec((bm, ncols), lambda i: (i, 0)),
                   pl.BlockSpec((bm, LANES), lambda i: (i, 0))],
        out_shape=[jax.ShapeDtypeStruct((t, ncols), BF16),
                   jax.ShapeDtypeStruct((t, LANES), F32)],
        scratch_shapes=[pltpu.VMEM((bm, d), BF16)],
        compiler_params=_cparams("parallel"),
        name="in_projection",
    )(x2d, wcat, wglr, cosf, sina, sinb)


def _split3(v):
    hi = v.astype(BF16)
    r1 = v - hi.astype(F32)
    mid = r1.astype(BF16)
    lo = (r1 - mid.astype(F32)).astype(BF16)
    return hi, mid, lo


def _gla_kernel(q_ref, k_ref, v_ref, r_ref, glr_ref, w2_ref, gb_ref, ng_ref, o_ref, st_ref):
    c = GLA_CHUNK
    sb = q_ref.shape[1]
    grp = 4 * c

    @pl.when(pl.program_id(2) == 0)
    def _():
        st_ref[...] = jnp.zeros_like(st_ref)

    z = _dot(glr_ref[0].astype(BF16), w2_ref[...]) + gb_ref[...]
    lg = (jnp.minimum(z, 0.0) - jnp.log1p(jnp.exp(-jnp.abs(z)))) / GLA_TAU

    row = lax.broadcasted_iota(jnp.int32, (2 * grp, grp), 0)
    col = lax.broadcasted_iota(jnp.int32, (2 * grp, grp), 1)
    rr = jnp.where(row < grp, row, row - grp)
    shift = c.bit_length() - 1
    same = (rr >> shift) == (col >> shift)
    lower = jnp.where(same & (col <= rr), 1.0, 0.0)
    upper = jnp.where(same & (col > rr), 1.0, 0.0)
    lu = jnp.where(row < grp, lower, upper).astype(BF16)
    b_parts, e_parts = [], []
    for g0 in range(0, sb, grp):
        pieces = jnp.concatenate(_split3(lg[g0:g0 + grp]), axis=1)
        res = _dot(lu, pieces)
        tot = res[:, :LANES] + res[:, LANES:2 * LANES] + res[:, 2 * LANES:]
        b_parts.append(tot[:grp])
        e_parts.append(tot[grp:])
    b = jnp.concatenate(b_parts, axis=0)
    brest = jnp.concatenate(e_parts, axis=0)

    qf = q_ref[0].astype(F32)
    kf = k_ref[0].astype(F32)
    q_in = (qf * jnp.exp(b)).astype(BF16)
    k_in = (kf * jnp.exp(-b)).astype(BF16)
    k_end = (kf * jnp.exp(brest)).astype(BF16)
    v = v_ref[0]

    ci = lax.broadcasted_iota(jnp.int32, (c, c), 0)
    cj = lax.broadcasted_iota(jnp.int32, (c, c), 1)
    causal = cj <= ci

    st = st_ref[...]
    outs = []
    for i in range(sb // c):
        rows = slice(i * c, (i + 1) * c)
        a = jnp.where(causal, _dot_nt(q_in[rows], k_in[rows]), 0.0).astype(BF16)
        o = _dot(a, v[rows]) + _dot_nt(q_in[rows], st.astype(BF16))
        outs.append(o)
        decay = jnp.exp(b[i * c + c - 1:i * c + c, :])
        st = st * decay + _dot_tn(v[rows], k_end[rows])
    st_ref[...] = st

    o = jnp.concatenate(outs, axis=0)
    mu = jnp.mean(o, axis=-1, keepdims=True)
    d = o - mu
    var = jnp.mean(d * d, axis=-1, keepdims=True)
    rg = r_ref[0].astype(F32)
    y = d * lax.rsqrt(var + LN_EPS) * ng_ref[...] * (rg * _sigmoid(rg))
    o_ref[0] = y.astype(o_ref.dtype)


def _gla(h3, glr3, w2p, gb, ng, sb):
    bsz, s, _ = h3.shape
    kb = GLA_HEADS * GLA_DK // LANES
    vb = 2 * GLA_HEADS * GLA_DK // GLA_DV
    rb = vb + GLA_HEADS
    return pl.pallas_call(
        _gla_kernel,
        grid=(bsz, GLA_HEADS, s // sb),
        in_specs=[pl.BlockSpec((1, sb, GLA_DK), lambda b, h, j: (b, j, h)),
                  pl.BlockSpec((1, sb, GLA_DK), lambda b, h, j: (b, j, kb + h)),
                  pl.BlockSpec((1, sb, GLA_DV), lambda b, h, j: (b, j, vb + h)),
                  pl.BlockSpec((1, sb, GLA_DV), lambda b, h, j: (b, j, rb + h)),
                  pl.BlockSpec((1, sb, LANES), lambda b, h, j: (b, j, 0)),
                  pl.BlockSpec((LANES, GLA_DK), lambda b, h, j: (0, h)),
                  pl.BlockSpec((1, GLA_DK), lambda b, h, j: (0, h)),
                  pl.BlockSpec((1, GLA_DV), lambda b, h, j: (0, h))],
        out_specs=pl.BlockSpec((1, sb, GLA_DV), lambda b, h, j: (b, j, h)),
        out_shape=jax.ShapeDtypeStruct((bsz, s, GLA_HEADS * GLA_DV), BF16),
        scratch_shapes=[pltpu.VMEM((GLA_DV, GLA_DK), F32)],
        compiler_params=_cparams("parallel", "parallel", "arbitrary"),
        name="gla",
    )(h3, h3, h3, h3, glr3, w2p, gb, ng)


def _dil_kernel(q_ref, k_ref, v_ref, o_ref, qf, kf, vf, qg, kg, vg, qc, kc, vc, ob, db, mx,
                scb, eb):
    s = q_ref.shape[1]
    band = DIL_BAND
    unroll = DIL_UNROLL
    nblk = s // band
    qf[...] = q_ref[0].astype(F32)
    kf[...] = k_ref[0].astype(F32)
    vf[...] = v_ref[0].astype(F32)
    kc[:band, :] = jnp.zeros((band, DIL_HD), BF16)
    vc[:band, :] = jnp.zeros((band, DIL_HD), BF16)

    qi = lax.broadcasted_iota(jnp.int32, (band, 2 * band), 0)
    kj = lax.broadcasted_iota(jnp.int32, (band, 2 * band), 1)
    allowed = (kj >= qi) & (kj <= qi + band)
    bias = jnp.where(allowed, 0.0, -jnp.inf).astype(F32)
    bias0 = jnp.where(allowed & (kj >= band), 0.0, -jnp.inf).astype(F32)
    ones = jnp.ones((2 * band, LANES), BF16)

    for p, (window, dil) in enumerate(DIL_PATTERNS):
        assert window // dil == band
        cls = s // dil
        nb = cls // band
        span = band * dil
        assert nblk % unroll == 0 and (nb % unroll == 0 or unroll % nb == 0)

        if dil == 1:
            qc[...] = q_ref[0]
            kc[band:, :] = k_ref[0]
            vc[band:, :] = v_ref[0]
        else:
            prev = DIL_PATTERNS[p - 1][1]
            step = dil // prev
            assert step * prev == dil and step in (2, 4)
            keep = p + 1 < len(DIL_PATTERNS)
            srcs, dsts = ((qf, kf, vf), (qg, kg, vg)) if p % 2 == 1 else ((qg, kg, vg), (qf, kf, vf))
            for r in range(dil):
                rows = pl.ds((r % prev) * (s // prev) + r // prev, cls, stride=step)
                for src, dst, cm, off in zip(srcs, dsts, (qc, kc, vc), (0, band, band)):
                    x = src[rows, :]
                    if keep:
                        dst[r * cls:(r + 1) * cls, :] = x
                    cm[off + r * cls:off + (r + 1) * cls, :] = x.astype(BF16)

        def out_rows(g, lo=0, cnt=band, dil=dil, nb=nb, span=span):
            start = g // nb + (g % nb) * span + lo * dil
            return pl.ds(start, cnt) if dil == 1 else pl.ds(start, cnt, stride=dil)

        def scores(t, slot):
            for u in range(unroll):
                g = t * unroll + u
                k2 = kc[g * band:(g + 2) * band, :]
                scb[slot, u * band:(u + 1) * band, :] = _dot_nt(qc[g * band:(g + 1) * band, :], k2)

        def softmax(t, slot, p=p, nb=nb, out_rows=out_rows):
            for u in range(unroll):
                g = t * unroll + u
                bb = bias0 if g % nb == 0 else bias
                rows = slice(u * band, (u + 1) * band)
                m = jnp.max(scb[slot, rows, :] + bb, axis=-1, keepdims=True)
                mx[p, out_rows(g), :] = jnp.broadcast_to(m, (band, LANES))
                for half in range(2):
                    cols = slice(half * band, (half + 1) * band)
                    eb[slot, rows, cols] = jnp.exp(scb[slot, rows, cols] + bb[:, cols] - m).astype(BF16)

        def values(t, slot, p=p, out_rows=out_rows):
            for u in range(unroll):
                g = t * unroll + u
                v2 = vc[g * band:(g + 2) * band, :]
                oe = _dot(eb[slot, u * band:(u + 1) * band, :],
                          jnp.concatenate([v2, ones], axis=1))
                ob[p, out_rows(g), :] = oe[:, :DIL_HD]
                db[p, out_rows(g), :] = oe[:, DIL_HD:]

        ngrp = nblk // unroll
        for t in range(ngrp + 2):
            if t < ngrp:
                scores(t, t % 2)
            if 1 <= t <= ngrp:
                softmax(t - 1, (t - 1) % 2)
            if t >= 2:
                values(t - 2, t % 2)

    mb = 512

    def merge(i, carry):
        rs = pl.ds(pl.multiple_of(i * mb, mb), mb)
        m0, m1, m2 = mx[0, rs, :], mx[1, rs, :], mx[2, rs, :]
        m = jnp.maximum(jnp.maximum(m0, m1), m2)
        e0, e1, e2 = jnp.exp(m0 - m), jnp.exp(m1 - m), jnp.exp(m2 - m)
        num = e0 * ob[0, rs, :] + e1 * ob[1, rs, :] + e2 * ob[2, rs, :]
        den = e0 * db[0, rs, :] + e1 * db[1, rs, :] + e2 * db[2, rs, :]
        o_ref[0, rs, :] = (num / den).astype(o_ref.dtype)
        return carry

    lax.fori_loop(0, s // mb, merge, 0)


def _dil(h3):
    bsz, s, _ = h3.shape
    qb = (2 * GLA_HEADS * GLA_DK + 2 * GLA_HEADS * GLA_DV) // DIL_HD
    kb = qb + DIL_HEADS
    vb = kb + DIL_HEADS
    npat = len(DIL_PATTERNS)
    return pl.pallas_call(
        _dil_kernel,
        grid=(bsz, DIL_HEADS),
        in_specs=[pl.BlockSpec((1, s, DIL_HD), lambda b, h: (b, 0, qb + h)),
                  pl.BlockSpec((1, s, DIL_HD), lambda b, h: (b, 0, kb + h)),
                  pl.BlockSpec((1, s, DIL_HD), lambda b, h: (b, 0, vb + h))],
        out_specs=pl.BlockSpec((1, s, DIL_HD), lambda b, h: (b, 0, h)),
        out_shape=jax.ShapeDtypeStruct((bsz, s, DIL_HEADS * DIL_HD), BF16),
        scratch_shapes=[pltpu.VMEM((s, DIL_HD), F32)] * 6 + [
                        pltpu.VMEM((s, DIL_HD), BF16), pltpu.VMEM((s + DIL_BAND, DIL_HD), BF16),
                        pltpu.VMEM((s + DIL_BAND, DIL_HD), BF16),
                        pltpu.VMEM((npat, s, DIL_HD), F32), pltpu.VMEM((npat, s, LANES), F32),
                        pltpu.VMEM((npat, s, LANES), F32),
                        pltpu.VMEM((2, DIL_UNROLL * DIL_BAND, 2 * DIL_BAND), F32),
                        pltpu.VMEM((2, DIL_UNROLL * DIL_BAND, 2 * DIL_BAND), BF16)],
        compiler_params=_cparams("parallel", "parallel"),
        name="dilated_attention",
    )(h3, h3, h3)


def _outproj_kernel(og_ref, od_ref, wa_ref, wb_ref, x_ref, g_ref, b_ref, o_ref, *, alpha):
    acc = _dot(og_ref[...], wa_ref[...]) + _dot(od_ref[...], wb_ref[...])
    o_ref[...] = _layer_norm(alpha * x_ref[...] + acc, g_ref[...], b_ref[...])


def _resident(shape):
    return pl.BlockSpec(shape, lambda *_: (0,) * len(shape), pipeline_mode=pl.Buffered(1))


def _outproj(og, od, wa, wb, x2d, g, b, bm, alpha):
    t, d = x2d.shape
    ka, kb = og.shape[1], od.shape[1]
    return pl.pallas_call(
        functools.partial(_outproj_kernel, alpha=alpha),
        grid=(t // bm,),
        in_specs=[pl.BlockSpec((bm, ka), lambda i: (i, 0)),
                  pl.BlockSpec((bm, kb), lambda i: (i, 0)),
                  _resident((ka, d)), _resident((kb, d)),
                  pl.BlockSpec((bm, d), lambda i: (i, 0)),
                  _resident((1, d)), _resident((1, d))],
        out_specs=pl.BlockSpec((bm, d), lambda i: (i, 0)),
        out_shape=jax.ShapeDtypeStruct((t, d), F32),
        compiler_params=_cparams("parallel"),
        name="out_projection_ln1",
    )(og, od, wa, wb, x2d, g, b)


def _memkv_kernel(m_ref, w_ref, o_ref):
    o_ref[...] = _dot(m_ref[...].astype(BF16), w_ref[...]).astype(o_ref.dtype)


def _memkv(mem2d, wkv, bn):
    t, d = mem2d.shape
    n = wkv.shape[1]
    return pl.pallas_call(
        _memkv_kernel,
        grid=(n // bn,),
        in_specs=[pl.BlockSpec((t, d), lambda j: (0, 0)),
                  pl.BlockSpec((d, bn), lambda j: (0, j))],
        out_specs=pl.BlockSpec((t, bn), lambda j: (0, j)),
        out_shape=jax.ShapeDtypeStruct((t, n), BF16),
        compiler_params=_cparams("parallel"),
        name="memory_kv_projection",
    )(mem2d, wkv)


def _cross_kernel(x_ref, wq_ref, kv_ref, wo_ref, g_ref, b_ref, o_ref, oc_ref, *, alpha):
    d = x_ref.shape[1]
    hd = d // CA_HEADS
    x = x_ref[...]
    q = _dot(x.astype(BF16), wq_ref[...]).astype(BF16)
    for h in range(CA_HEADS):
        cols = slice(h * hd, (h + 1) * hd)
        mk = kv_ref[0, :, cols]
        mv = kv_ref[0, :, d + h * hd:d + (h + 1) * hd]
        sc = _dot_nt(q[:, cols], mk) * (hd ** -0.5)
        m = jnp.max(sc, axis=-1, keepdims=True)
        e = jnp.exp(sc - m)
        p = e / jnp.sum(e, axis=-1, keepdims=True)
        oc_ref[:, cols] = _dot(p.astype(BF16), mv).astype(BF16)
    y = alpha * x + _dot(oc_ref[...], wo_ref[...])
    o_ref[...] = _layer_norm(y, g_ref[...], b_ref[...])


def _cross(x1, wq, kv3, wo, g, b, bm, seq, alpha):
    t, d = x1.shape
    m = kv3.shape[1]
    per = seq // bm
    return pl.pallas_call(
        functools.partial(_cross_kernel, alpha=alpha),
        grid=(t // bm,),
        in_specs=[pl.BlockSpec((bm, d), lambda i: (i, 0)),
                  _resident((d, d)),
                  pl.BlockSpec((1, m, 2 * d), lambda i: (i // per, 0, 0)),
                  _resident((d, d)),
                  _resident((1, d)), _resident((1, d))],
        out_specs=pl.BlockSpec((bm, d), lambda i: (i, 0)),
        out_shape=jax.ShapeDtypeStruct((t, d), F32),
        scratch_shapes=[pltpu.VMEM((bm, d), BF16)],
        compiler_params=_cparams("parallel"),
        name="cross_attention_ln2",
    )(x1, wq, kv3, wo, g, b)


def _causal_conv(u_ref, cw, cb, r0, n):
    h = FFN_TAIL + r0
    y = cb + cw[0:1, :] * u_ref[h - 2:h - 2 + n, :]
    y = y + cw[1:2, :] * u_ref[h - 1:h - 1 + n, :]
    return y + cw[2:3, :] * u_ref[h:h + n, :]


def _ffn_kernel(x_ref, cp_ref, g_ref, b_ref, w1_hbm, w2_hbm, o_ref,
                xb_ref, act0_ref, act1_ref, us_ref, tail_ref, wg_buf, wu_buf, w2_buf,
                sem, *, alpha, per):
    i = pl.program_id(0)
    d_ff = w2_hbm.shape[0]
    bm = x_ref.shape[0]
    bf = wg_buf.shape[2]
    nf = pl.cdiv(d_ff, bf)
    last_w = d_ff - (nf - 1) * bf
    assert nf % 2 == 1 and nf >= 3
    assert bf % FFN_SUB == 0 and last_w % LANES == 0 and bm % FFN_ROWS == 0
    c_first = i * nf
    acts = (act0_ref, act1_ref)
    nslab = bf // LANES
    half = o_ref.shape[1] // 2

    def width(f):
        return last_w if isinstance(f, int) and f == nf - 1 else bf

    def up_copies(f, slot):
        w = width(f)
        col = f * bf if isinstance(f, int) else pl.multiple_of(f * bf, bf)
        return (pltpu.make_async_copy(w1_hbm.at[:, pl.ds(col, w)],
                                      wg_buf.at[slot, :, pl.ds(0, w)], sem.at[0, slot]),
                pltpu.make_async_copy(w1_hbm.at[:, pl.ds(d_ff + col, w)],
                                      wu_buf.at[slot, :, pl.ds(0, w)], sem.at[1, slot]))

    def down_copy(f, slot):
        w = width(f)
        row = f * bf if isinstance(f, int) else pl.multiple_of(f * bf, bf)
        return pltpu.make_async_copy(w2_hbm.at[pl.ds(row, w), :],
                                     w2_buf.at[slot, pl.ds(0, w), :], sem.at[2, slot])

    def region_copies(f):
        slot = lax.rem(c_first + f, 2)
        for cp in up_copies(f, slot):
            cp.wait()
        if not (isinstance(f, int) and f == 0):
            down_copy(f - 1, 1 - slot).wait()
        if isinstance(f, int) and f == nf - 1:
            @pl.when(i + 1 < pl.num_programs(0))
            def _():
                for cp in up_copies(0, 1 - slot):
                    cp.start()
        else:
            for cp in up_copies(f + 1, 1 - slot):
                cp.start()
        down_copy(f, slot).start()
        return slot

    def up_matmul(slot, c0, wcols, r0):
        xr = xb_ref[r0:r0 + FFN_ROWS, :]
        rows = slice(FFN_TAIL + r0, FFN_TAIL + r0 + FFN_ROWS)
        for base, w_buf in ((0, wg_buf), (nslab, wu_buf)):
            res = _dot(xr, w_buf[slot, :, c0:c0 + wcols])
            for k in range(wcols // LANES):
                us_ref[base + c0 // LANES + k, rows, :] = res[:, k * LANES:(k + 1) * LANES]

    def activate(cv, act_ref, c0, wcols, r0):
        for s in range(c0 // LANES, (c0 + wcols) // LANES):
            cols = slice(s * LANES, (s + 1) * LANES)
            gate = _causal_conv(us_ref.at[s], cv[0:3, cols], cv[3:4, cols], r0, FFN_ROWS)
            up = _causal_conv(us_ref.at[nslab + s], cv[4:7, cols], cv[7:8, cols], r0, FFN_ROWS)
            act_ref[r0:r0 + FFN_ROWS, cols] = (gate * _sigmoid(gate) * up).astype(BF16)

    def down_matmul(slot, act_ref, w, n0):
        o_ref[:, n0:n0 + half] += _dot(act_ref[:, :w], w2_buf[slot, :w, n0:n0 + half])

    def region(f, act_slot):
        slot = region_copies(f)
        w = width(f)
        cv = cp_ref[f]
        act_ref, prev_ref = acts[act_slot], acts[1 - act_slot]
        slabs = [s for base in (0, nslab) for s in range(base, base + w // LANES)]
        for s in slabs:
            us_ref[s, :FFN_TAIL, :] = tail_ref[f, :, s * LANES:(s + 1) * LANES]
        units = [(c0, min(FFN_SUB, w - c0), r0) for c0 in range(0, w, FFN_SUB)
                 for r0 in range(0, bm, FFN_ROWS)]
        for unit in units:
            up_matmul(slot, *unit)
        if not (isinstance(f, int) and f == 0):
            for n0 in (0, half):
                down_matmul(1 - slot, prev_ref, width(f - 1) if isinstance(f, int) else bf, n0)
        for unit in units:
            activate(cv, act_ref, *unit)
        for s in slabs:
            tail_ref[f, :, s * LANES:(s + 1) * LANES] = us_ref[s, bm:bm + FFN_TAIL, :]

    @pl.when(i == 0)
    def _():
        for cp in up_copies(0, 0):
            cp.start()

    @pl.when(i % per == 0)
    def _():
        tail_ref[...] = jnp.zeros_like(tail_ref)

    xb_ref[...] = x_ref[...].astype(BF16)
    o_ref[...] = jnp.zeros_like(o_ref)

    region(0, 0)

    def pair(j, carry):
        region(2 * j + 1, 1)
        region(2 * j + 2, 0)
        return carry

    lax.fori_loop(0, (nf - 3) // 2, pair, 0)
    region(nf - 2, 1)
    region(nf - 1, 0)

    last = lax.rem(c_first + nf - 1, 2)
    down_copy(nf - 1, last).wait()
    for n0 in (0, half):
        down_matmul(last, acts[0], last_w, n0)
    o_ref[...] = _layer_norm(alpha * x_ref[...] + o_ref[...], g_ref[...], b_ref[...])


def _ffn(x2, w1, w2, cp, g, b, bm, seq, alpha):
    t, d = x2.shape
    bf = FFN_CHUNK
    nf = cp.shape[0]
    hbm = pl.BlockSpec(memory_space=pl.ANY)
    return pl.pallas_call(
        functools.partial(_ffn_kernel, alpha=alpha, per=seq // bm),
        grid=(t // bm,),
        in_specs=[pl.BlockSpec((bm, d), lambda i: (i, 0), pipeline_mode=pl.Buffered(1)),
                  _resident(cp.shape), _resident((1, d)), _resident((1, d)), hbm, hbm],
        out_specs=pl.BlockSpec((bm, d), lambda i: (i, 0)),
        out_shape=jax.ShapeDtypeStruct((t, d), F32),
        scratch_shapes=[pltpu.VMEM((bm, d), BF16),
                        pltpu.VMEM((bm, bf), BF16), pltpu.VMEM((bm, bf), BF16),
                        pltpu.VMEM((2 * bf // LANES, FFN_TAIL + bm, LANES), F32),
                        pltpu.VMEM((nf, FFN_TAIL, 2 * bf), F32),
                        pltpu.VMEM((2, d, bf), BF16), pltpu.VMEM((2, d, bf), BF16),
                        pltpu.VMEM((2, bf, d), BF16), pltpu.SemaphoreType.DMA((3, 2))],
        compiler_params=_cparams("arbitrary"),
        name="conv_ffn_ln3",
    )(x2, cp, g, b, w1, w2)


def _pad_cols(a, n):
    return jnp.pad(a, ((0, 0), (0, n - a.shape[1])))


def kernel(x, mem, positions, w_in, gla_gate_w2, gla_gate_b, gla_norm_g, w_out, ln1_g, ln1_b,
           ca_wq, ca_wkv, ca_wo, ln2_g, ln2_b, ffn_w_in, ffn_conv_w, ffn_conv_b, ffn_w_out,
           ln3_g, ln3_b):
    bsz, seq, d = x.shape
    depth = w_in.shape[0]
    t = bsz * seq
    alpha = (2.0 * depth) ** 0.25
    d_ff = ffn_w_out.shape[1]
    dff_pad = -(-d_ff // FFN_CHUNK) * FFN_CHUNK

    nqk = GLA_HEADS * GLA_DK
    nv = GLA_HEADS * GLA_DV
    c_glr = 2 * nqk + 2 * nv
    c_dil = c_glr + GLA_GATE_RANK

    half = ROPE_HALF
    inv_freq = ROPE_THETA ** (-jnp.arange(0, ROPE_DIMS, 2, dtype=F32) / ROPE_DIMS)
    inv_row = jnp.concatenate([inv_freq, inv_freq, jnp.zeros((LANES - 2 * half,), F32)])[None, :]
    pos_col = positions.astype(F32).reshape(t, 1)
    cosf, sina, sinb = _rope_tables(pos_col, inv_row, min(t, 2048))

    x2d = x.reshape(t, d)
    for l in range(depth):
        wl = w_in[l]
        wcat = jnp.concatenate([wl[:, :c_glr], wl[:, c_dil:]], axis=1).astype(BF16)
        wglr = _pad_cols(wl[:, c_glr:c_dil], LANES).astype(BF16)
        h, glr = _inproj(x2d, wcat, wglr, cosf, sina, sinb, min(t, 512))
        h3 = h.reshape(bsz, seq, h.shape[1])

        w2p = jnp.pad(gla_gate_w2[l], ((0, LANES - GLA_GATE_RANK), (0, 0))).astype(BF16)
        og = _gla(h3, glr.reshape(bsz, seq, LANES), w2p, gla_gate_b[l][None, :],
                  gla_norm_g[l][None, :], min(seq, 1024))
        od = _dil(h3)

        wo = w_out[l].astype(BF16)
        x1 = _outproj(og.reshape(t, nv), od.reshape(t, DIL_HEADS * DIL_HD), wo[:nv], wo[nv:],
                      x2d, ln1_g[l][None, :], ln1_b[l][None, :], min(t, 512), alpha)

        kv = _memkv(mem.reshape(-1, d), ca_wkv[l].astype(BF16), 1024)
        x2 = _cross(x1, ca_wq[l].astype(BF16), kv.reshape(bsz, -1, 2 * d), ca_wo[l].astype(BF16),
                    ln2_g[l][None, :], ln2_b[l][None, :], min(seq, 512), seq, alpha)

        cw = ffn_conv_w[l]
        cb = ffn_conv_b[l][None, :]
        nf = dff_pad // FFN_CHUNK
        conv = jnp.concatenate([cw[:, :d_ff], cb[:, :d_ff], cw[:, d_ff:], cb[:, d_ff:]], axis=0)
        conv = _pad_cols(conv, dff_pad).reshape(conv.shape[0], nf, FFN_CHUNK).transpose(1, 0, 2)
        x2d = _ffn(x2, ffn_w_in[l].astype(BF16), ffn_w_out[l].astype(BF16), conv,
                   ln3_g[l][None, :], ln3_b[l][None, :], min(seq, 1024), seq, alpha)
    return x2d.reshape(bsz, seq, d)
```

```python
import functools

import jax
import jax.numpy as jnp
from jax import lax
from jax.experimental import pallas as pl
from jax.experimental.pallas import tpu as pltpu

F32 = jnp.float32
BF16 = jnp.bfloat16

LANES = 128
LN_EPS = 1e-5
GLA_HEADS = 4
GLA_DK = 128
GLA_DV = 256
GLA_GATE_RANK = 16
GLA_TAU = 16.0
GLA_CHUNK = 64
DIL_HD = 128
DIL_HEADS = 8
DIL_PATTERNS = ((128, 1), (512, 4), (2048, 16))
DIL_BAND = 128
DIL_UNROLL = 4
ROPE_THETA = 500000.0
ROPE_DIMS = 32
ROPE_HALF = ROPE_DIMS // 2
CA_HEADS = 4
CONV_W = 3
INPROJ_BLOCK = 1024
INPROJ_SUB = 256
FFN_CHUNK = 512
FFN_SUB = 256
FFN_ROWS = 512
FFN_TAIL = 8
VMEM_LIMIT = 56 * 1024 * 1024


def _cparams(*sem):
    return pltpu.CompilerParams(dimension_semantics=sem, vmem_limit_bytes=VMEM_LIMIT)


def _dot(a, b):
    return jnp.dot(a, b, preferred_element_type=F32)


def _dot_nt(a, b):
    return lax.dot_general(a, b, (((1,), (1,)), ((), ())), preferred_element_type=F32)


def _dot_tn(a, b):
    return lax.dot_general(a, b, (((0,), (0,)), ((), ())), preferred_element_type=F32)


def _layer_norm(y, g, b):
    mu = jnp.mean(y, axis=-1, keepdims=True)
    d = y - mu
    var = jnp.mean(d * d, axis=-1, keepdims=True)
    return d * lax.rsqrt(var + LN_EPS) * g + b


def _sigmoid(x):
    return 1.0 / (1.0 + jnp.exp(-x))


def _rope_kernel(pos_ref, inv_ref, cos_ref, sa_ref, sb_ref):
    ang = pos_ref[...] * inv_ref[...]
    lane = lax.broadcasted_iota(jnp.int32, ang.shape, 1)
    c = jnp.cos(ang)
    s = jnp.sin(ang)
    cos_ref[...] = jnp.where(lane < ROPE_DIMS, c, 1.0)
    sa_ref[...] = jnp.where(lane < ROPE_HALF, -s, 0.0)
    sb_ref[...] = jnp.where(lane < ROPE_HALF, 0.0, jnp.where(lane < ROPE_DIMS, s, 0.0))


def _rope_tables(pos_col, inv_row, bs):
    t = pos_col.shape[0]
    out = jax.ShapeDtypeStruct((t, LANES), F32)
    spec = pl.BlockSpec((bs, LANES), lambda i: (i, 0))
    return pl.pallas_call(
        _rope_kernel,
        grid=(t // bs,),
        in_specs=[pl.BlockSpec((bs, 1), lambda i: (i, 0)),
                  pl.BlockSpec((1, LANES), lambda i: (0, 0))],
        out_specs=[spec, spec, spec],
        out_shape=[out, out, out],
        compiler_params=_cparams("parallel"),
        name="rope_tables",
    )(pos_col, inv_row)


def _prep_win_kernel(w_ref, wcat_ref, wglr_ref, *, c_glr, c_dil):
    w = w_ref[...]
    rows, ncat = wcat_ref.shape
    wcat_ref[:, :c_glr] = w[:, :c_glr].astype(BF16)
    wcat_ref[:, c_glr:] = w[:, c_dil:c_dil + ncat - c_glr].astype(BF16)
    pad = jnp.zeros((rows, LANES - (c_dil - c_glr)), F32)
    wglr_ref[...] = jnp.concatenate([w[:, c_glr:c_dil], pad], axis=1).astype(BF16)


def _prep_win(wl, c_glr, c_dil, br):
    d, ncols = wl.shape
    ncat = ncols - (c_dil - c_glr)
    return pl.pallas_call(
        functools.partial(_prep_win_kernel, c_glr=c_glr, c_dil=c_dil),
        grid=(d // br,),
        in_specs=[pl.BlockSpec((br, ncols), lambda i: (i, 0))],
        out_specs=[pl.BlockSpec((br, ncat), lambda i: (i, 0)),
                   pl.BlockSpec((br, LANES), lambda i: (i, 0))],
        out_shape=[jax.ShapeDtypeStruct((d, ncat), BF16), jax.ShapeDtypeStruct((d, LANES), BF16)],
        compiler_params=_cparams("parallel"),
        name="in_projection_weights",
    )(wl)


def _rope(t, cos, sa, sb):
    return t * cos + pltpu.roll(t, LANES - ROPE_HALF, 1) * sa + pltpu.roll(t, ROPE_HALF, 1) * sb


def _inproj_kernel(x_ref, w_ref, wg_ref, cos_ref, sa_ref, sb_ref, h_ref, glr_ref, xb_ref):
    bn = INPROJ_BLOCK
    xb_ref[...] = x_ref[...].astype(BF16)
    glr_ref[...] = _dot(xb_ref[...], wg_ref[...])

    def rope(scale):
        def epilogue(acc, c0):
            cos, sa, sb = cos_ref[...], sa_ref[...], sb_ref[...]
            heads = [acc[:, j:j + LANES] for j in range(0, acc.shape[1], LANES)]
            if scale is not None:
                heads = [t * scale for t in heads]
            return jnp.concatenate([_rope(t, cos, sa, sb) for t in heads], axis=1)
        return epilogue

    def plain(acc, c0):
        return acc

    epilogues = (lambda acc, c0: acc * (GLA_DK ** -0.5) if c0 < bn // 2 else acc,
                 plain, plain, rope(DIL_HD ** -0.5), rope(None), plain)
    assert len(epilogues) * bn == w_ref.shape[1]

    for n, epilogue in enumerate(epilogues):
        for c0 in range(0, bn, INPROJ_SUB):
            cols = slice(n * bn + c0, n * bn + c0 + INPROJ_SUB)
            h_ref[:, cols] = epilogue(_dot(xb_ref[...], w_ref[:, cols]), c0).astype(h_ref.dtype)


def _inproj(x2d, wcat, wglr, cosf, sina, sinb, bm):
    t, d = x2d.shape
    ncols = wcat.shape[1]
    tab = pl.BlockSpec((bm, LANES), lambda i: (i, 0))
    return pl.pallas_call(
        _inproj_kernel,
        grid=(t // bm,),
        in_specs=[pl.BlockSpec((bm, d), lambda i: (i, 0)),
                  _resident((d, ncols)), _resident((d, LANES)), tab, tab, tab],
        out_specs=[pl.BlockSpec((bm, ncols), lambda i: (i, 0)),
                   pl.BlockSpec((bm, LANES), lambda i: (i, 0))],
        out_shape=[jax.ShapeDtypeStruct((t, ncols), BF16),
                   jax.ShapeDtypeStruct((t, LANES), F32)],
        scratch_shapes=[pltpu.VMEM((bm, d), BF16)],
        compiler_params=_cparams("parallel"),
        name="in_projection",
    )(x2d, wcat, wglr, cosf, sina, sinb)


def _split3(v):
    hi = v.astype(BF16)
    r1 = v - hi.astype(F32)
    mid = r1.astype(BF16)
    lo = (r1 - mid.astype(F32)).astype(BF16)
    return hi, mid, lo


def _gla_kernel(q_ref, k_ref, v_ref, r_ref, glr_ref, w2_ref, gb_ref, ng_ref, o_ref, st_ref):
    c = GLA_CHUNK
    sb = q_ref.shape[1]
    grp = 4 * c

    @pl.when(pl.program_id(2) == 0)
    def _():
        st_ref[...] = jnp.zeros_like(st_ref)

    z = _dot(glr_ref[0].astype(BF16), w2_ref[...]) + gb_ref[...]
    lg = (jnp.minimum(z, 0.0) - jnp.log1p(jnp.exp(-jnp.abs(z)))) / GLA_TAU

    row = lax.broadcasted_iota(jnp.int32, (2 * grp, grp), 0)
    col = lax.broadcasted_iota(jnp.int32, (2 * grp, grp), 1)
    rr = jnp.where(row < grp, row, row - grp)
    shift = c.bit_length() - 1
    same = (rr >> shift) == (col >> shift)
    lower = jnp.where(same & (col <= rr), 1.0, 0.0)
    upper = jnp.where(same & (col > rr), 1.0, 0.0)
    lu = jnp.where(row < grp, lower, upper).astype(BF16)
    b_parts, e_parts = [], []
    for g0 in range(0, sb, grp):
        pieces = jnp.concatenate(_split3(lg[g0:g0 + grp]), axis=1)
        res = _dot(lu, pieces)
        tot = res[:, :LANES] + res[:, LANES:2 * LANES] + res[:, 2 * LANES:]
        b_parts.append(tot[:grp])
        e_parts.append(tot[grp:])
    b = jnp.concatenate(b_parts, axis=0)
    brest = jnp.concatenate(e_parts, axis=0)

    qf = q_ref[0].astype(F32)
    kf = k_ref[0].astype(F32)
    q_in = (qf * jnp.exp(b)).astype(BF16)
    k_in = (kf * jnp.exp(-b)).astype(BF16)
    k_end = (kf * jnp.exp(brest)).astype(BF16)
    v = v_ref[0]

    ci = lax.broadcasted_iota(jnp.int32, (c, c), 0)
    cj = lax.broadcasted_iota(jnp.int32, (c, c), 1)
    causal = cj <= ci

    st = st_ref[...]
    outs = []
    for i in range(sb // c):
        rows = slice(i * c, (i + 1) * c)
        a = jnp.where(causal, _dot_nt(q_in[rows], k_in[rows]), 0.0).astype(BF16)
        o = _dot(a, v[rows]) + _dot_nt(q_in[rows], st.astype(BF16))
        outs.append(o)
        decay = jnp.exp(b[i * c + c - 1:i * c + c, :])
        st = st * decay + _dot_tn(v[rows], k_end[rows])
    st_ref[...] = st

    o = jnp.concatenate(outs, axis=0)
    mu = jnp.mean(o, axis=-1, keepdims=True)
    d = o - mu
    var = jnp.mean(d * d, axis=-1, keepdims=True)
    rg = r_ref[0].astype(F32)
    y = d * lax.rsqrt(var + LN_EPS) * ng_ref[...] * (rg * _sigmoid(rg))
    o_ref[0] = y.astype(o_ref.dtype)


def _gla(h3, glr3, w2p, gb, ng, sb):
    bsz, s, _ = h3.shape
    kb = GLA_HEADS * GLA_DK // LANES
    vb = 2 * GLA_HEADS * GLA_DK // GLA_DV
    rb = vb + GLA_HEADS
    return pl.pallas_call(
        _gla_kernel,
        grid=(bsz, GLA_HEADS, s // sb),
        in_specs=[pl.BlockSpec((1, sb, GLA_DK), lambda b, h, j: (b, j, h)),
                  pl.BlockSpec((1, sb, GLA_DK), lambda b, h, j: (b, j, kb + h)),
                  pl.BlockSpec((1, sb, GLA_DV), lambda b, h, j: (b, j, vb + h)),
                  pl.BlockSpec((1, sb, GLA_DV), lambda b, h, j: (b, j, rb + h)),
                  pl.BlockSpec((1, sb, LANES), lambda b, h, j: (b, j, 0)),
                  pl.BlockSpec((LANES, GLA_DK), lambda b, h, j: (0, h)),
                  pl.BlockSpec((1, GLA_DK), lambda b, h, j: (0, h)),
                  pl.BlockSpec((1, GLA_DV), lambda b, h, j: (0, h))],
        out_specs=pl.BlockSpec((1, sb, GLA_DV), lambda b, h, j: (b, j, h)),
        out_shape=jax.ShapeDtypeStruct((bsz, s, GLA_HEADS * GLA_DV), BF16),
        scratch_shapes=[pltpu.VMEM((GLA_DV, GLA_DK), F32)],
        compiler_params=_cparams("parallel", "parallel", "arbitrary"),
        name="gla",
    )(h3, h3, h3, h3, glr3, w2p, gb, ng)


def _dil_kernel(q_ref, k_ref, v_ref, o_ref, qf, kf, vf, qg, kg, vg, qc, kc, vc, ob, db, mx,
                scb, eb):
    s = q_ref.shape[1]
    band = DIL_BAND
    unroll = DIL_UNROLL
    nblk = s // band
    qf[...] = q_ref[0].astype(F32)
    kf[...] = k_ref[0].astype(F32)
    vf[...] = v_ref[0].astype(F32)
    kc[:band, :] = jnp.zeros((band, DIL_HD), BF16)
    vc[:band, :] = jnp.zeros((band, DIL_HD), BF16)

    qi = lax.broadcasted_iota(jnp.int32, (band, 2 * band), 0)
    kj = lax.broadcasted_iota(jnp.int32, (band, 2 * band), 1)
    allowed = (kj >= qi) & (kj <= qi + band)
    bias = jnp.where(allowed, 0.0, -jnp.inf).astype(F32)
    bias0 = jnp.where(allowed & (kj >= band), 0.0, -jnp.inf).astype(F32)
    ones = jnp.ones((2 * band, LANES), BF16)

    for p, (window, dil) in enumerate(DIL_PATTERNS):
        assert window // dil == band
        cls = s // dil
        nb = cls // band
        span = band * dil
        assert nblk % unroll == 0 and (nb % unroll == 0 or unroll % nb == 0)

        if dil == 1:
            qc[...] = q_ref[0]
            kc[band:, :] = k_ref[0]
            vc[band:, :] = v_ref[0]
        else:
            prev = DIL_PATTERNS[p - 1][1]
            step = dil // prev
            assert step * prev == dil and step in (2, 4)
            keep = p + 1 < len(DIL_PATTERNS)
            srcs, dsts = ((qf, kf, vf), (qg, kg, vg)) if p % 2 == 1 else ((qg, kg, vg), (qf, kf, vf))
            for r in range(dil):
                rows = pl.ds((r % prev) * (s // prev) + r // prev, cls, stride=step)
                for src, dst, cm, off in zip(srcs, dsts, (qc, kc, vc), (0, band, band)):
                    x = src[rows, :]
                    if keep:
                        dst[r * cls:(r + 1) * cls, :] = x
                    cm[off + r * cls:off + (r + 1) * cls, :] = x.astype(BF16)

        def out_rows(g, lo=0, cnt=band, dil=dil, nb=nb, span=span):
            start = g // nb + (g % nb) * span + lo * dil
            return pl.ds(start, cnt) if dil == 1 else pl.ds(start, cnt, stride=dil)

        def scores(t, slot):
            for u in range(unroll):
                g = t * unroll + u
                k2 = kc[g * band:(g + 2) * band, :]
                scb[slot, u * band:(u + 1) * band, :] = _dot_nt(qc[g * band:(g + 1) * band, :], k2)

        def softmax(t, slot, p=p, nb=nb, out_rows=out_rows):
            for u in range(unroll):
                g = t * unroll + u
                bb = bias0 if g % nb == 0 else bias
                rows = slice(u * band, (u + 1) * band)
                m = jnp.max(scb[slot, rows, :] + bb, axis=-1, keepdims=True)
                mx[p, out_rows(g), :] = jnp.broadcast_to(m, (band, LANES))
                for half in range(2):
                    cols = slice(half * band, (half + 1) * band)
                    eb[slot, rows, cols] = jnp.exp(scb[slot, rows, cols] + bb[:, cols] - m).astype(BF16)

        def values(t, slot, p=p, out_rows=out_rows):
            for u in range(unroll):
                g = t * unroll + u
                v2 = vc[g * band:(g + 2) * band, :]
                oe = _dot(eb[slot, u * band:(u + 1) * band, :],
                          jnp.concatenate([v2, ones], axis=1))
                ob[p, out_rows(g), :] = oe[:, :DIL_HD]
                db[p, out_rows(g), :] = oe[:, DIL_HD:]

        ngrp = nblk // unroll
        for t in range(ngrp + 2):
            if t < ngrp:
                scores(t, t % 2)
            if 1 <= t <= ngrp:
                softmax(t - 1, (t - 1) % 2)
            if t >= 2:
                values(t - 2, t % 2)

    mb = 512

    def merge(i, carry):
        rs = pl.ds(pl.multiple_of(i * mb, mb), mb)
        m0, m1, m2 = mx[0, rs, :], mx[1, rs, :], mx[2, rs, :]
        m = jnp.maximum(jnp.maximum(m0, m1), m2)
        e0, e1, e2 = jnp.exp(m0 - m), jnp.exp(m1 - m), jnp.exp(m2 - m)
        num = e0 * ob[0, rs, :] + e1 * ob[1, rs, :] + e2 * ob[2, rs, :]
        den = e0 * db[0, rs, :] + e1 * db[1, rs, :] + e2 * db[2, rs, :]
        o_ref[0, rs, :] = (num / den).astype(o_ref.dtype)
        return carry

    lax.fori_loop(0, s // mb, merge, 0)


def _dil(h3):
    bsz, s, _ = h3.shape
    qb = (2 * GLA_HEADS * GLA_DK + 2 * GLA_HEADS * GLA_DV) // DIL_HD
    kb = qb + DIL_HEADS
    vb = kb + DIL_HEADS
    npat = len(DIL_PATTERNS)
    return pl.pallas_call(
        _dil_kernel,
        grid=(bsz, DIL_HEADS),
        in_specs=[pl.BlockSpec((1, s, DIL_HD), lambda b, h: (b, 0, qb + h)),
                  pl.BlockSpec((1, s, DIL_HD), lambda b, h: (b, 0, kb + h)),
                  pl.BlockSpec((1, s, DIL_HD), lambda b, h: (b, 0, vb + h))],
        out_specs=pl.BlockSpec((1, s, DIL_HD), lambda b, h: (b, 0, h)),
        out_shape=jax.ShapeDtypeStruct((bsz, s, DIL_HEADS * DIL_HD), BF16),
        scratch_shapes=[pltpu.VMEM((s, DIL_HD), F32)] * 6 + [
                        pltpu.VMEM((s, DIL_HD), BF16), pltpu.VMEM((s + DIL_BAND, DIL_HD), BF16),
                        pltpu.VMEM((s + DIL_BAND, DIL_HD), BF16),
                        pltpu.VMEM((npat, s, DIL_HD), F32), pltpu.VMEM((npat, s, LANES), F32),
                        pltpu.VMEM((npat, s, LANES), F32),
                        pltpu.VMEM((2, DIL_UNROLL * DIL_BAND, 2 * DIL_BAND), F32),
                        pltpu.VMEM((2, DIL_UNROLL * DIL_BAND, 2 * DIL_BAND), BF16)],
        compiler_params=_cparams("parallel", "parallel"),
        name="dilated_attention",
    )(h3, h3, h3)


def _outproj_kernel(og_ref, od_ref, wa_ref, wb_ref, x_ref, g_ref, b_ref, o_ref, *, alpha):
    acc = _dot(og_ref[...], wa_ref[...]) + _dot(od_ref[...], wb_ref[...])
    o_ref[...] = _layer_norm(alpha * x_ref[...] + acc, g_ref[...], b_ref[...])


def _resident(shape):
    return pl.BlockSpec(shape, lambda *_: (0,) * len(shape), pipeline_mode=pl.Buffered(1))


def _outproj(og, od, wa, wb, x2d, g, b, bm, alpha):
    t, d = x2d.shape
    ka, kb = og.shape[1], od.shape[1]
    return pl.pallas_call(
        functools.partial(_outproj_kernel, alpha=alpha),
        grid=(t // bm,),
        in_specs=[pl.BlockSpec((bm, ka), lambda i: (i, 0)),
                  pl.BlockSpec((bm, kb), lambda i: (i, 0)),
                  _resident((ka, d)), _resident((kb, d)),
                  pl.BlockSpec((bm, d), lambda i: (i, 0)),
                  _resident((1, d)), _resident((1, d))],
        out_specs=pl.BlockSpec((bm, d), lambda i: (i, 0)),
        out_shape=jax.ShapeDtypeStruct((t, d), F32),
        compiler_params=_cparams("parallel"),
        name="out_projection_ln1",
    )(og, od, wa, wb, x2d, g, b)


def _memkv_kernel(m_ref, w_ref, o_ref):
    o_ref[...] = _dot(m_ref[...].astype(BF16), w_ref[...]).astype(o_ref.dtype)


def _memkv(mem2d, wkv, bn):
    t, d = mem2d.shape
    n = wkv.shape[1]
    return pl.pallas_call(
        _memkv_kernel,
        grid=(n // bn,),
        in_specs=[pl.BlockSpec((t, d), lambda j: (0, 0)),
                  pl.BlockSpec((d, bn), lambda j: (0, j))],
        out_specs=pl.BlockSpec((t, bn), lambda j: (0, j)),
        out_shape=jax.ShapeDtypeStruct((t, n), BF16),
        compiler_params=_cparams("parallel"),
        name="memory_kv_projection",
    )(mem2d, wkv)


def _cross_kernel(x_ref, wq_ref, kv_ref, wo_ref, g_ref, b_ref, o_ref, oc_ref, *, alpha):
    d = x_ref.shape[1]
    hd = d // CA_HEADS
    x = x_ref[...]
    q = _dot(x.astype(BF16), wq_ref[...]).astype(BF16)
    for h in range(CA_HEADS):
        cols = slice(h * hd, (h + 1) * hd)
        mk = kv_ref[0, :, cols]
        mv = kv_ref[0, :, d + h * hd:d + (h + 1) * hd]
        sc = _dot_nt(q[:, cols], mk) * (hd ** -0.5)
        m = jnp.max(sc, axis=-1, keepdims=True)
        e = jnp.exp(sc - m)
        p = e / jnp.sum(e, axis=-1, keepdims=True)
        oc_ref[:, cols] = _dot(p.astype(BF16), mv).astype(BF16)
    y = alpha * x + _dot(oc_ref[...], wo_ref[...])
    o_ref[...] = _layer_norm(y, g_ref[...], b_ref[...])


def _cross(x1, wq, kv3, wo, g, b, bm, seq, alpha):
    t, d = x1.shape
    m = kv3.shape[1]
    per = seq // bm
    return pl.pallas_call(
        functools.partial(_cross_kernel, alpha=alpha),
        grid=(t // bm,),
        in_specs=[pl.BlockSpec((bm, d), lambda i: (i, 0)),
                  _resident((d, d)),
                  pl.BlockSpec((1, m, 2 * d), lambda i: (i // per, 0, 0)),
                  _resident((d, d)),
                  _resident((1, d)), _resident((1, d))],
        out_specs=pl.BlockSpec((bm, d), lambda i: (i, 0)),
        out_shape=jax.ShapeDtypeStruct((t, d), F32),
        scratch_shapes=[pltpu.VMEM((bm, d), BF16)],
        compiler_params=_cparams("parallel"),
        name="cross_attention_ln2",
    )(x1, wq, kv3, wo, g, b)


def _causal_conv(u_ref, cw, cb, r0, n):
    h = FFN_TAIL + r0
    y = cb + cw[0:1, :] * u_ref[h - 2:h - 2 + n, :]
    y = y + cw[1:2, :] * u_ref[h - 1:h - 1 + n, :]
    return y + cw[2:3, :] * u_ref[h:h + n, :]


def _ffn_kernel(cp_ref, g_ref, b_ref, x_hbm, w1_hbm, w2_hbm, o_ref,
                x_buf, xb_ref, act0_ref, act1_ref, us_ref, tail_ref, wg_buf, wu_buf, w2_buf,
                sem, x_sem, *, alpha, per):
    i = pl.program_id(0)
    d_ff = w2_hbm.shape[0]
    bm = x_buf.shape[0]
    bf = wg_buf.shape[2]
    nf = pl.cdiv(d_ff, bf)
    last_w = d_ff - (nf - 1) * bf
    assert nf % 2 == 1 and nf >= 3
    assert bf % FFN_SUB == 0 and last_w % LANES == 0 and bm % FFN_ROWS == 0
    c_first = i * nf
    acts = (act0_ref, act1_ref)
    nslab = bf // LANES
    half = o_ref.shape[1] // 2

    def width(f):
        return last_w if isinstance(f, int) and f == nf - 1 else bf

    def up_copies(f, slot):
        w = width(f)
        col = f * bf if isinstance(f, int) else pl.multiple_of(f * bf, bf)
        return (pltpu.make_async_copy(w1_hbm.at[:, pl.ds(col, w)],
                                      wg_buf.at[slot, :, pl.ds(0, w)], sem.at[0, slot]),
                pltpu.make_async_copy(w1_hbm.at[:, pl.ds(d_ff + col, w)],
                                      wu_buf.at[slot, :, pl.ds(0, w)], sem.at[1, slot]))

    def down_copy(f, slot):
        w = width(f)
        row = f * bf if isinstance(f, int) else pl.multiple_of(f * bf, bf)
        return pltpu.make_async_copy(w2_hbm.at[pl.ds(row, w), :],
                                     w2_buf.at[slot, pl.ds(0, w), :], sem.at[2, slot])

    def x_copy(step):
        row = pl.multiple_of(step * bm, bm)
        return pltpu.make_async_copy(x_hbm.at[pl.ds(row, bm), :], x_buf, x_sem.at[0])

    def region_copies(f):
        slot = lax.rem(c_first + f, 2)
        for cp in up_copies(f, slot):
            cp.wait()
        if isinstance(f, int) and f == 0:
            @pl.when(i + 1 < pl.num_programs(0))
            def _():
                x_copy(i + 1).start()
        if not (isinstance(f, int) and f == 0):
            down_copy(f - 1, 1 - slot).wait()
        if isinstance(f, int) and f == nf - 1:
            @pl.when(i + 1 < pl.num_programs(0))
            def _():
                for cp in up_copies(0, 1 - slot):
                    cp.start()
        else:
            for cp in up_copies(f + 1, 1 - slot):
                cp.start()
        down_copy(f, slot).start()
        return slot

    def up_matmul(slot, c0, wcols, r0):
        xr = xb_ref[r0:r0 + FFN_ROWS, :]
        rows = slice(FFN_TAIL + r0, FFN_TAIL + r0 + FFN_ROWS)
        for base, w_buf in ((0, wg_buf), (nslab, wu_buf)):
            res = _dot(xr, w_buf[slot, :, c0:c0 + wcols])
            for k in range(wcols // LANES):
                us_ref[base + c0 // LANES + k, rows, :] = res[:, k * LANES:(k + 1) * LANES]

    def activate(cv, act_ref, c0, wcols, r0):
        for s in range(c0 // LANES, (c0 + wcols) // LANES):
            cols = slice(s * LANES, (s + 1) * LANES)
            gate = _causal_conv(us_ref.at[s], cv[0:3, cols], cv[3:4, cols], r0, FFN_ROWS)
            up = _causal_conv(us_ref.at[nslab + s], cv[4:7, cols], cv[7:8, cols], r0, FFN_ROWS)
            act_ref[r0:r0 + FFN_ROWS, cols] = (gate * _sigmoid(gate) * up).astype(BF16)

    def down_matmul(slot, act_ref, w, n0):
        o_ref[:, n0:n0 + half] += _dot(act_ref[:, :w], w2_buf[slot, :w, n0:n0 + half])

    def region(f, act_slot):
        slot = region_copies(f)
        w = width(f)
        cv = cp_ref[f]
        act_ref, prev_ref = acts[act_slot], acts[1 - act_slot]
        slabs = [s for base in (0, nslab) for s in range(base, base + w // LANES)]
        for s in slabs:
            us_ref[s, :FFN_TAIL, :] = tail_ref[f, :, s * LANES:(s + 1) * LANES]
        units = [(c0, min(FFN_SUB, w - c0), r0) for c0 in range(0, w, FFN_SUB)
                 for r0 in range(0, bm, FFN_ROWS)]
        for unit in units:
            up_matmul(slot, *unit)
        if not (isinstance(f, int) and f == 0):
            for n0 in (0, half):
                down_matmul(1 - slot, prev_ref, width(f - 1) if isinstance(f, int) else bf, n0)
        for unit in units:
            activate(cv, act_ref, *unit)
        for s in slabs:
            tail_ref[f, :, s * LANES:(s + 1) * LANES] = us_ref[s, bm:bm + FFN_TAIL, :]

    @pl.when(i == 0)
    def _():
        x_copy(0).start()
        for cp in up_copies(0, 0):
            cp.start()

    @pl.when(i % per == 0)
    def _():
        tail_ref[...] = jnp.zeros_like(tail_ref)

    x_copy(i).wait()
    xb_ref[...] = x_buf[...].astype(BF16)
    o_ref[...] = alpha * x_buf[...]

    region(0, 0)

    def pair(j, carry):
        region(2 * j + 1, 1)
        region(2 * j + 2, 0)
        return carry

    lax.fori_loop(0, (nf - 3) // 2, pair, 0)
    region(nf - 2, 1)
    region(nf - 1, 0)

    last = lax.rem(c_first + nf - 1, 2)
    down_copy(nf - 1, last).wait()
    for n0 in (0, half):
        down_matmul(last, acts[0], last_w, n0)
    o_ref[...] = _layer_norm(o_ref[...], g_ref[...], b_ref[...])


def _ffn(x2, w1, w2, cp, g, b, bm, seq, alpha):
    t, d = x2.shape
    bf = FFN_CHUNK
    nf = cp.shape[0]
    hbm = pl.BlockSpec(memory_space=pl.ANY)
    return pl.pallas_call(
        functools.partial(_ffn_kernel, alpha=alpha, per=seq // bm),
        grid=(t // bm,),
        in_specs=[_resident(cp.shape), _resident((1, d)), _resident((1, d)), hbm, hbm, hbm],
        out_specs=pl.BlockSpec((bm, d), lambda i: (i, 0)),
        out_shape=jax.ShapeDtypeStruct((t, d), F32),
        scratch_shapes=[pltpu.VMEM((bm, d), F32), pltpu.VMEM((bm, d), BF16),
                        pltpu.VMEM((bm, bf), BF16), pltpu.VMEM((bm, bf), BF16),
                        pltpu.VMEM((2 * bf // LANES, FFN_TAIL + bm, LANES), F32),
                        pltpu.VMEM((nf, FFN_TAIL, 2 * bf), F32),
                        pltpu.VMEM((2, d, bf), BF16), pltpu.VMEM((2, d, bf), BF16),
                        pltpu.VMEM((2, bf, d), BF16), pltpu.SemaphoreType.DMA((3, 2)),
                        pltpu.SemaphoreType.DMA((1,))],
        compiler_params=_cparams("arbitrary"),
        name="conv_ffn_ln3",
    )(cp, g, b, x2, w1, w2)


def _pad_cols(a, n):
    return jnp.pad(a, ((0, 0), (0, n - a.shape[1])))


def kernel(x, mem, positions, w_in, gla_gate_w2, gla_gate_b, gla_norm_g, w_out, ln1_g, ln1_b,
           ca_wq, ca_wkv, ca_wo, ln2_g, ln2_b, ffn_w_in, ffn_conv_w, ffn_conv_b, ffn_w_out,
           ln3_g, ln3_b):
    bsz, seq, d = x.shape
    depth = w_in.shape[0]
    t = bsz * seq
    alpha = (2.0 * depth) ** 0.25
    d_ff = ffn_w_out.shape[1]
    dff_pad = -(-d_ff // FFN_CHUNK) * FFN_CHUNK

    nqk = GLA_HEADS * GLA_DK
    nv = GLA_HEADS * GLA_DV
    c_glr = 2 * nqk + 2 * nv
    c_dil = c_glr + GLA_GATE_RANK

    half = ROPE_HALF
    inv_freq = ROPE_THETA ** (-jnp.arange(0, ROPE_DIMS, 2, dtype=F32) / ROPE_DIMS)
    inv_row = jnp.concatenate([inv_freq, inv_freq, jnp.zeros((LANES - 2 * half,), F32)])[None, :]
    pos_col = positions.astype(F32).reshape(t, 1)
    cosf, sina, sinb = _rope_tables(pos_col, inv_row, min(t, 2048))

    x2d = x.reshape(t, d)
    for l in range(depth):
        wcat, wglr = _prep_win(w_in[l], c_glr, c_dil, 256)
        h, glr = _inproj(x2d, wcat, wglr, cosf, sina, sinb, min(t, 512))
        h3 = h.reshape(bsz, seq, h.shape[1])

        w2p = jnp.pad(gla_gate_w2[l], ((0, LANES - GLA_GATE_RANK), (0, 0))).astype(BF16)
        og = _gla(h3, glr.reshape(bsz, seq, LANES), w2p, gla_gate_b[l][None, :],
                  gla_norm_g[l][None, :], min(seq, 1024))
        od = _dil(h3)

        wo = w_out[l].astype(BF16)
        x1 = _outproj(og.reshape(t, nv), od.reshape(t, DIL_HEADS * DIL_HD), wo[:nv], wo[nv:],
                      x2d, ln1_g[l][None, :], ln1_b[l][None, :], min(t, 512), alpha)

        kv = _memkv(mem.reshape(-1, d), ca_wkv[l].astype(BF16), 1024)
        x2 = _cross(x1, ca_wq[l].astype(BF16), kv.reshape(bsz, -1, 2 * d), ca_wo[l].astype(BF16),
                    ln2_g[l][None, :], ln2_b[l][None, :], min(seq, 512), seq, alpha)

        cw = ffn_conv_w[l]
        cb = ffn_conv_b[l][None, :]
        nf = dff_pad // FFN_CHUNK
        conv = jnp.concatenate([cw[:, :d_ff], cb[:, :d_ff], cw[:, d_ff:], cb[:, d_ff:]], axis=0)
        conv = _pad_cols(conv, dff_pad).reshape(conv.shape[0], nf, FFN_CHUNK).transpose(1, 0, 2)
        x2d = _ffn(x2, ffn_w_in[l].astype(BF16), ffn_w_out[l].astype(BF16), conv,
                   ln3_g[l][None, :], ln3_b[l][None, :], min(seq, 1024), seq, alpha)
    return x2d.reshape(bsz, seq, d)
```

```python
import functools

import jax
import jax.numpy as jnp
from jax import lax
from jax.experimental import pallas as pl
from jax.experimental.pallas import tpu as pltpu

F32 = jnp.float32
BF16 = jnp.bfloat16

LANES = 128
LN_EPS = 1e-5
GLA_HEADS = 4
GLA_DK = 128
GLA_DV = 256
GLA_GATE_RANK = 16
GLA_TAU = 16.0
GLA_CHUNK = 64
DIL_HD = 128
DIL_HEADS = 8
DIL_PATTERNS = ((128, 1), (512, 4), (2048, 16))
DIL_BAND = 128
DIL_UNROLL = 4
ROPE_THETA = 500000.0
ROPE_DIMS = 32
ROPE_HALF = ROPE_DIMS // 2
CA_HEADS = 4
CONV_W = 3
INPROJ_BLOCK = 1024
INPROJ_SUB = 256
WEIGHT_ROWS = 256
FFN_CHUNK = 512
FFN_SUB = 256
FFN_ROWS = 512
FFN_TAIL = 8
VMEM_LIMIT = 56 * 1024 * 1024


def _cparams(*sem):
    return pltpu.CompilerParams(dimension_semantics=sem, vmem_limit_bytes=VMEM_LIMIT)


def _dot(a, b):
    return jnp.dot(a, b, preferred_element_type=F32)


def _dot_nt(a, b):
    return lax.dot_general(a, b, (((1,), (1,)), ((), ())), preferred_element_type=F32)


def _dot_tn(a, b):
    return lax.dot_general(a, b, (((0,), (0,)), ((), ())), preferred_element_type=F32)


def _layer_norm(y, g, b):
    mu = jnp.mean(y, axis=-1, keepdims=True)
    d = y - mu
    var = jnp.mean(d * d, axis=-1, keepdims=True)
    return d * lax.rsqrt(var + LN_EPS) * g + b


def _sigmoid(x):
    return 1.0 / (1.0 + jnp.exp(-x))


def _rope_kernel(pos_ref, inv_ref, cos_ref, sa_ref, sb_ref):
    ang = pos_ref[...] * inv_ref[...]
    lane = lax.broadcasted_iota(jnp.int32, ang.shape, 1)
    c = jnp.cos(ang)
    s = jnp.sin(ang)
    cos_ref[...] = jnp.where(lane < ROPE_DIMS, c, 1.0)
    sa_ref[...] = jnp.where(lane < ROPE_HALF, -s, 0.0)
    sb_ref[...] = jnp.where(lane < ROPE_HALF, 0.0, jnp.where(lane < ROPE_DIMS, s, 0.0))


def _rope_tables(pos_col, inv_row, bs):
    t = pos_col.shape[0]
    out = jax.ShapeDtypeStruct((t, LANES), F32)
    spec = pl.BlockSpec((bs, LANES), lambda i: (i, 0))
    return pl.pallas_call(
        _rope_kernel,
        grid=(t // bs,),
        in_specs=[pl.BlockSpec((bs, 1), lambda i: (i, 0)),
                  pl.BlockSpec((1, LANES), lambda i: (0, 0))],
        out_specs=[spec, spec, spec],
        out_shape=[out, out, out],
        compiler_params=_cparams("parallel"),
        name="rope_tables",
    )(pos_col, inv_row)


def _prep_win_kernel(wt_hbm, wcat_ref, wglr_ref, buf, gbuf, sem, *, c_glr, c_dil):
    j = pl.program_id(0)
    bn = buf.shape[1]
    rank = c_dil - c_glr

    def copy(jj, slot):
        row = jj * bn
        row = pl.multiple_of(row + jnp.where(row >= c_glr, rank, 0), 8)
        return pltpu.make_async_copy(wt_hbm.at[pl.ds(row, bn), :], buf.at[slot], sem.at[slot])

    gate_copy = pltpu.make_async_copy(wt_hbm.at[pl.ds(c_glr, rank), :], gbuf.at[pl.ds(0, rank), :],
                                      sem.at[2])

    @pl.when(j == 0)
    def _():
        copy(0, 0).start()
        gate_copy.start()
        gbuf[rank:, :] = jnp.zeros((gbuf.shape[0] - rank, gbuf.shape[1]), F32)

    slot = lax.rem(j, 2)

    @pl.when(j + 1 < pl.num_programs(0))
    def _():
        copy(j + 1, 1 - slot).start()

    copy(j, slot).wait()
    wcat_ref[...] = buf[slot].T.astype(BF16)

    @pl.when(j == 0)
    def _():
        gate_copy.wait()
        wglr_ref[...] = gbuf[...].T.astype(BF16)


def _prep_win(wt, c_glr, c_dil, bn):
    ncols, d = wt.shape
    ncat = ncols - (c_dil - c_glr)
    assert c_glr % bn == 0 and ncat % bn == 0 and (c_dil - c_glr) % 8 == 0
    return pl.pallas_call(
        functools.partial(_prep_win_kernel, c_glr=c_glr, c_dil=c_dil),
        grid=(ncat // bn,),
        in_specs=[pl.BlockSpec(memory_space=pl.ANY)],
        out_specs=[pl.BlockSpec((d, bn), lambda j: (0, j)),
                   pl.BlockSpec((d, LANES), lambda j: (0, 0))],
        out_shape=[jax.ShapeDtypeStruct((d, ncat), BF16), jax.ShapeDtypeStruct((d, LANES), BF16)],
        scratch_shapes=[pltpu.VMEM((2, bn, d), F32), pltpu.VMEM((LANES, d), F32),
                        pltpu.SemaphoreType.DMA((3,))],
        compiler_params=_cparams("arbitrary"),
        name="in_projection_weights",
    )(wt)


def _rope(t, cos, sa, sb):
    return t * cos + pltpu.roll(t, LANES - ROPE_HALF, 1) * sa + pltpu.roll(t, ROPE_HALF, 1) * sb


def _inproj_kernel(x_ref, w_ref, wg_ref, cos_ref, sa_ref, sb_ref, h_ref, glr_ref, xb_ref):
    bn = INPROJ_BLOCK
    xb_ref[...] = x_ref[...].astype(BF16)
    glr_ref[...] = _dot(xb_ref[...], wg_ref[...])

    def rope(scale):
        def epilogue(acc, c0):
            cos, sa, sb = cos_ref[...], sa_ref[...], sb_ref[...]
            heads = [acc[:, j:j + LANES] for j in range(0, acc.shape[1], LANES)]
            if scale is not None:
                heads = [t * scale for t in heads]
            return jnp.concatenate([_rope(t, cos, sa, sb) for t in heads], axis=1)
        return epilogue

    def plain(acc, c0):
        return acc

    epilogues = (lambda acc, c0: acc * (GLA_DK ** -0.5) if c0 < bn // 2 else acc,
                 plain, plain, rope(DIL_HD ** -0.5), rope(None), plain)
    assert len(epilogues) * bn == w_ref.shape[1]

    for n, epilogue in enumerate(epilogues):
        for c0 in range(0, bn, INPROJ_SUB):
            cols = slice(n * bn + c0, n * bn + c0 + INPROJ_SUB)
            h_ref[:, cols] = epilogue(_dot(xb_ref[...], w_ref[:, cols]), c0).astype(h_ref.dtype)


def _inproj(x2d, wcat, wglr, cosf, sina, sinb, bm):
    t, d = x2d.shape
    ncols = wcat.shape[1]
    tab = pl.BlockSpec((bm, LANES), lambda i: (i, 0))
    return pl.pallas_call(
        _inproj_kernel,
        grid=(t // bm,),
        in_specs=[pl.BlockSpec((bm, d), lambda i: (i, 0)),
                  _resident((d, ncols)), _resident((d, LANES)), tab, tab, tab],
        out_specs=[pl.BlockSpec((bm, ncols), lambda i: (i, 0)),
                   pl.BlockSpec((bm, LANES), lambda i: (i, 0))],
        out_shape=[jax.ShapeDtypeStruct((t, ncols), BF16),
                   jax.ShapeDtypeStruct((t, LANES), F32)],
        scratch_shapes=[pltpu.VMEM((bm, d), BF16)],
        compiler_params=_cparams("parallel"),
        name="in_projection",
    )(x2d, wcat, wglr, cosf, sina, sinb)


def _split3(v):
    hi = v.astype(BF16)
    r1 = v - hi.astype(F32)
    mid = r1.astype(BF16)
    lo = (r1 - mid.astype(F32)).astype(BF16)
    return hi, mid, lo


def _gla_kernel(q_ref, k_ref, v_ref, r_ref, glr_ref, w2_ref, gb_ref, ng_ref, o_ref, st_ref):
    c = GLA_CHUNK
    sb = q_ref.shape[1]
    grp = 4 * c

    @pl.when(pl.program_id(2) == 0)
    def _():
        st_ref[...] = jnp.zeros_like(st_ref)

    z = _dot(glr_ref[0].astype(BF16), w2_ref[...]) + gb_ref[...]
    lg = (jnp.minimum(z, 0.0) - jnp.log1p(jnp.exp(-jnp.abs(z)))) / GLA_TAU

    row = lax.broadcasted_iota(jnp.int32, (2 * grp, grp), 0)
    col = lax.broadcasted_iota(jnp.int32, (2 * grp, grp), 1)
    rr = jnp.where(row < grp, row, row - grp)
    shift = c.bit_length() - 1
    same = (rr >> shift) == (col >> shift)
    lower = jnp.where(same & (col <= rr), 1.0, 0.0)
    upper = jnp.where(same & (col > rr), 1.0, 0.0)
    lu = jnp.where(row < grp, lower, upper).astype(BF16)
    b_parts, e_parts = [], []
    for g0 in range(0, sb, grp):
        pieces = jnp.concatenate(_split3(lg[g0:g0 + grp]), axis=1)
        res = _dot(lu, pieces)
        tot = res[:, :LANES] + res[:, LANES:2 * LANES] + res[:, 2 * LANES:]
        b_parts.append(tot[:grp])
        e_parts.append(tot[grp:])
    b = jnp.concatenate(b_parts, axis=0)
    brest = jnp.concatenate(e_parts, axis=0)

    qf = q_ref[0].astype(F32)
    kf = k_ref[0].astype(F32)
    q_in = (qf * jnp.exp(b)).astype(BF16)
    k_in = (kf * jnp.exp(-b)).astype(BF16)
    k_end = (kf * jnp.exp(brest)).astype(BF16)
    v = v_ref[0]

    ci = lax.broadcasted_iota(jnp.int32, (c, c), 0)
    cj = lax.broadcasted_iota(jnp.int32, (c, c), 1)
    causal = cj <= ci

    st = st_ref[...]
    outs = []
    for i in range(sb // c):
        rows = slice(i * c, (i + 1) * c)
        a = jnp.where(causal, _dot_nt(q_in[rows], k_in[rows]), 0.0).astype(BF16)
        o = _dot(a, v[rows]) + _dot_nt(q_in[rows], st.astype(BF16))
        outs.append(o)
        decay = jnp.exp(b[i * c + c - 1:i * c + c, :])
        st = st * decay + _dot_tn(v[rows], k_end[rows])
    st_ref[...] = st

    o = jnp.concatenate(outs, axis=0)
    mu = jnp.mean(o, axis=-1, keepdims=True)
    d = o - mu
    var = jnp.mean(d * d, axis=-1, keepdims=True)
    rg = r_ref[0].astype(F32)
    y = d * lax.rsqrt(var + LN_EPS) * ng_ref[...] * (rg * _sigmoid(rg))
    o_ref[0] = y.astype(o_ref.dtype)


def _gla(h3, glr3, w2p, gb, ng, sb):
    bsz, s, _ = h3.shape
    kb = GLA_HEADS * GLA_DK // LANES
    vb = 2 * GLA_HEADS * GLA_DK // GLA_DV
    rb = vb + GLA_HEADS
    return pl.pallas_call(
        _gla_kernel,
        grid=(bsz, GLA_HEADS, s // sb),
        in_specs=[pl.BlockSpec((1, sb, GLA_DK), lambda b, h, j: (b, j, h)),
                  pl.BlockSpec((1, sb, GLA_DK), lambda b, h, j: (b, j, kb + h)),
                  pl.BlockSpec((1, sb, GLA_DV), lambda b, h, j: (b, j, vb + h)),
                  pl.BlockSpec((1, sb, GLA_DV), lambda b, h, j: (b, j, rb + h)),
                  pl.BlockSpec((1, sb, LANES), lambda b, h, j: (b, j, 0)),
                  pl.BlockSpec((LANES, GLA_DK), lambda b, h, j: (0, h)),
                  pl.BlockSpec((1, GLA_DK), lambda b, h, j: (0, h)),
                  pl.BlockSpec((1, GLA_DV), lambda b, h, j: (0, h))],
        out_specs=pl.BlockSpec((1, sb, GLA_DV), lambda b, h, j: (b, j, h)),
        out_shape=jax.ShapeDtypeStruct((bsz, s, GLA_HEADS * GLA_DV), BF16),
        scratch_shapes=[pltpu.VMEM((GLA_DV, GLA_DK), F32)],
        compiler_params=_cparams("parallel", "parallel", "arbitrary"),
        name="gla",
    )(h3, h3, h3, h3, glr3, w2p, gb, ng)


def _dil_kernel(q_ref, k_ref, v_ref, o_ref, qf, kf, vf, qg, kg, vg, qc, kc, vc, ob, db, mx,
                scb, eb):
    s = q_ref.shape[1]
    band = DIL_BAND
    unroll = DIL_UNROLL
    nblk = s // band
    qf[...] = q_ref[0].astype(F32)
    kf[...] = k_ref[0].astype(F32)
    vf[...] = v_ref[0].astype(F32)
    kc[:band, :] = jnp.zeros((band, DIL_HD), BF16)
    vc[:band, :] = jnp.zeros((band, DIL_HD), BF16)

    qi = lax.broadcasted_iota(jnp.int32, (band, 2 * band), 0)
    kj = lax.broadcasted_iota(jnp.int32, (band, 2 * band), 1)
    allowed = (kj >= qi) & (kj <= qi + band)
    bias = jnp.where(allowed, 0.0, -jnp.inf).astype(F32)
    bias0 = jnp.where(allowed & (kj >= band), 0.0, -jnp.inf).astype(F32)
    ones = jnp.ones((2 * band, LANES), BF16)

    for p, (window, dil) in enumerate(DIL_PATTERNS):
        assert window // dil == band
        cls = s // dil
        nb = cls // band
        span = band * dil
        assert nblk % unroll == 0 and (nb % unroll == 0 or unroll % nb == 0)

        if dil == 1:
            qc[...] = q_ref[0]
            kc[band:, :] = k_ref[0]
            vc[band:, :] = v_ref[0]
        else:
            prev = DIL_PATTERNS[p - 1][1]
            step = dil // prev
            assert step * prev == dil and step in (2, 4)
            keep = p + 1 < len(DIL_PATTERNS)
            srcs, dsts = ((qf, kf, vf), (qg, kg, vg)) if p % 2 == 1 else ((qg, kg, vg), (qf, kf, vf))
            for r in range(dil):
                rows = pl.ds((r % prev) * (s // prev) + r // prev, cls, stride=step)
                for src, dst, cm, off in zip(srcs, dsts, (qc, kc, vc), (0, band, band)):
                    x = src[rows, :]
                    if keep:
                        dst[r * cls:(r + 1) * cls, :] = x
                    cm[off + r * cls:off + (r + 1) * cls, :] = x.astype(BF16)

        def out_rows(g, lo=0, cnt=band, dil=dil, nb=nb, span=span):
            start = g // nb + (g % nb) * span + lo * dil
            return pl.ds(start, cnt) if dil == 1 else pl.ds(start, cnt, stride=dil)

        def scores(t, slot):
            for u in range(unroll):
                g = t * unroll + u
                k2 = kc[g * band:(g + 2) * band, :]
                scb[slot, u * band:(u + 1) * band, :] = _dot_nt(qc[g * band:(g + 1) * band, :], k2)

        def softmax(t, slot, p=p, nb=nb, out_rows=out_rows):
            for u in range(unroll):
                g = t * unroll + u
                bb = bias0 if g % nb == 0 else bias
                rows = slice(u * band, (u + 1) * band)
                m = jnp.max(scb[slot, rows, :] + bb, axis=-1, keepdims=True)
                mx[p, out_rows(g), :] = jnp.broadcast_to(m, (band, LANES))
                for half in range(2):
                    cols = slice(half * band, (half + 1) * band)
                    eb[slot, rows, cols] = jnp.exp(scb[slot, rows, cols] + bb[:, cols] - m).astype(BF16)

        def values(t, slot, p=p, out_rows=out_rows):
            for u in range(unroll):
                g = t * unroll + u
                v2 = vc[g * band:(g + 2) * band, :]
                oe = _dot(eb[slot, u * band:(u + 1) * band, :],
                          jnp.concatenate([v2, ones], axis=1))
                ob[p, out_rows(g), :] = oe[:, :DIL_HD]
                db[p, out_rows(g), :] = oe[:, DIL_HD:]

        ngrp = nblk // unroll
        for t in range(ngrp + 2):
            if t < ngrp:
                scores(t, t % 2)
            if 1 <= t <= ngrp:
                softmax(t - 1, (t - 1) % 2)
            if t >= 2:
                values(t - 2, t % 2)

    mb = 512

    def merge(i, carry):
        rs = pl.ds(pl.multiple_of(i * mb, mb), mb)
        m0, m1, m2 = mx[0, rs, :], mx[1, rs, :], mx[2, rs, :]
        m = jnp.maximum(jnp.maximum(m0, m1), m2)
        e0, e1, e2 = jnp.exp(m0 - m), jnp.exp(m1 - m), jnp.exp(m2 - m)
        num = e0 * ob[0, rs, :] + e1 * ob[1, rs, :] + e2 * ob[2, rs, :]
        den = e0 * db[0, rs, :] + e1 * db[1, rs, :] + e2 * db[2, rs, :]
        o_ref[0, rs, :] = (num / den).astype(o_ref.dtype)
        return carry

    lax.fori_loop(0, s // mb, merge, 0)


def _dil(h3):
    bsz, s, _ = h3.shape
    qb = (2 * GLA_HEADS * GLA_DK + 2 * GLA_HEADS * GLA_DV) // DIL_HD
    kb = qb + DIL_HEADS
    vb = kb + DIL_HEADS
    npat = len(DIL_PATTERNS)
    return pl.pallas_call(
        _dil_kernel,
        grid=(bsz, DIL_HEADS),
        in_specs=[pl.BlockSpec((1, s, DIL_HD), lambda b, h: (b, 0, qb + h)),
                  pl.BlockSpec((1, s, DIL_HD), lambda b, h: (b, 0, kb + h)),
                  pl.BlockSpec((1, s, DIL_HD), lambda b, h: (b, 0, vb + h))],
        out_specs=pl.BlockSpec((1, s, DIL_HD), lambda b, h: (b, 0, h)),
        out_shape=jax.ShapeDtypeStruct((bsz, s, DIL_HEADS * DIL_HD), BF16),
        scratch_shapes=[pltpu.VMEM((s, DIL_HD), F32)] * 6 + [
                        pltpu.VMEM((s, DIL_HD), BF16), pltpu.VMEM((s + DIL_BAND, DIL_HD), BF16),
                        pltpu.VMEM((s + DIL_BAND, DIL_HD), BF16),
                        pltpu.VMEM((npat, s, DIL_HD), F32), pltpu.VMEM((npat, s, LANES), F32),
                        pltpu.VMEM((npat, s, LANES), F32),
                        pltpu.VMEM((2, DIL_UNROLL * DIL_BAND, 2 * DIL_BAND), F32),
                        pltpu.VMEM((2, DIL_UNROLL * DIL_BAND, 2 * DIL_BAND), BF16)],
        compiler_params=_cparams("parallel", "parallel"),
        name="dilated_attention",
    )(h3, h3, h3)


def _resident(shape):
    return pl.BlockSpec(shape, lambda *_: (0,) * len(shape), pipeline_mode=pl.Buffered(1))


def _load_weight_bf16(w_hbm, dst_ref, stage_ref, sem):
    rows = stage_ref.shape[1]
    nchunk = w_hbm.shape[0] // rows
    copies = [pltpu.make_async_copy(w_hbm.at[pl.ds(c * rows, rows), :], stage_ref.at[c % 2],
                                    sem.at[c % 2]) for c in range(nchunk)]
    copies[0].start()
    for c in range(nchunk):
        if c + 1 < nchunk:
            copies[c + 1].start()
        copies[c].wait()
        dst_ref[c * rows:(c + 1) * rows, :] = stage_ref[c % 2].astype(BF16)


_WEIGHT_STAGE = [pltpu.VMEM((2, WEIGHT_ROWS, 2048), F32), pltpu.SemaphoreType.DMA((2,))]


def _outproj_kernel(og_ref, od_ref, x_ref, g_ref, b_ref, w_hbm, o_ref, w_ref, stage_ref, sem,
                    *, alpha):
    @pl.when(pl.program_id(0) == 0)
    def _():
        _load_weight_bf16(w_hbm, w_ref, stage_ref, sem)

    ka = og_ref.shape[1]
    acc = _dot(og_ref[...], w_ref[:ka, :]) + _dot(od_ref[...], w_ref[ka:, :])
    o_ref[...] = _layer_norm(alpha * x_ref[...] + acc, g_ref[...], b_ref[...])


def _outproj(og, od, w, x2d, g, b, bm, alpha):
    t, d = x2d.shape
    ka, kb = og.shape[1], od.shape[1]
    assert w.shape == (ka + kb, d) and d == _WEIGHT_STAGE[0].shape[2]
    return pl.pallas_call(
        functools.partial(_outproj_kernel, alpha=alpha),
        grid=(t // bm,),
        in_specs=[pl.BlockSpec((bm, ka), lambda i: (i, 0)),
                  pl.BlockSpec((bm, kb), lambda i: (i, 0)),
                  pl.BlockSpec((bm, d), lambda i: (i, 0)),
                  _resident((1, d)), _resident((1, d)),
                  pl.BlockSpec(memory_space=pl.ANY)],
        out_specs=pl.BlockSpec((bm, d), lambda i: (i, 0)),
        out_shape=jax.ShapeDtypeStruct((t, d), F32),
        scratch_shapes=[pltpu.VMEM((ka + kb, d), BF16)] + _WEIGHT_STAGE,
        compiler_params=_cparams("arbitrary"),
        name="out_projection_ln1",
    )(og, od, x2d, g, b, w)


def _memkv_kernel(m_ref, w_ref, o_ref):
    o_ref[...] = _dot(m_ref[...].astype(BF16), w_ref[...].astype(BF16)).astype(o_ref.dtype)


def _memkv(mem2d, wkv, bn):
    t, d = mem2d.shape
    n = wkv.shape[1]
    return pl.pallas_call(
        _memkv_kernel,
        grid=(n // bn,),
        in_specs=[pl.BlockSpec((t, d), lambda j: (0, 0)),
                  pl.BlockSpec((d, bn), lambda j: (0, j))],
        out_specs=pl.BlockSpec((t, bn), lambda j: (0, j)),
        out_shape=jax.ShapeDtypeStruct((t, n), BF16),
        compiler_params=_cparams("parallel"),
        name="memory_kv_projection",
    )(mem2d, wkv)


def _cross_kernel(x_ref, kv_ref, g_ref, b_ref, wq_hbm, wo_hbm, o_ref,
                  wq_ref, wo_ref, oc_ref, stage_ref, sem, *, alpha):
    @pl.when(pl.program_id(0) == 0)
    def _():
        _load_weight_bf16(wq_hbm, wq_ref, stage_ref, sem)
        _load_weight_bf16(wo_hbm, wo_ref, stage_ref, sem)

    d = x_ref.shape[1]
    hd = d // CA_HEADS
    x = x_ref[...]
    q = _dot(x.astype(BF16), wq_ref[...]).astype(BF16)
    for h in range(CA_HEADS):
        cols = slice(h * hd, (h + 1) * hd)
        mk = kv_ref[0, :, cols]
        mv = kv_ref[0, :, d + h * hd:d + (h + 1) * hd]
        sc = _dot_nt(q[:, cols], mk) * (hd ** -0.5)
        m = jnp.max(sc, axis=-1, keepdims=True)
        e = jnp.exp(sc - m)
        p = e / jnp.sum(e, axis=-1, keepdims=True)
        oc_ref[:, cols] = _dot(p.astype(BF16), mv).astype(BF16)
    y = alpha * x + _dot(oc_ref[...], wo_ref[...])
    o_ref[...] = _layer_norm(y, g_ref[...], b_ref[...])


def _cross(x1, wq, kv3, wo, g, b, bm, seq, alpha):
    t, d = x1.shape
    m = kv3.shape[1]
    per = seq // bm
    hbm = pl.BlockSpec(memory_space=pl.ANY)
    assert d == _WEIGHT_STAGE[0].shape[2]
    return pl.pallas_call(
        functools.partial(_cross_kernel, alpha=alpha),
        grid=(t // bm,),
        in_specs=[pl.BlockSpec((bm, d), lambda i: (i, 0)),
                  pl.BlockSpec((1, m, 2 * d), lambda i: (i // per, 0, 0)),
                  _resident((1, d)), _resident((1, d)), hbm, hbm],
        out_specs=pl.BlockSpec((bm, d), lambda i: (i, 0)),
        out_shape=jax.ShapeDtypeStruct((t, d), F32),
        scratch_shapes=[pltpu.VMEM((d, d), BF16), pltpu.VMEM((d, d), BF16),
                        pltpu.VMEM((bm, d), BF16)] + _WEIGHT_STAGE,
        compiler_params=_cparams("arbitrary"),
        name="cross_attention_ln2",
    )(x1, kv3, g, b, wq, wo)


def _causal_conv(u_ref, cw, cb, r0, n):
    h = FFN_TAIL + r0
    y = cb + cw[0:1, :] * u_ref[h - 2:h - 2 + n, :]
    y = y + cw[1:2, :] * u_ref[h - 1:h - 1 + n, :]
    return y + cw[2:3, :] * u_ref[h:h + n, :]


def _ffn_kernel(cp_ref, g_ref, b_ref, x_hbm, w1_hbm, w2_hbm, o_ref,
                x_buf, xb_ref, act0_ref, act1_ref, us_ref, tail_ref, wg_buf, wu_buf, w2_buf,
                sem, x_sem, *, alpha, per):
    i = pl.program_id(0)
    d_ff = w2_hbm.shape[0]
    bm = x_buf.shape[0]
    bf = wg_buf.shape[2]
    nf = pl.cdiv(d_ff, bf)
    last_w = d_ff - (nf - 1) * bf
    assert nf % 2 == 1 and nf >= 3
    assert bf % FFN_SUB == 0 and last_w % LANES == 0 and bm % FFN_ROWS == 0
    c_first = i * nf
    acts = (act0_ref, act1_ref)
    nslab = bf // LANES
    half = o_ref.shape[1] // 2

    def width(f):
        return last_w if isinstance(f, int) and f == nf - 1 else bf

    def up_copies(f, slot):
        w = width(f)
        col = f * bf if isinstance(f, int) else pl.multiple_of(f * bf, bf)
        return (pltpu.make_async_copy(w1_hbm.at[:, pl.ds(col, w)],
                                      wg_buf.at[slot, :, pl.ds(0, w)], sem.at[0, slot]),
                pltpu.make_async_copy(w1_hbm.at[:, pl.ds(d_ff + col, w)],
                                      wu_buf.at[slot, :, pl.ds(0, w)], sem.at[1, slot]))

    def down_copy(f, slot):
        w = width(f)
        row = f * bf if isinstance(f, int) else pl.multiple_of(f * bf, bf)
        return pltpu.make_async_copy(w2_hbm.at[pl.ds(row, w), :],
                                     w2_buf.at[slot, pl.ds(0, w), :], sem.at[2, slot])

    def x_copy(step):
        row = pl.multiple_of(step * bm, bm)
        return pltpu.make_async_copy(x_hbm.at[pl.ds(row, bm), :], x_buf, x_sem.at[0])

    def region_copies(f):
        slot = lax.rem(c_first + f, 2)
        for cp in up_copies(f, slot):
            cp.wait()
        if isinstance(f, int) and f == 0:
            @pl.when(i + 1 < pl.num_programs(0))
            def _():
                x_copy(i + 1).start()
        if not (isinstance(f, int) and f == 0):
            down_copy(f - 1, 1 - slot).wait()
        if isinstance(f, int) and f == nf - 1:
            @pl.when(i + 1 < pl.num_programs(0))
            def _():
                for cp in up_copies(0, 1 - slot):
                    cp.start()
        else:
            for cp in up_copies(f + 1, 1 - slot):
                cp.start()
        down_copy(f, slot).start()
        return slot

    def up_matmul(slot, c0, wcols, r0):
        xr = xb_ref[r0:r0 + FFN_ROWS, :]
        rows = slice(FFN_TAIL + r0, FFN_TAIL + r0 + FFN_ROWS)
        for base, w_buf in ((0, wg_buf), (nslab, wu_buf)):
            res = _dot(xr, w_buf[slot, :, c0:c0 + wcols])
            for k in range(wcols // LANES):
                us_ref[base + c0 // LANES + k, rows, :] = res[:, k * LANES:(k + 1) * LANES]

    def activate(cv, act_ref, c0, wcols, r0):
        for s in range(c0 // LANES, (c0 + wcols) // LANES):
            cols = slice(s * LANES, (s + 1) * LANES)
            gate = _causal_conv(us_ref.at[s], cv[0:3, cols], cv[3:4, cols], r0, FFN_ROWS)
            up = _causal_conv(us_ref.at[nslab + s], cv[4:7, cols], cv[7:8, cols], r0, FFN_ROWS)
            act_ref[r0:r0 + FFN_ROWS, cols] = (gate * _sigmoid(gate) * up).astype(BF16)

    def down_matmul(slot, act_ref, w, n0):
        o_ref[:, n0:n0 + half] += _dot(act_ref[:, :w], w2_buf[slot, :w, n0:n0 + half])

    def region(f, act_slot):
        slot = region_copies(f)
        w = width(f)
        cv = cp_ref[f]
        act_ref, prev_ref = acts[act_slot], acts[1 - act_slot]
        slabs = [s for base in (0, nslab) for s in range(base, base + w // LANES)]
        for s in slabs:
            us_ref[s, :FFN_TAIL, :] = tail_ref[f, :, s * LANES:(s + 1) * LANES]
        units = [(c0, min(FFN_SUB, w - c0), r0) for c0 in range(0, w, FFN_SUB)
                 for r0 in range(0, bm, FFN_ROWS)]
        for unit in units:
            up_matmul(slot, *unit)
        if not (isinstance(f, int) and f == 0):
            for n0 in (0, half):
                down_matmul(1 - slot, prev_ref, width(f - 1) if isinstance(f, int) else bf, n0)
        for unit in units:
            activate(cv, act_ref, *unit)
        for s in slabs:
            tail_ref[f, :, s * LANES:(s + 1) * LANES] = us_ref[s, bm:bm + FFN_TAIL, :]

    @pl.when(i == 0)
    def _():
        x_copy(0).start()
        for cp in up_copies(0, 0):
            cp.start()

    @pl.when(i % per == 0)
    def _():
        tail_ref[...] = jnp.zeros_like(tail_ref)

    x_copy(i).wait()
    xb_ref[...] = x_buf[...].astype(BF16)
    o_ref[...] = alpha * x_buf[...]

    region(0, 0)

    def pair(j, carry):
        region(2 * j + 1, 1)
        region(2 * j + 2, 0)
        return carry

    lax.fori_loop(0, (nf - 3) // 2, pair, 0)
    region(nf - 2, 1)
    region(nf - 1, 0)

    last = lax.rem(c_first + nf - 1, 2)
    down_copy(nf - 1, last).wait()
    for n0 in (0, half):
        down_matmul(last, acts[0], last_w, n0)
    o_ref[...] = _layer_norm(o_ref[...], g_ref[...], b_ref[...])


def _ffn(x2, w1, w2, cp, g, b, bm, seq, alpha):
    t, d = x2.shape
    bf = FFN_CHUNK
    nf = cp.shape[0]
    hbm = pl.BlockSpec(memory_space=pl.ANY)
    return pl.pallas_call(
        functools.partial(_ffn_kernel, alpha=alpha, per=seq // bm),
        grid=(t // bm,),
        in_specs=[_resident(cp.shape), _resident((1, d)), _resident((1, d)), hbm, hbm, hbm],
        out_specs=pl.BlockSpec((bm, d), lambda i: (i, 0)),
        out_shape=jax.ShapeDtypeStruct((t, d), F32),
        scratch_shapes=[pltpu.VMEM((bm, d), F32), pltpu.VMEM((bm, d), BF16),
                        pltpu.VMEM((bm, bf), BF16), pltpu.VMEM((bm, bf), BF16),
                        pltpu.VMEM((2 * bf // LANES, FFN_TAIL + bm, LANES), F32),
                        pltpu.VMEM((nf, FFN_TAIL, 2 * bf), F32),
                        pltpu.VMEM((2, d, bf), BF16), pltpu.VMEM((2, d, bf), BF16),
                        pltpu.VMEM((2, bf, d), BF16), pltpu.SemaphoreType.DMA((3, 2)),
                        pltpu.SemaphoreType.DMA((1,))],
        compiler_params=_cparams("arbitrary"),
        name="conv_ffn_ln3",
    )(cp, g, b, x2, w1, w2)


def _pad_cols(a, n):
    return jnp.pad(a, ((0, 0), (0, n - a.shape[1])))


def kernel(x, mem, positions, w_in, gla_gate_w2, gla_gate_b, gla_norm_g, w_out, ln1_g, ln1_b,
           ca_wq, ca_wkv, ca_wo, ln2_g, ln2_b, ffn_w_in, ffn_conv_w, ffn_conv_b, ffn_w_out,
           ln3_g, ln3_b):
    bsz, seq, d = x.shape
    depth = w_in.shape[0]
    t = bsz * seq
    alpha = (2.0 * depth) ** 0.25
    d_ff = ffn_w_out.shape[1]
    dff_pad = -(-d_ff // FFN_CHUNK) * FFN_CHUNK

    nqk = GLA_HEADS * GLA_DK
    nv = GLA_HEADS * GLA_DV
    c_glr = 2 * nqk + 2 * nv
    c_dil = c_glr + GLA_GATE_RANK

    half = ROPE_HALF
    inv_freq = ROPE_THETA ** (-jnp.arange(0, ROPE_DIMS, 2, dtype=F32) / ROPE_DIMS)
    inv_row = jnp.concatenate([inv_freq, inv_freq, jnp.zeros((LANES - 2 * half,), F32)])[None, :]
    pos_col = positions.astype(F32).reshape(t, 1)
    cosf, sina, sinb = _rope_tables(pos_col, inv_row, min(t, 2048))

    x2d = x.reshape(t, d)
    for l in range(depth):
        wcat, wglr = _prep_win(jnp.swapaxes(w_in[l], 0, 1), c_glr, c_dil, 256)
        h, glr = _inproj(x2d, wcat, wglr, cosf, sina, sinb, min(t, 512))
        h3 = h.reshape(bsz, seq, h.shape[1])

        w2p = jnp.pad(gla_gate_w2[l], ((0, LANES - GLA_GATE_RANK), (0, 0))).astype(BF16)
        og = _gla(h3, glr.reshape(bsz, seq, LANES), w2p, gla_gate_b[l][None, :],
                  gla_norm_g[l][None, :], min(seq, 1024))
        od = _dil(h3)

        x1 = _outproj(og.reshape(t, nv), od.reshape(t, DIL_HEADS * DIL_HD), w_out[l],
                      x2d, ln1_g[l][None, :], ln1_b[l][None, :], min(t, 512), alpha)

        kv = _memkv(mem.reshape(-1, d), ca_wkv[l], 1024)
        x2 = _cross(x1, ca_wq[l], kv.reshape(bsz, -1, 2 * d), ca_wo[l],
                    ln2_g[l][None, :], ln2_b[l][None, :], min(seq, 512), seq, alpha)

        cw = ffn_conv_w[l]
        cb = ffn_conv_b[l][None, :]
        nf = dff_pad // FFN_CHUNK
        conv = jnp.concatenate([cw[:, :d_ff], cb[:, :d_ff], cw[:, d_ff:], cb[:, d_ff:]], axis=0)
        conv = _pad_cols(conv, dff_pad).reshape(conv.shape[0], nf, FFN_CHUNK).transpose(1, 0, 2)
        x2d = _ffn(x2, ffn_w_in[l].astype(BF16), ffn_w_out[l].astype(BF16), conv,
                   ln3_g[l][None, :], ln3_b[l][None, :], min(seq, 1024), seq, alpha)
    return x2d.reshape(bsz, seq, d)
```

```python
import functools

import jax
import jax.numpy as jnp
from jax import lax
from jax.experimental import pallas as pl
from jax.experimental.pallas import tpu as pltpu

F32 = jnp.float32
BF16 = jnp.bfloat16

LANES = 128
LN_EPS = 1e-5
GLA_HEADS = 4
GLA_DK = 128
GLA_DV = 256
GLA_GATE_RANK = 16
GLA_TAU = 16.0
GLA_CHUNK = 64
DIL_HD = 128
DIL_HEADS = 8
DIL_PATTERNS = ((128, 1), (512, 4), (2048, 16))
DIL_BAND = 128
DIL_UNROLL = 4
ROPE_THETA = 500000.0
ROPE_DIMS = 32
ROPE_HALF = ROPE_DIMS // 2
CA_HEADS = 4
CONV_W = 3
INPROJ_BLOCK = 1024
INPROJ_SUB = 256
WEIGHT_ROWS = 256
FFN_CHUNK = 512
FFN_CAST_COLS = 256
FFN_SUB = 256
FFN_ROWS = 512
FFN_TAIL = 8
VMEM_LIMIT = 56 * 1024 * 1024


def _cparams(*sem):
    return pltpu.CompilerParams(dimension_semantics=sem, vmem_limit_bytes=VMEM_LIMIT)


def _dot(a, b):
    return jnp.dot(a, b, preferred_element_type=F32)


def _dot_nt(a, b):
    return lax.dot_general(a, b, (((1,), (1,)), ((), ())), preferred_element_type=F32)


def _dot_tn(a, b):
    return lax.dot_general(a, b, (((0,), (0,)), ((), ())), preferred_element_type=F32)


def _layer_norm(y, g, b):
    mu = jnp.mean(y, axis=-1, keepdims=True)
    d = y - mu
    var = jnp.mean(d * d, axis=-1, keepdims=True)
    return d * lax.rsqrt(var + LN_EPS) * g + b


def _sigmoid(x):
    return 1.0 / (1.0 + jnp.exp(-x))


def _rope_kernel(pos_ref, inv_ref, cos_ref, sa_ref, sb_ref):
    ang = pos_ref[...] * inv_ref[...]
    lane = lax.broadcasted_iota(jnp.int32, ang.shape, 1)
    c = jnp.cos(ang)
    s = jnp.sin(ang)
    cos_ref[...] = jnp.where(lane < ROPE_DIMS, c, 1.0)
    sa_ref[...] = jnp.where(lane < ROPE_HALF, -s, 0.0)
    sb_ref[...] = jnp.where(lane < ROPE_HALF, 0.0, jnp.where(lane < ROPE_DIMS, s, 0.0))


def _rope_tables(pos_col, inv_row, bs):
    t = pos_col.shape[0]
    out = jax.ShapeDtypeStruct((t, LANES), F32)
    spec = pl.BlockSpec((bs, LANES), lambda i: (i, 0))
    return pl.pallas_call(
        _rope_kernel,
        grid=(t // bs,),
        in_specs=[pl.BlockSpec((bs, 1), lambda i: (i, 0)),
                  pl.BlockSpec((1, LANES), lambda i: (0, 0))],
        out_specs=[spec, spec, spec],
        out_shape=[out, out, out],
        compiler_params=_cparams("parallel"),
        name="rope_tables",
    )(pos_col, inv_row)


def _prep_win_kernel(wt_hbm, wcat_ref, wglr_ref, buf, gbuf, sem, *, c_glr, c_dil):
    j = pl.program_id(0)
    bn = buf.shape[1]
    rank = c_dil - c_glr

    def copy(jj, slot):
        row = jj * bn
        row = pl.multiple_of(row + jnp.where(row >= c_glr, rank, 0), 8)
        return pltpu.make_async_copy(wt_hbm.at[pl.ds(row, bn), :], buf.at[slot], sem.at[slot])

    gate_copy = pltpu.make_async_copy(wt_hbm.at[pl.ds(c_glr, rank), :], gbuf.at[pl.ds(0, rank), :],
                                      sem.at[2])

    @pl.when(j == 0)
    def _():
        copy(0, 0).start()
        gate_copy.start()
        gbuf[rank:, :] = jnp.zeros((gbuf.shape[0] - rank, gbuf.shape[1]), F32)

    slot = lax.rem(j, 2)

    @pl.when(j + 1 < pl.num_programs(0))
    def _():
        copy(j + 1, 1 - slot).start()

    copy(j, slot).wait()
    wcat_ref[...] = buf[slot].T.astype(BF16)

    @pl.when(j == 0)
    def _():
        gate_copy.wait()
        wglr_ref[...] = gbuf[...].T.astype(BF16)


def _prep_win(wt, c_glr, c_dil, bn):
    ncols, d = wt.shape
    ncat = ncols - (c_dil - c_glr)
    assert c_glr % bn == 0 and ncat % bn == 0 and (c_dil - c_glr) % 8 == 0
    return pl.pallas_call(
        functools.partial(_prep_win_kernel, c_glr=c_glr, c_dil=c_dil),
        grid=(ncat // bn,),
        in_specs=[pl.BlockSpec(memory_space=pl.ANY)],
        out_specs=[pl.BlockSpec((d, bn), lambda j: (0, j)),
                   pl.BlockSpec((d, LANES), lambda j: (0, 0))],
        out_shape=[jax.ShapeDtypeStruct((d, ncat), BF16), jax.ShapeDtypeStruct((d, LANES), BF16)],
        scratch_shapes=[pltpu.VMEM((2, bn, d), F32), pltpu.VMEM((LANES, d), F32),
                        pltpu.SemaphoreType.DMA((3,))],
        compiler_params=_cparams("arbitrary"),
        name="in_projection_weights",
    )(wt)


def _rope(t, cos, sa, sb):
    return t * cos + pltpu.roll(t, LANES - ROPE_HALF, 1) * sa + pltpu.roll(t, ROPE_HALF, 1) * sb


def _inproj_kernel(x_ref, w_ref, wg_ref, cos_ref, sa_ref, sb_ref, h_ref, glr_ref, xb_ref):
    bn = INPROJ_BLOCK
    xb_ref[...] = x_ref[...].astype(BF16)
    glr_ref[...] = _dot(xb_ref[...], wg_ref[...])

    def rope(scale):
        def epilogue(acc, c0):
            cos, sa, sb = cos_ref[...], sa_ref[...], sb_ref[...]
            heads = [acc[:, j:j + LANES] for j in range(0, acc.shape[1], LANES)]
            if scale is not None:
                heads = [t * scale for t in heads]
            return jnp.concatenate([_rope(t, cos, sa, sb) for t in heads], axis=1)
        return epilogue

    def plain(acc, c0):
        return acc

    epilogues = (lambda acc, c0: acc * (GLA_DK ** -0.5) if c0 < bn // 2 else acc,
                 plain, plain, rope(DIL_HD ** -0.5), rope(None), plain)
    assert len(epilogues) * bn == w_ref.shape[1]

    for n, epilogue in enumerate(epilogues):
        for c0 in range(0, bn, INPROJ_SUB):
            cols = slice(n * bn + c0, n * bn + c0 + INPROJ_SUB)
            h_ref[:, cols] = epilogue(_dot(xb_ref[...], w_ref[:, cols]), c0).astype(h_ref.dtype)


def _inproj(x2d, wcat, wglr, cosf, sina, sinb, bm):
    t, d = x2d.shape
    ncols = wcat.shape[1]
    tab = pl.BlockSpec((bm, LANES), lambda i: (i, 0))
    return pl.pallas_call(
        _inproj_kernel,
        grid=(t // bm,),
        in_specs=[pl.BlockSpec((bm, d), lambda i: (i, 0)),
                  _resident((d, ncols)), _resident((d, LANES)), tab, tab, tab],
        out_specs=[pl.BlockSpec((bm, ncols), lambda i: (i, 0)),
                   pl.BlockSpec((bm, LANES), lambda i: (i, 0))],
        out_shape=[jax.ShapeDtypeStruct((t, ncols), BF16),
                   jax.ShapeDtypeStruct((t, LANES), F32)],
        scratch_shapes=[pltpu.VMEM((bm, d), BF16)],
        compiler_params=_cparams("parallel"),
        name="in_projection",
    )(x2d, wcat, wglr, cosf, sina, sinb)


def _split3(v):
    hi = v.astype(BF16)
    r1 = v - hi.astype(F32)
    mid = r1.astype(BF16)
    lo = (r1 - mid.astype(F32)).astype(BF16)
    return hi, mid, lo


def _side_cast_step(step, nsteps, srcs, dsts, in_bufs, out_bufs, sem):
    slot = lax.rem(step, 2)
    arrays = range(len(srcs))

    def rows_of(k, s):
        rows = in_bufs[k].shape[1]
        return pl.ds(pl.multiple_of(s * rows, 16), rows)

    def fetch(k, s, sl):
        return pltpu.make_async_copy(srcs[k].at[rows_of(k, s), :], in_bufs[k].at[sl], sem.at[0, k, sl])

    def write(k, s, sl):
        return pltpu.make_async_copy(out_bufs[k].at[sl], dsts[k].at[rows_of(k, s), :], sem.at[1, k, sl])

    @pl.when(step == 0)
    def _():
        for k in arrays:
            fetch(k, 0, 0).start()

    for k in arrays:
        fetch(k, step, slot).wait()

    @pl.when(step + 1 < nsteps)
    def _():
        for k in arrays:
            fetch(k, step + 1, 1 - slot).start()

    @pl.when(step >= 2)
    def _():
        for k in arrays:
            write(k, step - 2, slot).wait()

    for k in arrays:
        out_bufs[k][slot] = in_bufs[k][slot].astype(BF16)
        write(k, step, slot).start()

    @pl.when(step == nsteps - 1)
    def _():
        for k in arrays:
            write(k, step, slot).wait()
            write(k, step - 1, 1 - slot).wait()


def _gla_kernel(q_ref, k_ref, v_ref, r_ref, glr_ref, w2_ref, gb_ref, ng_ref, wa_hbm, wb_hbm,
                o_ref, wa_out, wb_out, st_ref, wa_in, wb_in, wa_cast, wb_cast, cast_sem):
    c = GLA_CHUNK
    npid = [pl.num_programs(a) for a in range(3)]
    step = (pl.program_id(0) * npid[1] + pl.program_id(1)) * npid[2] + pl.program_id(2)
    _side_cast_step(step, npid[0] * npid[1] * npid[2], (wa_hbm, wb_hbm), (wa_out, wb_out),
                    (wa_in, wb_in), (wa_cast, wb_cast), cast_sem)
    sb = q_ref.shape[1]
    grp = 4 * c

    @pl.when(pl.program_id(2) == 0)
    def _():
        st_ref[...] = jnp.zeros_like(st_ref)

    z = _dot(glr_ref[0].astype(BF16), w2_ref[...]) + gb_ref[...]
    lg = (jnp.minimum(z, 0.0) - jnp.log1p(jnp.exp(-jnp.abs(z)))) / GLA_TAU

    row = lax.broadcasted_iota(jnp.int32, (2 * grp, grp), 0)
    col = lax.broadcasted_iota(jnp.int32, (2 * grp, grp), 1)
    rr = jnp.where(row < grp, row, row - grp)
    shift = c.bit_length() - 1
    same = (rr >> shift) == (col >> shift)
    lower = jnp.where(same & (col <= rr), 1.0, 0.0)
    upper = jnp.where(same & (col > rr), 1.0, 0.0)
    lu = jnp.where(row < grp, lower, upper).astype(BF16)
    b_parts, e_parts = [], []
    for g0 in range(0, sb, grp):
        pieces = jnp.concatenate(_split3(lg[g0:g0 + grp]), axis=1)
        res = _dot(lu, pieces)
        tot = res[:, :LANES] + res[:, LANES:2 * LANES] + res[:, 2 * LANES:]
        b_parts.append(tot[:grp])
        e_parts.append(tot[grp:])
    b = jnp.concatenate(b_parts, axis=0)
    brest = jnp.concatenate(e_parts, axis=0)

    qf = q_ref[0].astype(F32)
    kf = k_ref[0].astype(F32)
    q_in = (qf * jnp.exp(b)).astype(BF16)
    k_in = (kf * jnp.exp(-b)).astype(BF16)
    k_end = (kf * jnp.exp(brest)).astype(BF16)
    v = v_ref[0]

    ci = lax.broadcasted_iota(jnp.int32, (c, c), 0)
    cj = lax.broadcasted_iota(jnp.int32, (c, c), 1)
    causal = cj <= ci

    st = st_ref[...]
    outs = []
    for i in range(sb // c):
        rows = slice(i * c, (i + 1) * c)
        a = jnp.where(causal, _dot_nt(q_in[rows], k_in[rows]), 0.0).astype(BF16)
        o = _dot(a, v[rows]) + _dot_nt(q_in[rows], st.astype(BF16))
        outs.append(o)
        decay = jnp.exp(b[i * c + c - 1:i * c + c, :])
        st = st * decay + _dot_tn(v[rows], k_end[rows])
    st_ref[...] = st

    o = jnp.concatenate(outs, axis=0)
    mu = jnp.mean(o, axis=-1, keepdims=True)
    d = o - mu
    var = jnp.mean(d * d, axis=-1, keepdims=True)
    rg = r_ref[0].astype(F32)
    y = d * lax.rsqrt(var + LN_EPS) * ng_ref[...] * (rg * _sigmoid(rg))
    o_ref[0] = y.astype(o_ref.dtype)


def _gla(h3, glr3, w2p, gb, ng, sb, side_a, side_b):
    bsz, s, _ = h3.shape
    nsteps = bsz * GLA_HEADS * (s // sb)
    assert nsteps >= 2 and all(a.shape[0] % (16 * nsteps) == 0 for a in (side_a, side_b))
    ra, rb_ = side_a.shape[0] // nsteps, side_b.shape[0] // nsteps
    hbm = pl.BlockSpec(memory_space=pl.ANY)
    kb = GLA_HEADS * GLA_DK // LANES
    vb = 2 * GLA_HEADS * GLA_DK // GLA_DV
    rb = vb + GLA_HEADS
    return pl.pallas_call(
        _gla_kernel,
        grid=(bsz, GLA_HEADS, s // sb),
        in_specs=[pl.BlockSpec((1, sb, GLA_DK), lambda b, h, j: (b, j, h)),
                  pl.BlockSpec((1, sb, GLA_DK), lambda b, h, j: (b, j, kb + h)),
                  pl.BlockSpec((1, sb, GLA_DV), lambda b, h, j: (b, j, vb + h)),
                  pl.BlockSpec((1, sb, GLA_DV), lambda b, h, j: (b, j, rb + h)),
                  pl.BlockSpec((1, sb, LANES), lambda b, h, j: (b, j, 0)),
                  pl.BlockSpec((LANES, GLA_DK), lambda b, h, j: (0, h)),
                  pl.BlockSpec((1, GLA_DK), lambda b, h, j: (0, h)),
                  pl.BlockSpec((1, GLA_DV), lambda b, h, j: (0, h)), hbm, hbm],
        out_specs=[pl.BlockSpec((1, sb, GLA_DV), lambda b, h, j: (b, j, h)), hbm, hbm],
        out_shape=[jax.ShapeDtypeStruct((bsz, s, GLA_HEADS * GLA_DV), BF16),
                   jax.ShapeDtypeStruct(side_a.shape, BF16), jax.ShapeDtypeStruct(side_b.shape, BF16)],
        scratch_shapes=[pltpu.VMEM((GLA_DV, GLA_DK), F32),
                        pltpu.VMEM((2, ra, side_a.shape[1]), F32), pltpu.VMEM((2, rb_, side_b.shape[1]), F32),
                        pltpu.VMEM((2, ra, side_a.shape[1]), BF16), pltpu.VMEM((2, rb_, side_b.shape[1]), BF16),
                        pltpu.SemaphoreType.DMA((2, 2, 2))],
        compiler_params=_cparams("arbitrary", "arbitrary", "arbitrary"),
        name="gla",
    )(h3, h3, h3, h3, glr3, w2p, gb, ng, side_a, side_b)


def _dil_kernel(q_ref, k_ref, v_ref, o_ref, qf, kf, vf, qg, kg, vg, qc, kc, vc, ob, db, mx,
                scb, eb):
    s = q_ref.shape[1]
    band = DIL_BAND
    unroll = DIL_UNROLL
    nblk = s // band
    qf[...] = q_ref[0].astype(F32)
    kf[...] = k_ref[0].astype(F32)
    vf[...] = v_ref[0].astype(F32)
    kc[:band, :] = jnp.zeros((band, DIL_HD), BF16)
    vc[:band, :] = jnp.zeros((band, DIL_HD), BF16)

    qi = lax.broadcasted_iota(jnp.int32, (band, 2 * band), 0)
    kj = lax.broadcasted_iota(jnp.int32, (band, 2 * band), 1)
    allowed = (kj >= qi) & (kj <= qi + band)
    bias = jnp.where(allowed, 0.0, -jnp.inf).astype(F32)
    bias0 = jnp.where(allowed & (kj >= band), 0.0, -jnp.inf).astype(F32)
    ones = jnp.ones((2 * band, LANES), BF16)

    for p, (window, dil) in enumerate(DIL_PATTERNS):
        assert window // dil == band
        cls = s // dil
        nb = cls // band
        span = band * dil
        assert nblk % unroll == 0 and (nb % unroll == 0 or unroll % nb == 0)

        if dil == 1:
            qc[...] = q_ref[0]
            kc[band:, :] = k_ref[0]
            vc[band:, :] = v_ref[0]
        else:
            prev = DIL_PATTERNS[p - 1][1]
            step = dil // prev
            assert step * prev == dil and step in (2, 4)
            keep = p + 1 < len(DIL_PATTERNS)
            srcs, dsts = ((qf, kf, vf), (qg, kg, vg)) if p % 2 == 1 else ((qg, kg, vg), (qf, kf, vf))
            for r in range(dil):
                rows = pl.ds((r % prev) * (s // prev) + r // prev, cls, stride=step)
                for src, dst, cm, off in zip(srcs, dsts, (qc, kc, vc), (0, band, band)):
                    x = src[rows, :]
                    if keep:
                        dst[r * cls:(r + 1) * cls, :] = x
                    cm[off + r * cls:off + (r + 1) * cls, :] = x.astype(BF16)

        def out_rows(g, lo=0, cnt=band, dil=dil, nb=nb, span=span):
            start = g // nb + (g % nb) * span + lo * dil
            return pl.ds(start, cnt) if dil == 1 else pl.ds(start, cnt, stride=dil)

        def scores(t, slot):
            for u in range(unroll):
                g = t * unroll + u
                k2 = kc[g * band:(g + 2) * band, :]
                scb[slot, u * band:(u + 1) * band, :] = _dot_nt(qc[g * band:(g + 1) * band, :], k2)

        def softmax(t, slot, p=p, nb=nb, out_rows=out_rows):
            for u in range(unroll):
                g = t * unroll + u
                bb = bias0 if g % nb == 0 else bias
                rows = slice(u * band, (u + 1) * band)
                m = jnp.max(scb[slot, rows, :] + bb, axis=-1, keepdims=True)
                mx[p, out_rows(g), :] = jnp.broadcast_to(m, (band, LANES))
                for half in range(2):
                    cols = slice(half * band, (half + 1) * band)
                    eb[slot, rows, cols] = jnp.exp(scb[slot, rows, cols] + bb[:, cols] - m).astype(BF16)

        def values(t, slot, p=p, out_rows=out_rows):
            for u in range(unroll):
                g = t * unroll + u
                v2 = vc[g * band:(g + 2) * band, :]
                oe = _dot(eb[slot, u * band:(u + 1) * band, :],
                          jnp.concatenate([v2, ones], axis=1))
                ob[p, out_rows(g), :] = oe[:, :DIL_HD]
                db[p, out_rows(g), :] = oe[:, DIL_HD:]

        ngrp = nblk // unroll
        for t in range(ngrp + 2):
            if t < ngrp:
                scores(t, t % 2)
            if 1 <= t <= ngrp:
                softmax(t - 1, (t - 1) % 2)
            if t >= 2:
                values(t - 2, t % 2)

    mb = 512

    def merge(i, carry):
        rs = pl.ds(pl.multiple_of(i * mb, mb), mb)
        m0, m1, m2 = mx[0, rs, :], mx[1, rs, :], mx[2, rs, :]
        m = jnp.maximum(jnp.maximum(m0, m1), m2)
        e0, e1, e2 = jnp.exp(m0 - m), jnp.exp(m1 - m), jnp.exp(m2 - m)
        num = e0 * ob[0, rs, :] + e1 * ob[1, rs, :] + e2 * ob[2, rs, :]
        den = e0 * db[0, rs, :] + e1 * db[1, rs, :] + e2 * db[2, rs, :]
        o_ref[0, rs, :] = (num / den).astype(o_ref.dtype)
        return carry

    lax.fori_loop(0, s // mb, merge, 0)


def _dil(h3):
    bsz, s, _ = h3.shape
    qb = (2 * GLA_HEADS * GLA_DK + 2 * GLA_HEADS * GLA_DV) // DIL_HD
    kb = qb + DIL_HEADS
    vb = kb + DIL_HEADS
    npat = len(DIL_PATTERNS)
    return pl.pallas_call(
        _dil_kernel,
        grid=(bsz, DIL_HEADS),
        in_specs=[pl.BlockSpec((1, s, DIL_HD), lambda b, h: (b, 0, qb + h)),
                  pl.BlockSpec((1, s, DIL_HD), lambda b, h: (b, 0, kb + h)),
                  pl.BlockSpec((1, s, DIL_HD), lambda b, h: (b, 0, vb + h))],
        out_specs=pl.BlockSpec((1, s, DIL_HD), lambda b, h: (b, 0, h)),
        out_shape=jax.ShapeDtypeStruct((bsz, s, DIL_HEADS * DIL_HD), BF16),
        scratch_shapes=[pltpu.VMEM((s, DIL_HD), F32)] * 6 + [
                        pltpu.VMEM((s, DIL_HD), BF16), pltpu.VMEM((s + DIL_BAND, DIL_HD), BF16),
                        pltpu.VMEM((s + DIL_BAND, DIL_HD), BF16),
                        pltpu.VMEM((npat, s, DIL_HD), F32), pltpu.VMEM((npat, s, LANES), F32),
                        pltpu.VMEM((npat, s, LANES), F32),
                        pltpu.VMEM((2, DIL_UNROLL * DIL_BAND, 2 * DIL_BAND), F32),
                        pltpu.VMEM((2, DIL_UNROLL * DIL_BAND, 2 * DIL_BAND), BF16)],
        compiler_params=_cparams("parallel", "parallel"),
        name="dilated_attention",
    )(h3, h3, h3)


def _resident(shape):
    return pl.BlockSpec(shape, lambda *_: (0,) * len(shape), pipeline_mode=pl.Buffered(1))


def _load_weight_bf16(w_hbm, dst_ref, stage_ref, sem):
    rows = stage_ref.shape[1]
    nchunk = w_hbm.shape[0] // rows
    copies = [pltpu.make_async_copy(w_hbm.at[pl.ds(c * rows, rows), :], stage_ref.at[c % 2],
                                    sem.at[c % 2]) for c in range(nchunk)]
    copies[0].start()
    for c in range(nchunk):
        if c + 1 < nchunk:
            copies[c + 1].start()
        copies[c].wait()
        dst_ref[c * rows:(c + 1) * rows, :] = stage_ref[c % 2].astype(BF16)


_WEIGHT_STAGE = [pltpu.VMEM((2, WEIGHT_ROWS, 2048), F32), pltpu.SemaphoreType.DMA((2,))]


def _outproj_kernel(og_ref, od_ref, x_ref, g_ref, b_ref, w_hbm, o_ref, w_ref, stage_ref, sem,
                    *, alpha):
    @pl.when(pl.program_id(0) == 0)
    def _():
        _load_weight_bf16(w_hbm, w_ref, stage_ref, sem)

    ka = og_ref.shape[1]
    acc = _dot(og_ref[...], w_ref[:ka, :]) + _dot(od_ref[...], w_ref[ka:, :])
    o_ref[...] = _layer_norm(alpha * x_ref[...] + acc, g_ref[...], b_ref[...])


def _outproj(og, od, w, x2d, g, b, bm, alpha):
    t, d = x2d.shape
    ka, kb = og.shape[1], od.shape[1]
    assert w.shape == (ka + kb, d) and d == _WEIGHT_STAGE[0].shape[2]
    return pl.pallas_call(
        functools.partial(_outproj_kernel, alpha=alpha),
        grid=(t // bm,),
        in_specs=[pl.BlockSpec((bm, ka), lambda i: (i, 0)),
                  pl.BlockSpec((bm, kb), lambda i: (i, 0)),
                  pl.BlockSpec((bm, d), lambda i: (i, 0)),
                  _resident((1, d)), _resident((1, d)),
                  pl.BlockSpec(memory_space=pl.ANY)],
        out_specs=pl.BlockSpec((bm, d), lambda i: (i, 0)),
        out_shape=jax.ShapeDtypeStruct((t, d), F32),
        scratch_shapes=[pltpu.VMEM((ka + kb, d), BF16)] + _WEIGHT_STAGE,
        compiler_params=_cparams("arbitrary"),
        name="out_projection_ln1",
    )(og, od, x2d, g, b, w)


def _memkv_kernel(m_ref, w_ref, o_ref):
    o_ref[...] = _dot(m_ref[...].astype(BF16), w_ref[...].astype(BF16)).astype(o_ref.dtype)


def _memkv(mem2d, wkv, bn):
    t, d = mem2d.shape
    n = wkv.shape[1]
    return pl.pallas_call(
        _memkv_kernel,
        grid=(n // bn,),
        in_specs=[pl.BlockSpec((t, d), lambda j: (0, 0)),
                  pl.BlockSpec((d, bn), lambda j: (0, j))],
        out_specs=pl.BlockSpec((t, bn), lambda j: (0, j)),
        out_shape=jax.ShapeDtypeStruct((t, n), BF16),
        compiler_params=_cparams("parallel"),
        name="memory_kv_projection",
    )(mem2d, wkv)


def _cross_kernel(x_ref, kv_ref, g_ref, b_ref, wq_hbm, wo_hbm, o_ref,
                  wq_ref, wo_ref, oc_ref, stage_ref, sem, *, alpha):
    @pl.when(pl.program_id(0) == 0)
    def _():
        _load_weight_bf16(wq_hbm, wq_ref, stage_ref, sem)
        _load_weight_bf16(wo_hbm, wo_ref, stage_ref, sem)

    d = x_ref.shape[1]
    hd = d // CA_HEADS
    x = x_ref[...]
    q = _dot(x.astype(BF16), wq_ref[...]).astype(BF16)
    for h in range(CA_HEADS):
        cols = slice(h * hd, (h + 1) * hd)
        mk = kv_ref[0, :, cols]
        mv = kv_ref[0, :, d + h * hd:d + (h + 1) * hd]
        sc = _dot_nt(q[:, cols], mk) * (hd ** -0.5)
        m = jnp.max(sc, axis=-1, keepdims=True)
        e = jnp.exp(sc - m)
        p = e / jnp.sum(e, axis=-1, keepdims=True)
        oc_ref[:, cols] = _dot(p.astype(BF16), mv).astype(BF16)
    y = alpha * x + _dot(oc_ref[...], wo_ref[...])
    o_ref[...] = _layer_norm(y, g_ref[...], b_ref[...])


def _cross(x1, wq, kv3, wo, g, b, bm, seq, alpha):
    t, d = x1.shape
    m = kv3.shape[1]
    per = seq // bm
    hbm = pl.BlockSpec(memory_space=pl.ANY)
    assert d == _WEIGHT_STAGE[0].shape[2]
    return pl.pallas_call(
        functools.partial(_cross_kernel, alpha=alpha),
        grid=(t // bm,),
        in_specs=[pl.BlockSpec((bm, d), lambda i: (i, 0)),
                  pl.BlockSpec((1, m, 2 * d), lambda i: (i // per, 0, 0)),
                  _resident((1, d)), _resident((1, d)), hbm, hbm],
        out_specs=pl.BlockSpec((bm, d), lambda i: (i, 0)),
        out_shape=jax.ShapeDtypeStruct((t, d), F32),
        scratch_shapes=[pltpu.VMEM((d, d), BF16), pltpu.VMEM((d, d), BF16),
                        pltpu.VMEM((bm, d), BF16)] + _WEIGHT_STAGE,
        compiler_params=_cparams("arbitrary"),
        name="cross_attention_ln2",
    )(x1, kv3, g, b, wq, wo)


def _causal_conv(u_ref, cw, cb, r0, n):
    h = FFN_TAIL + r0
    y = cb + cw[0:1, :] * u_ref[h - 2:h - 2 + n, :]
    y = y + cw[1:2, :] * u_ref[h - 1:h - 1 + n, :]
    return y + cw[2:3, :] * u_ref[h:h + n, :]


def _ffn_kernel(cp_ref, g_ref, b_ref, x_hbm, w1_hbm, w2_hbm, o_ref,
                x_buf, xb_ref, act0_ref, act1_ref, us_ref, tail_ref, wg_buf, wu_buf, w2_buf,
                sem, x_sem, *, alpha, per):
    i = pl.program_id(0)
    d_ff = w2_hbm.shape[0]
    bm = x_buf.shape[0]
    bf = wg_buf.shape[2]
    nf = pl.cdiv(d_ff, bf)
    last_w = d_ff - (nf - 1) * bf
    assert nf % 2 == 1 and nf >= 3
    assert bf % FFN_SUB == 0 and last_w % LANES == 0 and bm % FFN_ROWS == 0
    c_first = i * nf
    acts = (act0_ref, act1_ref)
    nslab = bf // LANES
    half = o_ref.shape[1] // 2

    def width(f):
        return last_w if isinstance(f, int) and f == nf - 1 else bf

    def up_copies(f, slot):
        w = width(f)
        col = f * bf if isinstance(f, int) else pl.multiple_of(f * bf, bf)
        return (pltpu.make_async_copy(w1_hbm.at[:, pl.ds(col, w)],
                                      wg_buf.at[slot, :, pl.ds(0, w)], sem.at[0, slot]),
                pltpu.make_async_copy(w1_hbm.at[:, pl.ds(d_ff + col, w)],
                                      wu_buf.at[slot, :, pl.ds(0, w)], sem.at[1, slot]))

    def down_copy(f, slot):
        w = width(f)
        row = f * bf if isinstance(f, int) else pl.multiple_of(f * bf, bf)
        return pltpu.make_async_copy(w2_hbm.at[pl.ds(row, w), :],
                                     w2_buf.at[slot, pl.ds(0, w), :], sem.at[2, slot])

    def x_copy(step):
        row = pl.multiple_of(step * bm, bm)
        return pltpu.make_async_copy(x_hbm.at[pl.ds(row, bm), :], x_buf, x_sem.at[0])

    def region_copies(f):
        slot = lax.rem(c_first + f, 2)
        for cp in up_copies(f, slot):
            cp.wait()
        if isinstance(f, int) and f == 0:
            @pl.when(i + 1 < pl.num_programs(0))
            def _():
                x_copy(i + 1).start()
        if not (isinstance(f, int) and f == 0):
            down_copy(f - 1, 1 - slot).wait()
        if isinstance(f, int) and f == nf - 1:
            @pl.when(i + 1 < pl.num_programs(0))
            def _():
                for cp in up_copies(0, 1 - slot):
                    cp.start()
        else:
            for cp in up_copies(f + 1, 1 - slot):
                cp.start()
        down_copy(f, slot).start()
        return slot

    def up_matmul(slot, c0, wcols, r0):
        xr = xb_ref[r0:r0 + FFN_ROWS, :]
        rows = slice(FFN_TAIL + r0, FFN_TAIL + r0 + FFN_ROWS)
        for base, w_buf in ((0, wg_buf), (nslab, wu_buf)):
            res = _dot(xr, w_buf[slot, :, c0:c0 + wcols])
            for k in range(wcols // LANES):
                us_ref[base + c0 // LANES + k, rows, :] = res[:, k * LANES:(k + 1) * LANES]

    def activate(cv, act_ref, c0, wcols, r0):
        for s in range(c0 // LANES, (c0 + wcols) // LANES):
            cols = slice(s * LANES, (s + 1) * LANES)
            gate = _causal_conv(us_ref.at[s], cv[0:3, cols], cv[3:4, cols], r0, FFN_ROWS)
            up = _causal_conv(us_ref.at[nslab + s], cv[4:7, cols], cv[7:8, cols], r0, FFN_ROWS)
            act_ref[r0:r0 + FFN_ROWS, cols] = (gate * _sigmoid(gate) * up).astype(BF16)

    def down_matmul(slot, act_ref, w, n0):
        o_ref[:, n0:n0 + half] += _dot(act_ref[:, :w], w2_buf[slot, :w, n0:n0 + half])

    def region(f, act_slot):
        slot = region_copies(f)
        w = width(f)
        cv = cp_ref[f]
        act_ref, prev_ref = acts[act_slot], acts[1 - act_slot]
        slabs = [s for base in (0, nslab) for s in range(base, base + w // LANES)]
        for s in slabs:
            us_ref[s, :FFN_TAIL, :] = tail_ref[f, :, s * LANES:(s + 1) * LANES]
        units = [(c0, min(FFN_SUB, w - c0), r0) for c0 in range(0, w, FFN_SUB)
                 for r0 in range(0, bm, FFN_ROWS)]
        for unit in units:
            up_matmul(slot, *unit)
        if not (isinstance(f, int) and f == 0):
            for n0 in (0, half):
                down_matmul(1 - slot, prev_ref, width(f - 1) if isinstance(f, int) else bf, n0)
        for unit in units:
            activate(cv, act_ref, *unit)
        for s in slabs:
            tail_ref[f, :, s * LANES:(s + 1) * LANES] = us_ref[s, bm:bm + FFN_TAIL, :]

    @pl.when(i == 0)
    def _():
        x_copy(0).start()
        for cp in up_copies(0, 0):
            cp.start()

    @pl.when(i % per == 0)
    def _():
        tail_ref[...] = jnp.zeros_like(tail_ref)

    x_copy(i).wait()
    xb_ref[...] = x_buf[...].astype(BF16)
    o_ref[...] = alpha * x_buf[...]

    region(0, 0)

    def pair(j, carry):
        region(2 * j + 1, 1)
        region(2 * j + 2, 0)
        return carry

    lax.fori_loop(0, (nf - 3) // 2, pair, 0)
    region(nf - 2, 1)
    region(nf - 1, 0)

    last = lax.rem(c_first + nf - 1, 2)
    down_copy(nf - 1, last).wait()
    for n0 in (0, half):
        down_matmul(last, acts[0], last_w, n0)
    o_ref[...] = _layer_norm(o_ref[...], g_ref[...], b_ref[...])


def _ffn(x2, w1, w2, cp, g, b, bm, seq, alpha):
    t, d = x2.shape
    bf = FFN_CHUNK
    nf = cp.shape[0]
    hbm = pl.BlockSpec(memory_space=pl.ANY)
    return pl.pallas_call(
        functools.partial(_ffn_kernel, alpha=alpha, per=seq // bm),
        grid=(t // bm,),
        in_specs=[_resident(cp.shape), _resident((1, d)), _resident((1, d)), hbm, hbm, hbm],
        out_specs=pl.BlockSpec((bm, d), lambda i: (i, 0)),
        out_shape=jax.ShapeDtypeStruct((t, d), F32),
        scratch_shapes=[pltpu.VMEM((bm, d), F32), pltpu.VMEM((bm, d), BF16),
                        pltpu.VMEM((bm, bf), BF16), pltpu.VMEM((bm, bf), BF16),
                        pltpu.VMEM((2 * bf // LANES, FFN_TAIL + bm, LANES), F32),
                        pltpu.VMEM((nf, FFN_TAIL, 2 * bf), F32),
                        pltpu.VMEM((2, d, bf), BF16), pltpu.VMEM((2, d, bf), BF16),
                        pltpu.VMEM((2, bf, d), BF16), pltpu.SemaphoreType.DMA((3, 2)),
                        pltpu.SemaphoreType.DMA((1,))],
        compiler_params=_cparams("arbitrary"),
        name="conv_ffn_ln3",
    )(cp, g, b, x2, w1, w2)


def _pad_cols(a, n):
    return jnp.pad(a, ((0, 0), (0, n - a.shape[1])))


def kernel(x, mem, positions, w_in, gla_gate_w2, gla_gate_b, gla_norm_g, w_out, ln1_g, ln1_b,
           ca_wq, ca_wkv, ca_wo, ln2_g, ln2_b, ffn_w_in, ffn_conv_w, ffn_conv_b, ffn_w_out,
           ln3_g, ln3_b):
    bsz, seq, d = x.shape
    depth = w_in.shape[0]
    t = bsz * seq
    alpha = (2.0 * depth) ** 0.25
    d_ff = ffn_w_out.shape[1]
    dff_pad = -(-d_ff // FFN_CHUNK) * FFN_CHUNK

    nqk = GLA_HEADS * GLA_DK
    nv = GLA_HEADS * GLA_DV
    c_glr = 2 * nqk + 2 * nv
    c_dil = c_glr + GLA_GATE_RANK

    half = ROPE_HALF
    inv_freq = ROPE_THETA ** (-jnp.arange(0, ROPE_DIMS, 2, dtype=F32) / ROPE_DIMS)
    inv_row = jnp.concatenate([inv_freq, inv_freq, jnp.zeros((LANES - 2 * half,), F32)])[None, :]
    pos_col = positions.astype(F32).reshape(t, 1)
    cosf, sina, sinb = _rope_tables(pos_col, inv_row, min(t, 2048))

    x2d = x.reshape(t, d)
    for l in range(depth):
        wcat, wglr = _prep_win(jnp.swapaxes(w_in[l], 0, 1), c_glr, c_dil, 256)
        h, glr = _inproj(x2d, wcat, wglr, cosf, sina, sinb, min(t, 512))
        h3 = h.reshape(bsz, seq, h.shape[1])

        w2p = jnp.pad(gla_gate_w2[l], ((0, LANES - GLA_GATE_RANK), (0, 0))).astype(BF16)
        w2_view = ffn_w_out[l].reshape(-1, FFN_CAST_COLS)
        og, ffn_w1, ffn_w2 = _gla(h3, glr.reshape(bsz, seq, LANES), w2p, gla_gate_b[l][None, :],
                                  gla_norm_g[l][None, :], min(seq, 1024), ffn_w_in[l], w2_view)
        od = _dil(h3)

        x1 = _outproj(og.reshape(t, nv), od.reshape(t, DIL_HEADS * DIL_HD), w_out[l],
                      x2d, ln1_g[l][None, :], ln1_b[l][None, :], min(t, 512), alpha)

        kv = _memkv(mem.reshape(-1, d), ca_wkv[l], 1024)
        x2 = _cross(x1, ca_wq[l], kv.reshape(bsz, -1, 2 * d), ca_wo[l],
                    ln2_g[l][None, :], ln2_b[l][None, :], min(seq, 512), seq, alpha)

        cw = ffn_conv_w[l]
        cb = ffn_conv_b[l][None, :]
        nf = dff_pad // FFN_CHUNK
        conv = jnp.concatenate([cw[:, :d_ff], cb[:, :d_ff], cw[:, d_ff:], cb[:, d_ff:]], axis=0)
        conv = _pad_cols(conv, dff_pad).reshape(conv.shape[0], nf, FFN_CHUNK).transpose(1, 0, 2)
        x2d = _ffn(x2, ffn_w1, ffn_w2.reshape(ffn_w_out[l].shape), conv,
                   ln3_g[l][None, :], ln3_b[l][None, :], min(seq, 1024), seq, alpha)
    return x2d.reshape(bsz, seq, d)
```

```python
import functools

import jax
import jax.numpy as jnp
from jax import lax
from jax.experimental import pallas as pl
from jax.experimental.pallas import tpu as pltpu

F32 = jnp.float32
BF16 = jnp.bfloat16

LANES = 128
LN_EPS = 1e-5
GLA_HEADS = 4
GLA_DK = 128
GLA_DV = 256
GLA_GATE_RANK = 16
GLA_TAU = 16.0
GLA_CHUNK = 64
DIL_HD = 128
DIL_HEADS = 8
DIL_PATTERNS = ((128, 1), (512, 4), (2048, 16))
DIL_BAND = 128
DIL_UNROLL = 4
ROPE_THETA = 500000.0
ROPE_DIMS = 32
ROPE_HALF = ROPE_DIMS // 2
CA_HEADS = 4
CONV_W = 3
INPROJ_BLOCK = 1024
INPROJ_SUB = 256
WEIGHT_ROWS = 256
FFN_CHUNK = 512
FFN_SUB = 256
FFN_ROWS = 512
FFN_TAIL = 8
VMEM_LIMIT = 56 * 1024 * 1024


def _cparams(*sem):
    return pltpu.CompilerParams(dimension_semantics=sem, vmem_limit_bytes=VMEM_LIMIT)


def _dot(a, b):
    return jnp.dot(a, b, preferred_element_type=F32)


def _dot_nt(a, b):
    return lax.dot_general(a, b, (((1,), (1,)), ((), ())), preferred_element_type=F32)


def _dot_tn(a, b):
    return lax.dot_general(a, b, (((0,), (0,)), ((), ())), preferred_element_type=F32)


def _layer_norm(y, g, b):
    mu = jnp.mean(y, axis=-1, keepdims=True)
    d = y - mu
    var = jnp.mean(d * d, axis=-1, keepdims=True)
    return d * lax.rsqrt(var + LN_EPS) * g + b


def _sigmoid(x):
    return 1.0 / (1.0 + jnp.exp(-x))


def _rope_kernel(pos_ref, inv_ref, cos_ref, sa_ref, sb_ref):
    ang = pos_ref[...] * inv_ref[...]
    lane = lax.broadcasted_iota(jnp.int32, ang.shape, 1)
    c = jnp.cos(ang)
    s = jnp.sin(ang)
    cos_ref[...] = jnp.where(lane < ROPE_DIMS, c, 1.0)
    sa_ref[...] = jnp.where(lane < ROPE_HALF, -s, 0.0)
    sb_ref[...] = jnp.where(lane < ROPE_HALF, 0.0, jnp.where(lane < ROPE_DIMS, s, 0.0))


def _rope_tables(pos_col, inv_row, bs):
    t = pos_col.shape[0]
    out = jax.ShapeDtypeStruct((t, LANES), F32)
    spec = pl.BlockSpec((bs, LANES), lambda i: (i, 0))
    return pl.pallas_call(
        _rope_kernel,
        grid=(t // bs,),
        in_specs=[pl.BlockSpec((bs, 1), lambda i: (i, 0)),
                  pl.BlockSpec((1, LANES), lambda i: (0, 0))],
        out_specs=[spec, spec, spec],
        out_shape=[out, out, out],
        compiler_params=_cparams("parallel"),
        name="rope_tables",
    )(pos_col, inv_row)


def _prep_win_kernel(wt_hbm, wcat_ref, wglr_ref, buf, gbuf, sem, *, c_glr, c_dil):
    j = pl.program_id(0)
    bn = buf.shape[1]
    rank = c_dil - c_glr

    def copy(jj, slot):
        row = jj * bn
        row = pl.multiple_of(row + jnp.where(row >= c_glr, rank, 0), 8)
        return pltpu.make_async_copy(wt_hbm.at[pl.ds(row, bn), :], buf.at[slot], sem.at[slot])

    gate_copy = pltpu.make_async_copy(wt_hbm.at[pl.ds(c_glr, rank), :], gbuf.at[pl.ds(0, rank), :],
                                      sem.at[2])

    @pl.when(j == 0)
    def _():
        copy(0, 0).start()
        gate_copy.start()
        gbuf[rank:, :] = jnp.zeros((gbuf.shape[0] - rank, gbuf.shape[1]), F32)

    slot = lax.rem(j, 2)

    @pl.when(j + 1 < pl.num_programs(0))
    def _():
        copy(j + 1, 1 - slot).start()

    copy(j, slot).wait()
    wcat_ref[...] = buf[slot].T.astype(BF16)

    @pl.when(j == 0)
    def _():
        gate_copy.wait()
        wglr_ref[...] = gbuf[...].T.astype(BF16)


def _prep_win(wt, c_glr, c_dil, bn):
    ncols, d = wt.shape
    ncat = ncols - (c_dil - c_glr)
    assert c_glr % bn == 0 and ncat % bn == 0 and (c_dil - c_glr) % 8 == 0
    return pl.pallas_call(
        functools.partial(_prep_win_kernel, c_glr=c_glr, c_dil=c_dil),
        grid=(ncat // bn,),
        in_specs=[pl.BlockSpec(memory_space=pl.ANY)],
        out_specs=[pl.BlockSpec((d, bn), lambda j: (0, j)),
                   pl.BlockSpec((d, LANES), lambda j: (0, 0))],
        out_shape=[jax.ShapeDtypeStruct((d, ncat), BF16), jax.ShapeDtypeStruct((d, LANES), BF16)],
        scratch_shapes=[pltpu.VMEM((2, bn, d), F32), pltpu.VMEM((LANES, d), F32),
                        pltpu.SemaphoreType.DMA((3,))],
        compiler_params=_cparams("arbitrary"),
        name="in_projection_weights",
    )(wt)


def _rope(t, cos, sa, sb):
    return t * cos + pltpu.roll(t, LANES - ROPE_HALF, 1) * sa + pltpu.roll(t, ROPE_HALF, 1) * sb


def _inproj_kernel(x_ref, w_ref, wg_ref, cos_ref, sa_ref, sb_ref, h_ref, glr_ref, xb_ref):
    bn = INPROJ_BLOCK
    xb_ref[...] = x_ref[...].astype(BF16)
    glr_ref[...] = _dot(xb_ref[...], wg_ref[...])

    def rope(scale):
        def epilogue(acc, c0):
            cos, sa, sb = cos_ref[...], sa_ref[...], sb_ref[...]
            heads = [acc[:, j:j + LANES] for j in range(0, acc.shape[1], LANES)]
            if scale is not None:
                heads = [t * scale for t in heads]
            return jnp.concatenate([_rope(t, cos, sa, sb) for t in heads], axis=1)
        return epilogue

    def plain(acc, c0):
        return acc

    epilogues = (lambda acc, c0: acc * (GLA_DK ** -0.5) if c0 < bn // 2 else acc,
                 plain, plain, rope(DIL_HD ** -0.5), rope(None), plain)
    assert len(epilogues) * bn == w_ref.shape[1]

    for n, epilogue in enumerate(epilogues):
        for c0 in range(0, bn, INPROJ_SUB):
            cols = slice(n * bn + c0, n * bn + c0 + INPROJ_SUB)
            h_ref[:, cols] = epilogue(_dot(xb_ref[...], w_ref[:, cols]), c0).astype(h_ref.dtype)


def _inproj(x2d, wcat, wglr, cosf, sina, sinb, bm):
    t, d = x2d.shape
    ncols = wcat.shape[1]
    tab = pl.BlockSpec((bm, LANES), lambda i: (i, 0))
    return pl.pallas_call(
        _inproj_kernel,
        grid=(t // bm,),
        in_specs=[pl.BlockSpec((bm, d), lambda i: (i, 0)),
                  _resident((d, ncols)), _resident((d, LANES)), tab, tab, tab],
        out_specs=[pl.BlockSpec((bm, ncols), lambda i: (i, 0)),
                   pl.BlockSpec((bm, LANES), lambda i: (i, 0))],
        out_shape=[jax.ShapeDtypeStruct((t, ncols), BF16),
                   jax.ShapeDtypeStruct((t, LANES), F32)],
        scratch_shapes=[pltpu.VMEM((bm, d), BF16)],
        compiler_params=_cparams("parallel"),
        name="in_projection",
    )(x2d, wcat, wglr, cosf, sina, sinb)


def _split3(v):
    hi = v.astype(BF16)
    r1 = v - hi.astype(F32)
    mid = r1.astype(BF16)
    lo = (r1 - mid.astype(F32)).astype(BF16)
    return hi, mid, lo


def _side_cast_step(step, nsteps, srcs, dsts, in_bufs, out_bufs, sem):
    slot = lax.rem(step, 2)

    for k, (src, dst, ibuf, obuf) in enumerate(zip(srcs, dsts, in_bufs, out_bufs)):
        rows = ibuf.shape[1]
        nslice = src.shape[0] // rows
        assert src.shape[0] == nslice * rows and rows % 16 == 0 and 2 <= nslice <= nsteps

        def fetch(s, sl, src=src, ibuf=ibuf, rows=rows, k=k):
            at = pl.ds(pl.multiple_of(s * rows, 16), rows)
            return pltpu.make_async_copy(src.at[at, :], ibuf.at[sl], sem.at[0, k, sl])

        def write(s, sl, dst=dst, obuf=obuf, rows=rows, k=k):
            at = pl.ds(pl.multiple_of(s * rows, 16), rows)
            return pltpu.make_async_copy(obuf.at[sl], dst.at[at, :], sem.at[1, k, sl])

        @pl.when(step == 0)
        def _():
            fetch(0, 0).start()

        @pl.when(step + 1 < nslice)
        def _():
            fetch(step + 1, 1 - slot).start()

        @pl.when((step >= 2) & (step - 2 < nslice))
        def _():
            write(step - 2, slot).wait()

        @pl.when(step < nslice)
        def _():
            fetch(step, slot).wait()
            obuf[slot] = ibuf[slot].astype(BF16)
            write(step, slot).start()

        for s in range(max(nsteps - 2, 0), nsteps):
            if s < nslice:
                @pl.when(step == nsteps - 1)
                def _():
                    write(s, s % 2).wait()


def _side_cast_rows(total, nsteps):
    return next(r for r in range(16, total + 1, 16) if total % r == 0 and total // r <= nsteps)


def _gla_kernel(q_ref, k_ref, v_ref, r_ref, glr_ref, w2_ref, gb_ref, ng_ref, wa_hbm, wb_hbm,
                o_ref, wa_out, wb_out, st_ref, wa_in, wb_in, wa_cast, wb_cast, cast_sem,
                *, layer, nsteps):
    c = GLA_CHUNK
    step = (pl.program_id(0) * pl.num_programs(1) + pl.program_id(1)) * pl.num_programs(2) \
        + pl.program_id(2)
    _side_cast_step(step, nsteps, (wa_hbm.at[layer], wb_hbm.at[layer]), (wa_out, wb_out),
                    (wa_in, wb_in), (wa_cast, wb_cast), cast_sem)
    sb = q_ref.shape[1]
    grp = 4 * c

    @pl.when(pl.program_id(2) == 0)
    def _():
        st_ref[...] = jnp.zeros_like(st_ref)

    z = _dot(glr_ref[0].astype(BF16), w2_ref[...]) + gb_ref[...]
    lg = (jnp.minimum(z, 0.0) - jnp.log1p(jnp.exp(-jnp.abs(z)))) / GLA_TAU

    row = lax.broadcasted_iota(jnp.int32, (2 * grp, grp), 0)
    col = lax.broadcasted_iota(jnp.int32, (2 * grp, grp), 1)
    rr = jnp.where(row < grp, row, row - grp)
    shift = c.bit_length() - 1
    same = (rr >> shift) == (col >> shift)
    lower = jnp.where(same & (col <= rr), 1.0, 0.0)
    upper = jnp.where(same & (col > rr), 1.0, 0.0)
    lu = jnp.where(row < grp, lower, upper).astype(BF16)
    b_parts, e_parts = [], []
    for g0 in range(0, sb, grp):
        pieces = jnp.concatenate(_split3(lg[g0:g0 + grp]), axis=1)
        res = _dot(lu, pieces)
        tot = res[:, :LANES] + res[:, LANES:2 * LANES] + res[:, 2 * LANES:]
        b_parts.append(tot[:grp])
        e_parts.append(tot[grp:])
    b = jnp.concatenate(b_parts, axis=0)
    brest = jnp.concatenate(e_parts, axis=0)

    qf = q_ref[0].astype(F32)
    kf = k_ref[0].astype(F32)
    q_in = (qf * jnp.exp(b)).astype(BF16)
    k_in = (kf * jnp.exp(-b)).astype(BF16)
    k_end = (kf * jnp.exp(brest)).astype(BF16)
    v = v_ref[0]

    ci = lax.broadcasted_iota(jnp.int32, (c, c), 0)
    cj = lax.broadcasted_iota(jnp.int32, (c, c), 1)
    causal = cj <= ci

    st = st_ref[...]
    outs = []
    for i in range(sb // c):
        rows = slice(i * c, (i + 1) * c)
        a = jnp.where(causal, _dot_nt(q_in[rows], k_in[rows]), 0.0).astype(BF16)
        o = _dot(a, v[rows]) + _dot_nt(q_in[rows], st.astype(BF16))
        outs.append(o)
        decay = jnp.exp(b[i * c + c - 1:i * c + c, :])
        st = st * decay + _dot_tn(v[rows], k_end[rows])
    st_ref[...] = st

    o = jnp.concatenate(outs, axis=0)
    mu = jnp.mean(o, axis=-1, keepdims=True)
    d = o - mu
    var = jnp.mean(d * d, axis=-1, keepdims=True)
    rg = r_ref[0].astype(F32)
    y = d * lax.rsqrt(var + LN_EPS) * ng_ref[...] * (rg * _sigmoid(rg))
    o_ref[0] = y.astype(o_ref.dtype)


def _gla(h3, glr3, w2p, gb, ng, sb, side_a, side_b, layer):
    bsz, s, _ = h3.shape
    nsteps = bsz * GLA_HEADS * (s // sb)
    ra, rb_ = (_side_cast_rows(a.shape[1], nsteps) for a in (side_a, side_b))
    hbm = pl.BlockSpec(memory_space=pl.ANY)
    kb = GLA_HEADS * GLA_DK // LANES
    vb = 2 * GLA_HEADS * GLA_DK // GLA_DV
    rb = vb + GLA_HEADS
    return pl.pallas_call(
        functools.partial(_gla_kernel, layer=layer, nsteps=nsteps),
        grid=(bsz, GLA_HEADS, s // sb),
        in_specs=[pl.BlockSpec((1, sb, GLA_DK), lambda b, h, j: (b, j, h)),
                  pl.BlockSpec((1, sb, GLA_DK), lambda b, h, j: (b, j, kb + h)),
                  pl.BlockSpec((1, sb, GLA_DV), lambda b, h, j: (b, j, vb + h)),
                  pl.BlockSpec((1, sb, GLA_DV), lambda b, h, j: (b, j, rb + h)),
                  pl.BlockSpec((1, sb, LANES), lambda b, h, j: (b, j, 0)),
                  pl.BlockSpec((LANES, GLA_DK), lambda b, h, j: (0, h)),
                  pl.BlockSpec((1, GLA_DK), lambda b, h, j: (0, h)),
                  pl.BlockSpec((1, GLA_DV), lambda b, h, j: (0, h)), hbm, hbm],
        out_specs=[pl.BlockSpec((1, sb, GLA_DV), lambda b, h, j: (b, j, h)), hbm, hbm],
        out_shape=[jax.ShapeDtypeStruct((bsz, s, GLA_HEADS * GLA_DV), BF16),
                   jax.ShapeDtypeStruct(side_a.shape[1:], BF16),
                   jax.ShapeDtypeStruct(side_b.shape[1:], BF16)],
        scratch_shapes=[pltpu.VMEM((GLA_DV, GLA_DK), F32),
                        pltpu.VMEM((2, ra, side_a.shape[2]), F32), pltpu.VMEM((2, rb_, side_b.shape[2]), F32),
                        pltpu.VMEM((2, ra, side_a.shape[2]), BF16), pltpu.VMEM((2, rb_, side_b.shape[2]), BF16),
                        pltpu.SemaphoreType.DMA((2, 2, 2))],
        compiler_params=_cparams("arbitrary", "arbitrary", "arbitrary"),
        name="gla",
    )(h3, h3, h3, h3, glr3, w2p, gb, ng, side_a, side_b)


def _dil_kernel(q_ref, k_ref, v_ref, o_ref, qf, kf, vf, qg, kg, vg, qc, kc, vc, ob, db, mx,
                scb, eb):
    s = q_ref.shape[1]
    band = DIL_BAND
    unroll = DIL_UNROLL
    nblk = s // band
    qf[...] = q_ref[0].astype(F32)
    kf[...] = k_ref[0].astype(F32)
    vf[...] = v_ref[0].astype(F32)
    kc[:band, :] = jnp.zeros((band, DIL_HD), BF16)
    vc[:band, :] = jnp.zeros((band, DIL_HD), BF16)

    qi = lax.broadcasted_iota(jnp.int32, (band, 2 * band), 0)
    kj = lax.broadcasted_iota(jnp.int32, (band, 2 * band), 1)
    allowed = (kj >= qi) & (kj <= qi + band)
    bias = jnp.where(allowed, 0.0, -jnp.inf).astype(F32)
    bias0 = jnp.where(allowed & (kj >= band), 0.0, -jnp.inf).astype(F32)
    ones = jnp.ones((2 * band, LANES), BF16)

    for p, (window, dil) in enumerate(DIL_PATTERNS):
        assert window // dil == band
        cls = s // dil
        nb = cls // band
        span = band * dil
        assert nblk % unroll == 0 and (nb % unroll == 0 or unroll % nb == 0)

        if dil == 1:
            qc[...] = q_ref[0]
            kc[band:, :] = k_ref[0]
            vc[band:, :] = v_ref[0]
        else:
            prev = DIL_PATTERNS[p - 1][1]
            step = dil // prev
            assert step * prev == dil and step in (2, 4)
            keep = p + 1 < len(DIL_PATTERNS)
            srcs, dsts = ((qf, kf, vf), (qg, kg, vg)) if p % 2 == 1 else ((qg, kg, vg), (qf, kf, vf))
            for r in range(dil):
                rows = pl.ds((r % prev) * (s // prev) + r // prev, cls, stride=step)
                for src, dst, cm, off in zip(srcs, dsts, (qc, kc, vc), (0, band, band)):
                    x = src[rows, :]
                    if keep:
                        dst[r * cls:(r + 1) * cls, :] = x
                    cm[off + r * cls:off + (r + 1) * cls, :] = x.astype(BF16)

        def out_rows(g, lo=0, cnt=band, dil=dil, nb=nb, span=span):
            start = g // nb + (g % nb) * span + lo * dil
            return pl.ds(start, cnt) if dil == 1 else pl.ds(start, cnt, stride=dil)

        def scores(t, slot):
            for u in range(unroll):
                g = t * unroll + u
                k2 = kc[g * band:(g + 2) * band, :]
                scb[slot, u * band:(u + 1) * band, :] = _dot_nt(qc[g * band:(g + 1) * band, :], k2)

        def softmax(t, slot, p=p, nb=nb, out_rows=out_rows):
            for u in range(unroll):
                g = t * unroll + u
                bb = bias0 if g % nb == 0 else bias
                rows = slice(u * band, (u + 1) * band)
                m = jnp.max(scb[slot, rows, :] + bb, axis=-1, keepdims=True)
                mx[p, out_rows(g), :] = jnp.broadcast_to(m, (band, LANES))
                for half in range(2):
                    cols = slice(half * band, (half + 1) * band)
                    eb[slot, rows, cols] = jnp.exp(scb[slot, rows, cols] + bb[:, cols] - m).astype(BF16)

        def values(t, slot, p=p, out_rows=out_rows):
            for u in range(unroll):
                g = t * unroll + u
                v2 = vc[g * band:(g + 2) * band, :]
                oe = _dot(eb[slot, u * band:(u + 1) * band, :],
                          jnp.concatenate([v2, ones], axis=1))
                ob[p, out_rows(g), :] = oe[:, :DIL_HD]
                db[p, out_rows(g), :] = oe[:, DIL_HD:]

        ngrp = nblk // unroll
        for t in range(ngrp + 2):
            if t < ngrp:
                scores(t, t % 2)
            if 1 <= t <= ngrp:
                softmax(t - 1, (t - 1) % 2)
            if t >= 2:
                values(t - 2, t % 2)

    mb = 512

    def merge(i, carry):
        rs = pl.ds(pl.multiple_of(i * mb, mb), mb)
        m0, m1, m2 = mx[0, rs, :], mx[1, rs, :], mx[2, rs, :]
        m = jnp.maximum(jnp.maximum(m0, m1), m2)
        e0, e1, e2 = jnp.exp(m0 - m), jnp.exp(m1 - m), jnp.exp(m2 - m)
        num = e0 * ob[0, rs, :] + e1 * ob[1, rs, :] + e2 * ob[2, rs, :]
        den = e0 * db[0, rs, :] + e1 * db[1, rs, :] + e2 * db[2, rs, :]
        o_ref[0, rs, :] = (num / den).astype(o_ref.dtype)
        return carry

    lax.fori_loop(0, s // mb, merge, 0)


def _dil(h3):
    bsz, s, _ = h3.shape
    qb = (2 * GLA_HEADS * GLA_DK + 2 * GLA_HEADS * GLA_DV) // DIL_HD
    kb = qb + DIL_HEADS
    vb = kb + DIL_HEADS
    npat = len(DIL_PATTERNS)
    return pl.pallas_call(
        _dil_kernel,
        grid=(bsz, DIL_HEADS),
        in_specs=[pl.BlockSpec((1, s, DIL_HD), lambda b, h: (b, 0, qb + h)),
                  pl.BlockSpec((1, s, DIL_HD), lambda b, h: (b, 0, kb + h)),
                  pl.BlockSpec((1, s, DIL_HD), lambda b, h: (b, 0, vb + h))],
        out_specs=pl.BlockSpec((1, s, DIL_HD), lambda b, h: (b, 0, h)),
        out_shape=jax.ShapeDtypeStruct((bsz, s, DIL_HEADS * DIL_HD), BF16),
        scratch_shapes=[pltpu.VMEM((s, DIL_HD), F32)] * 6 + [
                        pltpu.VMEM((s, DIL_HD), BF16), pltpu.VMEM((s + DIL_BAND, DIL_HD), BF16),
                        pltpu.VMEM((s + DIL_BAND, DIL_HD), BF16),
                        pltpu.VMEM((npat, s, DIL_HD), F32), pltpu.VMEM((npat, s, LANES), F32),
                        pltpu.VMEM((npat, s, LANES), F32),
                        pltpu.VMEM((2, DIL_UNROLL * DIL_BAND, 2 * DIL_BAND), F32),
                        pltpu.VMEM((2, DIL_UNROLL * DIL_BAND, 2 * DIL_BAND), BF16)],
        compiler_params=_cparams("parallel", "parallel"),
        name="dilated_attention",
    )(h3, h3, h3)


def _resident(shape):
    return pl.BlockSpec(shape, lambda *_: (0,) * len(shape), pipeline_mode=pl.Buffered(1))


def _load_weight_bf16(w_hbm, dst_ref, stage_ref, sem):
    rows = stage_ref.shape[1]
    nchunk = w_hbm.shape[0] // rows
    copies = [pltpu.make_async_copy(w_hbm.at[pl.ds(c * rows, rows), :], stage_ref.at[c % 2],
                                    sem.at[c % 2]) for c in range(nchunk)]
    copies[0].start()
    for c in range(nchunk):
        if c + 1 < nchunk:
            copies[c + 1].start()
        copies[c].wait()
        dst_ref[c * rows:(c + 1) * rows, :] = stage_ref[c % 2].astype(BF16)


_WEIGHT_STAGE = [pltpu.VMEM((2, WEIGHT_ROWS, 2048), F32), pltpu.SemaphoreType.DMA((2,))]


def _outproj_kernel(og_ref, od_ref, x_ref, g_ref, b_ref, w_hbm, o_ref, w_ref, stage_ref, sem,
                    *, alpha):
    @pl.when(pl.program_id(0) == 0)
    def _():
        _load_weight_bf16(w_hbm, w_ref, stage_ref, sem)

    ka = og_ref.shape[1]
    acc = _dot(og_ref[...], w_ref[:ka, :]) + _dot(od_ref[...], w_ref[ka:, :])
    o_ref[...] = _layer_norm(alpha * x_ref[...] + acc, g_ref[...], b_ref[...])


def _outproj(og, od, w, x2d, g, b, bm, alpha):
    t, d = x2d.shape
    ka, kb = og.shape[1], od.shape[1]
    assert w.shape == (ka + kb, d) and d == _WEIGHT_STAGE[0].shape[2]
    return pl.pallas_call(
        functools.partial(_outproj_kernel, alpha=alpha),
        grid=(t // bm,),
        in_specs=[pl.BlockSpec((bm, ka), lambda i: (i, 0)),
                  pl.BlockSpec((bm, kb), lambda i: (i, 0)),
                  pl.BlockSpec((bm, d), lambda i: (i, 0)),
                  _resident((1, d)), _resident((1, d)),
                  pl.BlockSpec(memory_space=pl.ANY)],
        out_specs=pl.BlockSpec((bm, d), lambda i: (i, 0)),
        out_shape=jax.ShapeDtypeStruct((t, d), F32),
        scratch_shapes=[pltpu.VMEM((ka + kb, d), BF16)] + _WEIGHT_STAGE,
        compiler_params=_cparams("arbitrary"),
        name="out_projection_ln1",
    )(og, od, x2d, g, b, w)


def _memkv_kernel(m_ref, w_ref, o_ref):
    o_ref[...] = _dot(m_ref[...].astype(BF16), w_ref[...].astype(BF16)).astype(o_ref.dtype)


def _memkv(mem2d, wkv, bn):
    t, d = mem2d.shape
    n = wkv.shape[1]
    return pl.pallas_call(
        _memkv_kernel,
        grid=(n // bn,),
        in_specs=[pl.BlockSpec((t, d), lambda j: (0, 0)),
                  pl.BlockSpec((d, bn), lambda j: (0, j))],
        out_specs=pl.BlockSpec((t, bn), lambda j: (0, j)),
        out_shape=jax.ShapeDtypeStruct((t, n), BF16),
        compiler_params=_cparams("parallel"),
        name="memory_kv_projection",
    )(mem2d, wkv)


def _cross_kernel(x_ref, kv_ref, g_ref, b_ref, wq_hbm, wo_hbm, o_ref,
                  wq_ref, wo_ref, oc_ref, stage_ref, sem, *, alpha):
    @pl.when(pl.program_id(0) == 0)
    def _():
        _load_weight_bf16(wq_hbm, wq_ref, stage_ref, sem)
        _load_weight_bf16(wo_hbm, wo_ref, stage_ref, sem)

    d = x_ref.shape[1]
    hd = d // CA_HEADS
    x = x_ref[...]
    q = _dot(x.astype(BF16), wq_ref[...]).astype(BF16)
    for h in range(CA_HEADS):
        cols = slice(h * hd, (h + 1) * hd)
        mk = kv_ref[0, :, cols]
        mv = kv_ref[0, :, d + h * hd:d + (h + 1) * hd]
        sc = _dot_nt(q[:, cols], mk) * (hd ** -0.5)
        m = jnp.max(sc, axis=-1, keepdims=True)
        e = jnp.exp(sc - m)
        p = e / jnp.sum(e, axis=-1, keepdims=True)
        oc_ref[:, cols] = _dot(p.astype(BF16), mv).astype(BF16)
    y = alpha * x + _dot(oc_ref[...], wo_ref[...])
    o_ref[...] = _layer_norm(y, g_ref[...], b_ref[...])


def _cross(x1, wq, kv3, wo, g, b, bm, seq, alpha):
    t, d = x1.shape
    m = kv3.shape[1]
    per = seq // bm
    hbm = pl.BlockSpec(memory_space=pl.ANY)
    assert d == _WEIGHT_STAGE[0].shape[2]
    return pl.pallas_call(
        functools.partial(_cross_kernel, alpha=alpha),
        grid=(t // bm,),
        in_specs=[pl.BlockSpec((bm, d), lambda i: (i, 0)),
                  pl.BlockSpec((1, m, 2 * d), lambda i: (i // per, 0, 0)),
                  _resident((1, d)), _resident((1, d)), hbm, hbm],
        out_specs=pl.BlockSpec((bm, d), lambda i: (i, 0)),
        out_shape=jax.ShapeDtypeStruct((t, d), F32),
        scratch_shapes=[pltpu.VMEM((d, d), BF16), pltpu.VMEM((d, d), BF16),
                        pltpu.VMEM((bm, d), BF16)] + _WEIGHT_STAGE,
        compiler_params=_cparams("arbitrary"),
        name="cross_attention_ln2",
    )(x1, kv3, g, b, wq, wo)


def _causal_conv(u_ref, cw, cb, r0, n):
    h = FFN_TAIL + r0
    y = cb + cw[0:1, :] * u_ref[h - 2:h - 2 + n, :]
    y = y + cw[1:2, :] * u_ref[h - 1:h - 1 + n, :]
    return y + cw[2:3, :] * u_ref[h:h + n, :]


def _ffn_kernel(cp_ref, g_ref, b_ref, x_hbm, w1_hbm, w2_hbm, o_ref,
                x_buf, xb_ref, act0_ref, act1_ref, us_ref, tail_ref, wg_buf, wu_buf, w2_buf,
                sem, x_sem, *, alpha, per):
    i = pl.program_id(0)
    d_ff = w2_hbm.shape[0]
    bm = x_buf.shape[0]
    bf = wg_buf.shape[2]
    nf = pl.cdiv(d_ff, bf)
    last_w = d_ff - (nf - 1) * bf
    assert nf % 2 == 1 and nf >= 3
    assert bf % FFN_SUB == 0 and last_w % LANES == 0 and bm % FFN_ROWS == 0
    c_first = i * nf
    acts = (act0_ref, act1_ref)
    nslab = bf // LANES
    half = o_ref.shape[1] // 2

    def width(f):
        return last_w if isinstance(f, int) and f == nf - 1 else bf

    def up_copies(f, slot):
        w = width(f)
        col = f * bf if isinstance(f, int) else pl.multiple_of(f * bf, bf)
        return (pltpu.make_async_copy(w1_hbm.at[:, pl.ds(col, w)],
                                      wg_buf.at[slot, :, pl.ds(0, w)], sem.at[0, slot]),
                pltpu.make_async_copy(w1_hbm.at[:, pl.ds(d_ff + col, w)],
                                      wu_buf.at[slot, :, pl.ds(0, w)], sem.at[1, slot]))

    def down_copy(f, slot):
        w = width(f)
        row = f * bf if isinstance(f, int) else pl.multiple_of(f * bf, bf)
        return pltpu.make_async_copy(w2_hbm.at[pl.ds(row, w), :],
                                     w2_buf.at[slot, pl.ds(0, w), :], sem.at[2, slot])

    def x_copy(step):
        row = pl.multiple_of(step * bm, bm)
        return pltpu.make_async_copy(x_hbm.at[pl.ds(row, bm), :], x_buf, x_sem.at[0])

    def region_copies(f):
        slot = lax.rem(c_first + f, 2)
        for cp in up_copies(f, slot):
            cp.wait()
        if isinstance(f, int) and f == 0:
            @pl.when(i + 1 < pl.num_programs(0))
            def _():
                x_copy(i + 1).start()
        if not (isinstance(f, int) and f == 0):
            down_copy(f - 1, 1 - slot).wait()
        if isinstance(f, int) and f == nf - 1:
            @pl.when(i + 1 < pl.num_programs(0))
            def _():
                for cp in up_copies(0, 1 - slot):
                    cp.start()
        else:
            for cp in up_copies(f + 1, 1 - slot):
                cp.start()
        down_copy(f, slot).start()
        return slot

    def up_matmul(slot, c0, wcols, r0):
        xr = xb_ref[r0:r0 + FFN_ROWS, :]
        rows = slice(FFN_TAIL + r0, FFN_TAIL + r0 + FFN_ROWS)
        for base, w_buf in ((0, wg_buf), (nslab, wu_buf)):
            res = _dot(xr, w_buf[slot, :, c0:c0 + wcols])
            for k in range(wcols // LANES):
                us_ref[base + c0 // LANES + k, rows, :] = res[:, k * LANES:(k + 1) * LANES]

    def activate(cv, act_ref, c0, wcols, r0):
        for s in range(c0 // LANES, (c0 + wcols) // LANES):
            cols = slice(s * LANES, (s + 1) * LANES)
            gate = _causal_conv(us_ref.at[s], cv[0:3, cols], cv[3:4, cols], r0, FFN_ROWS)
            up = _causal_conv(us_ref.at[nslab + s], cv[4:7, cols], cv[7:8, cols], r0, FFN_ROWS)
            act_ref[r0:r0 + FFN_ROWS, cols] = (gate * _sigmoid(gate) * up).astype(BF16)

    def down_matmul(slot, act_ref, w, n0):
        o_ref[:, n0:n0 + half] += _dot(act_ref[:, :w], w2_buf[slot, :w, n0:n0 + half])

    def region(f, act_slot):
        slot = region_copies(f)
        w = width(f)
        cv = cp_ref[f]
        act_ref, prev_ref = acts[act_slot], acts[1 - act_slot]
        slabs = [s for base in (0, nslab) for s in range(base, base + w // LANES)]
        for s in slabs:
            us_ref[s, :FFN_TAIL, :] = tail_ref[f, :, s * LANES:(s + 1) * LANES]
        units = [(c0, min(FFN_SUB, w - c0), r0) for c0 in range(0, w, FFN_SUB)
                 for r0 in range(0, bm, FFN_ROWS)]
        for unit in units:
            up_matmul(slot, *unit)
        if not (isinstance(f, int) and f == 0):
            for n0 in (0, half):
                down_matmul(1 - slot, prev_ref, width(f - 1) if isinstance(f, int) else bf, n0)
        for unit in units:
            activate(cv, act_ref, *unit)
        for s in slabs:
            tail_ref[f, :, s * LANES:(s + 1) * LANES] = us_ref[s, bm:bm + FFN_TAIL, :]

    @pl.when(i == 0)
    def _():
        x_copy(0).start()
        for cp in up_copies(0, 0):
            cp.start()

    @pl.when(i % per == 0)
    def _():
        tail_ref[...] = jnp.zeros_like(tail_ref)

    x_copy(i).wait()
    xb_ref[...] = x_buf[...].astype(BF16)
    o_ref[...] = alpha * x_buf[...]

    region(0, 0)

    def pair(j, carry):
        region(2 * j + 1, 1)
        region(2 * j + 2, 0)
        return carry

    lax.fori_loop(0, (nf - 3) // 2, pair, 0)
    region(nf - 2, 1)
    region(nf - 1, 0)

    last = lax.rem(c_first + nf - 1, 2)
    down_copy(nf - 1, last).wait()
    for n0 in (0, half):
        down_matmul(last, acts[0], last_w, n0)
    o_ref[...] = _layer_norm(o_ref[...], g_ref[...], b_ref[...])


def _ffn(x2, w1, w2, cp, g, b, bm, seq, alpha):
    t, d = x2.shape
    bf = FFN_CHUNK
    nf = cp.shape[0]
    hbm = pl.BlockSpec(memory_space=pl.ANY)
    return pl.pallas_call(
        functools.partial(_ffn_kernel, alpha=alpha, per=seq // bm),
        grid=(t // bm,),
        in_specs=[_resident(cp.shape), _resident((1, d)), _resident((1, d)), hbm, hbm, hbm],
        out_specs=pl.BlockSpec((bm, d), lambda i: (i, 0)),
        out_shape=jax.ShapeDtypeStruct((t, d), F32),
        scratch_shapes=[pltpu.VMEM((bm, d), F32), pltpu.VMEM((bm, d), BF16),
                        pltpu.VMEM((bm, bf), BF16), pltpu.VMEM((bm, bf), BF16),
                        pltpu.VMEM((2 * bf // LANES, FFN_TAIL + bm, LANES), F32),
                        pltpu.VMEM((nf, FFN_TAIL, 2 * bf), F32),
                        pltpu.VMEM((2, d, bf), BF16), pltpu.VMEM((2, d, bf), BF16),
                        pltpu.VMEM((2, bf, d), BF16), pltpu.SemaphoreType.DMA((3, 2)),
                        pltpu.SemaphoreType.DMA((1,))],
        compiler_params=_cparams("arbitrary"),
        name="conv_ffn_ln3",
    )(cp, g, b, x2, w1, w2)


def _pad_cols(a, n):
    return jnp.pad(a, ((0, 0), (0, n - a.shape[1])))


def kernel(x, mem, positions, w_in, gla_gate_w2, gla_gate_b, gla_norm_g, w_out, ln1_g, ln1_b,
           ca_wq, ca_wkv, ca_wo, ln2_g, ln2_b, ffn_w_in, ffn_conv_w, ffn_conv_b, ffn_w_out,
           ln3_g, ln3_b):
    bsz, seq, d = x.shape
    depth = w_in.shape[0]
    t = bsz * seq
    alpha = (2.0 * depth) ** 0.25
    d_ff = ffn_w_out.shape[1]
    dff_pad = -(-d_ff // FFN_CHUNK) * FFN_CHUNK

    nqk = GLA_HEADS * GLA_DK
    nv = GLA_HEADS * GLA_DV
    c_glr = 2 * nqk + 2 * nv
    c_dil = c_glr + GLA_GATE_RANK

    half = ROPE_HALF
    inv_freq = ROPE_THETA ** (-jnp.arange(0, ROPE_DIMS, 2, dtype=F32) / ROPE_DIMS)
    inv_row = jnp.concatenate([inv_freq, inv_freq, jnp.zeros((LANES - 2 * half,), F32)])[None, :]
    pos_col = positions.astype(F32).reshape(t, 1)
    cosf, sina, sinb = _rope_tables(pos_col, inv_row, min(t, 2048))

    x2d = x.reshape(t, d)
    for l in range(depth):
        wcat, wglr = _prep_win(jnp.swapaxes(w_in[l], 0, 1), c_glr, c_dil, 256)
        h, glr = _inproj(x2d, wcat, wglr, cosf, sina, sinb, min(t, 512))
        h3 = h.reshape(bsz, seq, h.shape[1])

        w2p = jnp.pad(gla_gate_w2[l], ((0, LANES - GLA_GATE_RANK), (0, 0))).astype(BF16)
        og, ffn_w1, ffn_w2 = _gla(h3, glr.reshape(bsz, seq, LANES), w2p, gla_gate_b[l][None, :],
                                  gla_norm_g[l][None, :], min(seq, 1024), ffn_w_in, ffn_w_out, l)
        od = _dil(h3)

        x1 = _outproj(og.reshape(t, nv), od.reshape(t, DIL_HEADS * DIL_HD), w_out[l],
                      x2d, ln1_g[l][None, :], ln1_b[l][None, :], min(t, 512), alpha)

        kv = _memkv(mem.reshape(-1, d), ca_wkv[l], 1024)
        x2 = _cross(x1, ca_wq[l], kv.reshape(bsz, -1, 2 * d), ca_wo[l],
                    ln2_g[l][None, :], ln2_b[l][None, :], min(seq, 512), seq, alpha)

        cw = ffn_conv_w[l]
        cb = ffn_conv_b[l][None, :]
        nf = dff_pad // FFN_CHUNK
        conv = jnp.concatenate([cw[:, :d_ff], cb[:, :d_ff], cw[:, d_ff:], cb[:, d_ff:]], axis=0)
        conv = _pad_cols(conv, dff_pad).reshape(conv.shape[0], nf, FFN_CHUNK).transpose(1, 0, 2)
        x2d = _ffn(x2, ffn_w1, ffn_w2, conv,
                   ln3_g[l][None, :], ln3_b[l][None, :], min(seq, 1024), seq, alpha)
    return x2d.reshape(bsz, seq, d)
```

```python
import functools

import jax
import jax.numpy as jnp
from jax import lax
from jax.experimental import pallas as pl
from jax.experimental.pallas import tpu as pltpu

F32 = jnp.float32
BF16 = jnp.bfloat16

LANES = 128
LN_EPS = 1e-5
LOG2E = 1.4426950408889634
GLA_HEADS = 4
GLA_DK = 128
GLA_DV = 256
GLA_GATE_RANK = 16
GLA_TAU = 16.0
GLA_CHUNK = 64
DIL_HD = 128
DIL_HEADS = 8
DIL_PATTERNS = ((128, 1), (512, 4), (2048, 16))
DIL_BAND = 128
DIL_UNROLL = 4
ROPE_THETA = 500000.0
ROPE_DIMS = 32
ROPE_HALF = ROPE_DIMS // 2
CA_HEADS = 4
CONV_W = 3
INPROJ_BLOCK = 1024
INPROJ_SUB = 256
WEIGHT_ROWS = 256
FFN_CHUNK = 512
FFN_SUB = 256
FFN_ROWS = 512
FFN_TAIL = 8
VMEM_LIMIT = 56 * 1024 * 1024


def _cparams(*sem):
    return pltpu.CompilerParams(dimension_semantics=sem, vmem_limit_bytes=VMEM_LIMIT)


def _dot(a, b):
    return jnp.dot(a, b, preferred_element_type=F32)


def _dot_nt(a, b):
    return lax.dot_general(a, b, (((1,), (1,)), ((), ())), preferred_element_type=F32)


def _dot_tn(a, b):
    return lax.dot_general(a, b, (((0,), (0,)), ((), ())), preferred_element_type=F32)


def _layer_norm(y, g, b):
    mu = jnp.mean(y, axis=-1, keepdims=True)
    d = y - mu
    var = jnp.mean(d * d, axis=-1, keepdims=True)
    return d * lax.rsqrt(var + LN_EPS) * g + b


def _sigmoid(x):
    return 1.0 / (1.0 + jnp.exp2(x * -LOG2E))


def _rope_kernel(pos_ref, inv_ref, cos_ref, sa_ref, sb_ref):
    ang = pos_ref[...] * inv_ref[...]
    lane = lax.broadcasted_iota(jnp.int32, ang.shape, 1)
    c = jnp.cos(ang)
    s = jnp.sin(ang)
    cos_ref[...] = jnp.where(lane < ROPE_DIMS, c, 1.0)
    sa_ref[...] = jnp.where(lane < ROPE_HALF, -s, 0.0)
    sb_ref[...] = jnp.where(lane < ROPE_HALF, 0.0, jnp.where(lane < ROPE_DIMS, s, 0.0))


def _rope_tables(pos_col, inv_row, bs):
    t = pos_col.shape[0]
    out = jax.ShapeDtypeStruct((t, LANES), F32)
    spec = pl.BlockSpec((bs, LANES), lambda i: (i, 0))
    return pl.pallas_call(
        _rope_kernel,
        grid=(t // bs,),
        in_specs=[pl.BlockSpec((bs, 1), lambda i: (i, 0)),
                  pl.BlockSpec((1, LANES), lambda i: (0, 0))],
        out_specs=[spec, spec, spec],
        out_shape=[out, out, out],
        compiler_params=_cparams("parallel"),
        name="rope_tables",
    )(pos_col, inv_row)


def _prep_win_kernel(wt_hbm, wcat_ref, wglr_ref, buf, gbuf, sem, *, c_glr, c_dil):
    j = pl.program_id(0)
    bn = buf.shape[1]
    rank = c_dil - c_glr

    def copy(jj, slot):
        row = jj * bn
        row = pl.multiple_of(row + jnp.where(row >= c_glr, rank, 0), 8)
        return pltpu.make_async_copy(wt_hbm.at[pl.ds(row, bn), :], buf.at[slot], sem.at[slot])

    gate_copy = pltpu.make_async_copy(wt_hbm.at[pl.ds(c_glr, rank), :], gbuf.at[pl.ds(0, rank), :],
                                      sem.at[2])

    @pl.when(j == 0)
    def _():
        copy(0, 0).start()
        gate_copy.start()
        gbuf[rank:, :] = jnp.zeros((gbuf.shape[0] - rank, gbuf.shape[1]), F32)

    slot = lax.rem(j, 2)

    @pl.when(j + 1 < pl.num_programs(0))
    def _():
        copy(j + 1, 1 - slot).start()

    copy(j, slot).wait()
    wcat_ref[...] = buf[slot].T.astype(BF16)

    @pl.when(j == 0)
    def _():
        gate_copy.wait()
        wglr_ref[...] = gbuf[...].T.astype(BF16)


def _prep_win(wt, c_glr, c_dil, bn):
    ncols, d = wt.shape
    ncat = ncols - (c_dil - c_glr)
    assert c_glr % bn == 0 and ncat % bn == 0 and (c_dil - c_glr) % 8 == 0
    return pl.pallas_call(
        functools.partial(_prep_win_kernel, c_glr=c_glr, c_dil=c_dil),
        grid=(ncat // bn,),
        in_specs=[pl.BlockSpec(memory_space=pl.ANY)],
        out_specs=[pl.BlockSpec((d, bn), lambda j: (0, j)),
                   pl.BlockSpec((d, LANES), lambda j: (0, 0))],
        out_shape=[jax.ShapeDtypeStruct((d, ncat), BF16), jax.ShapeDtypeStruct((d, LANES), BF16)],
        scratch_shapes=[pltpu.VMEM((2, bn, d), F32), pltpu.VMEM((LANES, d), F32),
                        pltpu.SemaphoreType.DMA((3,))],
        compiler_params=_cparams("arbitrary"),
        name="in_projection_weights",
    )(wt)


def _rope(t, cos, sa, sb):
    return t * cos + pltpu.roll(t, LANES - ROPE_HALF, 1) * sa + pltpu.roll(t, ROPE_HALF, 1) * sb


def _inproj_kernel(x_ref, w_ref, wg_ref, cos_ref, sa_ref, sb_ref, h_ref, glr_ref, xb_ref):
    bn = INPROJ_BLOCK
    xb_ref[...] = x_ref[...].astype(BF16)
    glr_ref[...] = _dot(xb_ref[...], wg_ref[...])

    def rope(scale):
        def epilogue(acc, c0):
            cos, sa, sb = cos_ref[...], sa_ref[...], sb_ref[...]
            heads = [acc[:, j:j + LANES] for j in range(0, acc.shape[1], LANES)]
            if scale is not None:
                heads = [t * scale for t in heads]
            return jnp.concatenate([_rope(t, cos, sa, sb) for t in heads], axis=1)
        return epilogue

    def plain(acc, c0):
        return acc

    epilogues = (lambda acc, c0: acc * (GLA_DK ** -0.5) if c0 < bn // 2 else acc,
                 plain, plain, rope(DIL_HD ** -0.5), rope(None), plain)
    assert len(epilogues) * bn == w_ref.shape[1]

    for n, epilogue in enumerate(epilogues):
        for c0 in range(0, bn, INPROJ_SUB):
            cols = slice(n * bn + c0, n * bn + c0 + INPROJ_SUB)
            h_ref[:, cols] = epilogue(_dot(xb_ref[...], w_ref[:, cols]), c0).astype(h_ref.dtype)


def _inproj(x2d, wcat, wglr, cosf, sina, sinb, bm):
    t, d = x2d.shape
    ncols = wcat.shape[1]
    tab = pl.BlockSpec((bm, LANES), lambda i: (i, 0))
    return pl.pallas_call(
        _inproj_kernel,
        grid=(t // bm,),
        in_specs=[pl.BlockSpec((bm, d), lambda i: (i, 0)),
                  _resident((d, ncols)), _resident((d, LANES)), tab, tab, tab],
        out_specs=[pl.BlockSpec((bm, ncols), lambda i: (i, 0)),
                   pl.BlockSpec((bm, LANES), lambda i: (i, 0))],
        out_shape=[jax.ShapeDtypeStruct((t, ncols), BF16),
                   jax.ShapeDtypeStruct((t, LANES), F32)],
        scratch_shapes=[pltpu.VMEM((bm, d), BF16)],
        compiler_params=_cparams("parallel"),
        name="in_projection",
    )(x2d, wcat, wglr, cosf, sina, sinb)


def _split3(v):
    hi = v.astype(BF16)
    r1 = v - hi.astype(F32)
    mid = r1.astype(BF16)
    lo = (r1 - mid.astype(F32)).astype(BF16)
    return hi, mid, lo


def _side_cast_step(step, nsteps, srcs, dsts, in_bufs, out_bufs, sem):
    slot = lax.rem(step, 2)

    for k, (src, dst, ibuf, obuf) in enumerate(zip(srcs, dsts, in_bufs, out_bufs)):
        rows = ibuf.shape[1]
        nslice = src.shape[0] // rows
        assert src.shape[0] == nslice * rows and rows % 16 == 0 and 2 <= nslice <= nsteps

        def fetch(s, sl, src=src, ibuf=ibuf, rows=rows, k=k):
            at = pl.ds(pl.multiple_of(s * rows, 16), rows)
            return pltpu.make_async_copy(src.at[at, :], ibuf.at[sl], sem.at[0, k, sl])

        def write(s, sl, dst=dst, obuf=obuf, rows=rows, k=k):
            at = pl.ds(pl.multiple_of(s * rows, 16), rows)
            return pltpu.make_async_copy(obuf.at[sl], dst.at[at, :], sem.at[1, k, sl])

        @pl.when(step == 0)
        def _():
            fetch(0, 0).start()

        @pl.when(step + 1 < nslice)
        def _():
            fetch(step + 1, 1 - slot).start()

        @pl.when((step >= 2) & (step - 2 < nslice))
        def _():
            write(step - 2, slot).wait()

        @pl.when(step < nslice)
        def _():
            fetch(step, slot).wait()
            obuf[slot] = ibuf[slot].astype(BF16)
            write(step, slot).start()

        for s in range(max(nsteps - 2, 0), nsteps):
            if s < nslice:
                @pl.when(step == nsteps - 1)
                def _():
                    write(s, s % 2).wait()


def _side_cast_rows(total, nsteps):
    return next(r for r in range(16, total + 1, 16) if total % r == 0 and total // r <= nsteps)


def _gla_kernel(q_ref, k_ref, v_ref, r_ref, glr_ref, w2_ref, gb_ref, ng_ref, wa_hbm, wb_hbm,
                o_ref, wa_out, wb_out, st_ref, wa_in, wb_in, wa_cast, wb_cast, cast_sem,
                *, layer, nsteps):
    c = GLA_CHUNK
    step = (pl.program_id(0) * pl.num_programs(1) + pl.program_id(1)) * pl.num_programs(2) \
        + pl.program_id(2)
    _side_cast_step(step, nsteps, (wa_hbm.at[layer], wb_hbm.at[layer]), (wa_out, wb_out),
                    (wa_in, wb_in), (wa_cast, wb_cast), cast_sem)
    sb = q_ref.shape[1]
    grp = 4 * c

    @pl.when(pl.program_id(2) == 0)
    def _():
        st_ref[...] = jnp.zeros_like(st_ref)

    z = _dot(glr_ref[0].astype(BF16), w2_ref[...]) + gb_ref[...]
    lg = (jnp.minimum(z, 0.0) - jnp.log(1.0 + jnp.exp(-jnp.abs(z)))) / GLA_TAU

    row = lax.broadcasted_iota(jnp.int32, (grp, grp), 0)
    col = lax.broadcasted_iota(jnp.int32, (grp, grp), 1)
    shift = c.bit_length() - 1
    tril = jnp.where(((row >> shift) == (col >> shift)) & (col <= row), 1.0, 0.0).astype(BF16)
    b_parts = []
    for g0 in range(0, sb, grp):
        pieces = jnp.concatenate(_split3(lg[g0:g0 + grp]), axis=1)
        res = _dot(tril, pieces)
        b_parts.append(res[:, :LANES] + res[:, LANES:2 * LANES] + res[:, 2 * LANES:])
    b = jnp.concatenate(b_parts, axis=0)

    qf = q_ref[0].astype(F32)
    kf = k_ref[0].astype(F32)
    q_in = (qf * jnp.exp(b)).astype(BF16)
    k_in = (kf * jnp.exp(-b)).astype(BF16)
    v = v_ref[0]

    ci = lax.broadcasted_iota(jnp.int32, (c, c), 0)
    cj = lax.broadcasted_iota(jnp.int32, (c, c), 1)
    causal = cj <= ci

    st = st_ref[...]
    outs = []
    for i in range(sb // c):
        rows = slice(i * c, (i + 1) * c)
        a = jnp.where(causal, _dot_nt(q_in[rows], k_in[rows]), 0.0).astype(BF16)
        o = _dot(a, v[rows]) + _dot_nt(q_in[rows], st.astype(BF16))
        outs.append(o)
        decay = jnp.exp(b[i * c + c - 1:i * c + c, :])
        st = (st + _dot_tn(v[rows], k_in[rows])) * decay
    st_ref[...] = st

    o = jnp.concatenate(outs, axis=0)
    mu = jnp.mean(o, axis=-1, keepdims=True)
    d = o - mu
    var = jnp.mean(d * d, axis=-1, keepdims=True)
    rg = r_ref[0].astype(F32)
    y = d * lax.rsqrt(var + LN_EPS) * ng_ref[...] * (rg * _sigmoid(rg))
    o_ref[0] = y.astype(o_ref.dtype)


def _gla(h3, glr3, w2p, gb, ng, sb, side_a, side_b, layer):
    bsz, s, _ = h3.shape
    nsteps = bsz * GLA_HEADS * (s // sb)
    ra, rb_ = (_side_cast_rows(a.shape[1], nsteps) for a in (side_a, side_b))
    hbm = pl.BlockSpec(memory_space=pl.ANY)
    kb = GLA_HEADS * GLA_DK // LANES
    vb = 2 * GLA_HEADS * GLA_DK // GLA_DV
    rb = vb + GLA_HEADS
    return pl.pallas_call(
        functools.partial(_gla_kernel, layer=layer, nsteps=nsteps),
        grid=(bsz, GLA_HEADS, s // sb),
        in_specs=[pl.BlockSpec((1, sb, GLA_DK), lambda b, h, j: (b, j, h)),
                  pl.BlockSpec((1, sb, GLA_DK), lambda b, h, j: (b, j, kb + h)),
                  pl.BlockSpec((1, sb, GLA_DV), lambda b, h, j: (b, j, vb + h)),
                  pl.BlockSpec((1, sb, GLA_DV), lambda b, h, j: (b, j, rb + h)),
                  pl.BlockSpec((1, sb, LANES), lambda b, h, j: (b, j, 0)),
                  pl.BlockSpec((LANES, GLA_DK), lambda b, h, j: (0, h)),
                  pl.BlockSpec((1, GLA_DK), lambda b, h, j: (0, h)),
                  pl.BlockSpec((1, GLA_DV), lambda b, h, j: (0, h)), hbm, hbm],
        out_specs=[pl.BlockSpec((1, sb, GLA_DV), lambda b, h, j: (b, j, h)), hbm, hbm],
        out_shape=[jax.ShapeDtypeStruct((bsz, s, GLA_HEADS * GLA_DV), BF16),
                   jax.ShapeDtypeStruct(side_a.shape[1:], BF16),
                   jax.ShapeDtypeStruct(side_b.shape[1:], BF16)],
        scratch_shapes=[pltpu.VMEM((GLA_DV, GLA_DK), F32),
                        pltpu.VMEM((2, ra, side_a.shape[2]), F32), pltpu.VMEM((2, rb_, side_b.shape[2]), F32),
                        pltpu.VMEM((2, ra, side_a.shape[2]), BF16), pltpu.VMEM((2, rb_, side_b.shape[2]), BF16),
                        pltpu.SemaphoreType.DMA((2, 2, 2))],
        compiler_params=_cparams("arbitrary", "arbitrary", "arbitrary"),
        name="gla",
    )(h3, h3, h3, h3, glr3, w2p, gb, ng, side_a, side_b)


def _dil_kernel(q_ref, k_ref, v_ref, o_ref, qf, kf, vf, qg, kg, vg, qc, kc, vc, ob, db, mx,
                scb, eb):
    s = q_ref.shape[1]
    band = DIL_BAND
    unroll = DIL_UNROLL
    nblk = s // band
    qf[...] = q_ref[0].astype(F32)
    kf[...] = k_ref[0].astype(F32)
    vf[...] = v_ref[0].astype(F32)
    kc[:band, :] = jnp.zeros((band, DIL_HD), BF16)
    vc[:band, :] = jnp.zeros((band, DIL_HD), BF16)

    qi = lax.broadcasted_iota(jnp.int32, (band, 2 * band), 0)
    kj = lax.broadcasted_iota(jnp.int32, (band, 2 * band), 1)
    allowed = (kj >= qi) & (kj <= qi + band)
    bias = jnp.where(allowed, 0.0, -jnp.inf).astype(F32)
    bias0 = jnp.where(allowed & (kj >= band), 0.0, -jnp.inf).astype(F32)
    ones = jnp.ones((2 * band, LANES), BF16)

    for p, (window, dil) in enumerate(DIL_PATTERNS):
        assert window // dil == band
        cls = s // dil
        nb = cls // band
        span = band * dil
        assert nblk % unroll == 0 and (nb % unroll == 0 or unroll % nb == 0)

        if dil == 1:
            qc[...] = q_ref[0]
            kc[band:, :] = k_ref[0]
            vc[band:, :] = v_ref[0]
        else:
            prev = DIL_PATTERNS[p - 1][1]
            step = dil // prev
            assert step * prev == dil and step in (2, 4)
            keep = p + 1 < len(DIL_PATTERNS)
            srcs, dsts = ((qf, kf, vf), (qg, kg, vg)) if p % 2 == 1 else ((qg, kg, vg), (qf, kf, vf))
            for r in range(dil):
                rows = pl.ds((r % prev) * (s // prev) + r // prev, cls, stride=step)
                for src, dst, cm, off in zip(srcs, dsts, (qc, kc, vc), (0, band, band)):
                    x = src[rows, :]
                    if keep:
                        dst[r * cls:(r + 1) * cls, :] = x
                    cm[off + r * cls:off + (r + 1) * cls, :] = x.astype(BF16)

        def out_rows(g, lo=0, cnt=band, dil=dil, nb=nb, span=span):
            start = g // nb + (g % nb) * span + lo * dil
            return pl.ds(start, cnt) if dil == 1 else pl.ds(start, cnt, stride=dil)

        def scores(t, slot):
            for u in range(unroll):
                g = t * unroll + u
                k2 = kc[g * band:(g + 2) * band, :]
                scb[slot, u * band:(u + 1) * band, :] = _dot_nt(qc[g * band:(g + 1) * band, :], k2)

        def softmax(t, slot, p=p, nb=nb, out_rows=out_rows):
            for u in range(unroll):
                g = t * unroll + u
                bb = bias0 if g % nb == 0 else bias
                rows = slice(u * band, (u + 1) * band)
                m = jnp.max(scb[slot, rows, :] + bb, axis=-1, keepdims=True)
                mx[p, out_rows(g), :] = jnp.broadcast_to(m, (band, LANES))
                for half in range(2):
                    cols = slice(half * band, (half + 1) * band)
                    eb[slot, rows, cols] = jnp.exp(scb[slot, rows, cols] + bb[:, cols] - m).astype(BF16)

        def values(t, slot, p=p, out_rows=out_rows):
            for u in range(unroll):
                g = t * unroll + u
                v2 = vc[g * band:(g + 2) * band, :]
                oe = _dot(eb[slot, u * band:(u + 1) * band, :],
                          jnp.concatenate([v2, ones], axis=1))
                ob[p, out_rows(g), :] = oe[:, :DIL_HD]
                db[p, out_rows(g), :] = oe[:, DIL_HD:]

        ngrp = nblk // unroll
        for t in range(ngrp + 2):
            if t < ngrp:
                scores(t, t % 2)
            if 1 <= t <= ngrp:
                softmax(t - 1, (t - 1) % 2)
            if t >= 2:
                values(t - 2, t % 2)

    mb = 512

    def merge(i, carry):
        rs = pl.ds(pl.multiple_of(i * mb, mb), mb)
        m0, m1, m2 = mx[0, rs, :], mx[1, rs, :], mx[2, rs, :]
        m = jnp.maximum(jnp.maximum(m0, m1), m2)
        e0, e1, e2 = jnp.exp(m0 - m), jnp.exp(m1 - m), jnp.exp(m2 - m)
        num = e0 * ob[0, rs, :] + e1 * ob[1, rs, :] + e2 * ob[2, rs, :]
        den = e0 * db[0, rs, :] + e1 * db[1, rs, :] + e2 * db[2, rs, :]
        o_ref[0, rs, :] = (num / den).astype(o_ref.dtype)
        return carry

    lax.fori_loop(0, s // mb, merge, 0)


def _dil(h3):
    bsz, s, _ = h3.shape
    qb = (2 * GLA_HEADS * GLA_DK + 2 * GLA_HEADS * GLA_DV) // DIL_HD
    kb = qb + DIL_HEADS
    vb = kb + DIL_HEADS
    npat = len(DIL_PATTERNS)
    return pl.pallas_call(
        _dil_kernel,
        grid=(bsz, DIL_HEADS),
        in_specs=[pl.BlockSpec((1, s, DIL_HD), lambda b, h: (b, 0, qb + h)),
                  pl.BlockSpec((1, s, DIL_HD), lambda b, h: (b, 0, kb + h)),
                  pl.BlockSpec((1, s, DIL_HD), lambda b, h: (b, 0, vb + h))],
        out_specs=pl.BlockSpec((1, s, DIL_HD), lambda b, h: (b, 0, h)),
        out_shape=jax.ShapeDtypeStruct((bsz, s, DIL_HEADS * DIL_HD), BF16),
        scratch_shapes=[pltpu.VMEM((s, DIL_HD), F32)] * 6 + [
                        pltpu.VMEM((s, DIL_HD), BF16), pltpu.VMEM((s + DIL_BAND, DIL_HD), BF16),
                        pltpu.VMEM((s + DIL_BAND, DIL_HD), BF16),
                        pltpu.VMEM((npat, s, DIL_HD), F32), pltpu.VMEM((npat, s, LANES), F32),
                        pltpu.VMEM((npat, s, LANES), F32),
                        pltpu.VMEM((2, DIL_UNROLL * DIL_BAND, 2 * DIL_BAND), F32),
                        pltpu.VMEM((2, DIL_UNROLL * DIL_BAND, 2 * DIL_BAND), BF16)],
        compiler_params=_cparams("parallel", "parallel"),
        name="dilated_attention",
    )(h3, h3, h3)


def _resident(shape):
    return pl.BlockSpec(shape, lambda *_: (0,) * len(shape), pipeline_mode=pl.Buffered(1))


def _load_weight_bf16(w_hbm, dst_ref, stage_ref, sem):
    rows = stage_ref.shape[1]
    nchunk = w_hbm.shape[0] // rows
    copies = [pltpu.make_async_copy(w_hbm.at[pl.ds(c * rows, rows), :], stage_ref.at[c % 2],
                                    sem.at[c % 2]) for c in range(nchunk)]
    copies[0].start()
    for c in range(nchunk):
        if c + 1 < nchunk:
            copies[c + 1].start()
        copies[c].wait()
        dst_ref[c * rows:(c + 1) * rows, :] = stage_ref[c % 2].astype(BF16)


_WEIGHT_STAGE = [pltpu.VMEM((2, WEIGHT_ROWS, 2048), F32), pltpu.SemaphoreType.DMA((2,))]


def _outproj_kernel(og_ref, od_ref, x_ref, g_ref, b_ref, w_hbm, o_ref, w_ref, stage_ref, sem,
                    *, alpha):
    @pl.when(pl.program_id(0) == 0)
    def _():
        _load_weight_bf16(w_hbm, w_ref, stage_ref, sem)

    ka = og_ref.shape[1]
    acc = _dot(og_ref[...], w_ref[:ka, :]) + _dot(od_ref[...], w_ref[ka:, :])
    o_ref[...] = _layer_norm(alpha * x_ref[...] + acc, g_ref[...], b_ref[...])


def _outproj(og, od, w, x2d, g, b, bm, alpha):
    t, d = x2d.shape
    ka, kb = og.shape[1], od.shape[1]
    assert w.shape == (ka + kb, d) and d == _WEIGHT_STAGE[0].shape[2]
    return pl.pallas_call(
        functools.partial(_outproj_kernel, alpha=alpha),
        grid=(t // bm,),
        in_specs=[pl.BlockSpec((bm, ka), lambda i: (i, 0)),
                  pl.BlockSpec((bm, kb), lambda i: (i, 0)),
                  pl.BlockSpec((bm, d), lambda i: (i, 0)),
                  _resident((1, d)), _resident((1, d)),
                  pl.BlockSpec(memory_space=pl.ANY)],
        out_specs=pl.BlockSpec((bm, d), lambda i: (i, 0)),
        out_shape=jax.ShapeDtypeStruct((t, d), F32),
        scratch_shapes=[pltpu.VMEM((ka + kb, d), BF16)] + _WEIGHT_STAGE,
        compiler_params=_cparams("arbitrary"),
        name="out_projection_ln1",
    )(og, od, x2d, g, b, w)


def _memkv_kernel(m_ref, w_ref, o_ref):
    o_ref[...] = _dot(m_ref[...].astype(BF16), w_ref[...].astype(BF16)).astype(o_ref.dtype)


def _memkv(mem2d, wkv, bn):
    t, d = mem2d.shape
    n = wkv.shape[1]
    return pl.pallas_call(
        _memkv_kernel,
        grid=(n // bn,),
        in_specs=[pl.BlockSpec((t, d), lambda j: (0, 0)),
                  pl.BlockSpec((d, bn), lambda j: (0, j))],
        out_specs=pl.BlockSpec((t, bn), lambda j: (0, j)),
        out_shape=jax.ShapeDtypeStruct((t, n), BF16),
        compiler_params=_cparams("parallel"),
        name="memory_kv_projection",
    )(mem2d, wkv)


def _cross_kernel(x_ref, kv_ref, g_ref, b_ref, wq_hbm, wo_hbm, o_ref,
                  wq_ref, wo_ref, oc_ref, stage_ref, sem, *, alpha):
    @pl.when(pl.program_id(0) == 0)
    def _():
        _load_weight_bf16(wq_hbm, wq_ref, stage_ref, sem)
        _load_weight_bf16(wo_hbm, wo_ref, stage_ref, sem)

    d = x_ref.shape[1]
    hd = d // CA_HEADS
    x = x_ref[...]
    q = _dot(x.astype(BF16), wq_ref[...]).astype(BF16)
    for h in range(CA_HEADS):
        cols = slice(h * hd, (h + 1) * hd)
        mk = kv_ref[0, :, cols]
        mv = kv_ref[0, :, d + h * hd:d + (h + 1) * hd]
        sc = _dot_nt(q[:, cols], mk) * (hd ** -0.5)
        m = jnp.max(sc, axis=-1, keepdims=True)
        e = jnp.exp(sc - m)
        p = e / jnp.sum(e, axis=-1, keepdims=True)
        oc_ref[:, cols] = _dot(p.astype(BF16), mv).astype(BF16)
    y = alpha * x + _dot(oc_ref[...], wo_ref[...])
    o_ref[...] = _layer_norm(y, g_ref[...], b_ref[...])


def _cross(x1, wq, kv3, wo, g, b, bm, seq, alpha):
    t, d = x1.shape
    m = kv3.shape[1]
    per = seq // bm
    hbm = pl.BlockSpec(memory_space=pl.ANY)
    assert d == _WEIGHT_STAGE[0].shape[2]
    return pl.pallas_call(
        functools.partial(_cross_kernel, alpha=alpha),
        grid=(t // bm,),
        in_specs=[pl.BlockSpec((bm, d), lambda i: (i, 0)),
                  pl.BlockSpec((1, m, 2 * d), lambda i: (i // per, 0, 0)),
                  _resident((1, d)), _resident((1, d)), hbm, hbm],
        out_specs=pl.BlockSpec((bm, d), lambda i: (i, 0)),
        out_shape=jax.ShapeDtypeStruct((t, d), F32),
        scratch_shapes=[pltpu.VMEM((d, d), BF16), pltpu.VMEM((d, d), BF16),
                        pltpu.VMEM((bm, d), BF16)] + _WEIGHT_STAGE,
        compiler_params=_cparams("arbitrary"),
        name="cross_attention_ln2",
    )(x1, kv3, g, b, wq, wo)


def _causal_conv(u_ref, cw, cb, r0, n):
    h = FFN_TAIL + r0
    y = cb + cw[0:1, :] * u_ref[h - 2:h - 2 + n, :]
    y = y + cw[1:2, :] * u_ref[h - 1:h - 1 + n, :]
    return y + cw[2:3, :] * u_ref[h:h + n, :]


def _ffn_kernel(cp_ref, g_ref, b_ref, x_hbm, w1_hbm, w2_hbm, o_ref,
                x_buf, xb_ref, act0_ref, act1_ref, us_ref, tail_ref, wg_buf, wu_buf, w2_buf,
                sem, x_sem, *, alpha, per):
    i = pl.program_id(0)
    d_ff = w2_hbm.shape[0]
    bm = x_buf.shape[0]
    bf = wg_buf.shape[2]
    nf = pl.cdiv(d_ff, bf)
    last_w = d_ff - (nf - 1) * bf
    assert nf % 2 == 1 and nf >= 3
    assert bf % FFN_SUB == 0 and last_w % LANES == 0 and bm % FFN_ROWS == 0
    c_first = i * nf
    acts = (act0_ref, act1_ref)
    nslab = bf // LANES
    half = o_ref.shape[1] // 2

    def width(f):
        return last_w if isinstance(f, int) and f == nf - 1 else bf

    def up_copies(f, slot):
        w = width(f)
        col = f * bf if isinstance(f, int) else pl.multiple_of(f * bf, bf)
        return (pltpu.make_async_copy(w1_hbm.at[:, pl.ds(col, w)],
                                      wg_buf.at[slot, :, pl.ds(0, w)], sem.at[0, slot]),
                pltpu.make_async_copy(w1_hbm.at[:, pl.ds(d_ff + col, w)],
                                      wu_buf.at[slot, :, pl.ds(0, w)], sem.at[1, slot]))

    def down_copy(f, slot):
        w = width(f)
        row = f * bf if isinstance(f, int) else pl.multiple_of(f * bf, bf)
        return pltpu.make_async_copy(w2_hbm.at[pl.ds(row, w), :],
                                     w2_buf.at[slot, pl.ds(0, w), :], sem.at[2, slot])

    def x_copy(step):
        row = pl.multiple_of(step * bm, bm)
        return pltpu.make_async_copy(x_hbm.at[pl.ds(row, bm), :], x_buf, x_sem.at[0])

    def region_copies(f):
        slot = lax.rem(c_first + f, 2)
        for cp in up_copies(f, slot):
            cp.wait()
        if isinstance(f, int) and f == 0:
            @pl.when(i + 1 < pl.num_programs(0))
            def _():
                x_copy(i + 1).start()
        if not (isinstance(f, int) and f == 0):
            down_copy(f - 1, 1 - slot).wait()
        if isinstance(f, int) and f == nf - 1:
            @pl.when(i + 1 < pl.num_programs(0))
            def _():
                for cp in up_copies(0, 1 - slot):
                    cp.start()
        else:
            for cp in up_copies(f + 1, 1 - slot):
                cp.start()
        down_copy(f, slot).start()
        return slot

    def up_matmul(slot, c0, wcols, r0):
        xr = xb_ref[r0:r0 + FFN_ROWS, :]
        rows = slice(FFN_TAIL + r0, FFN_TAIL + r0 + FFN_ROWS)
        for base, w_buf in ((0, wg_buf), (nslab, wu_buf)):
            res = _dot(xr, w_buf[slot, :, c0:c0 + wcols])
            for k in range(wcols // LANES):
                us_ref[base + c0 // LANES + k, rows, :] = res[:, k * LANES:(k + 1) * LANES]

    def activate(cv, act_ref, c0, wcols, r0):
        for s in range(c0 // LANES, (c0 + wcols) // LANES):
            cols = slice(s * LANES, (s + 1) * LANES)
            gate = _causal_conv(us_ref.at[s], cv[0:3, cols], cv[3:4, cols], r0, FFN_ROWS)
            up = _causal_conv(us_ref.at[nslab + s], cv[4:7, cols], cv[7:8, cols], r0, FFN_ROWS)
            act_ref[r0:r0 + FFN_ROWS, cols] = (gate * _sigmoid(gate) * up).astype(BF16)

    def down_matmul(slot, act_ref, w, n0):
        o_ref[:, n0:n0 + half] += _dot(act_ref[:, :w], w2_buf[slot, :w, n0:n0 + half])

    def region(f, act_slot):
        slot = region_copies(f)
        w = width(f)
        cv = cp_ref[f]
        act_ref, prev_ref = acts[act_slot], acts[1 - act_slot]
        slabs = [s for base in (0, nslab) for s in range(base, base + w // LANES)]
        for s in slabs:
            us_ref[s, :FFN_TAIL, :] = tail_ref[f, :, s * LANES:(s + 1) * LANES]
        units = [(c0, min(FFN_SUB, w - c0), r0) for c0 in range(0, w, FFN_SUB)
                 for r0 in range(0, bm, FFN_ROWS)]
        for unit in units:
            up_matmul(slot, *unit)
        if not (isinstance(f, int) and f == 0):
            for n0 in (0, half):
                down_matmul(1 - slot, prev_ref, width(f - 1) if isinstance(f, int) else bf, n0)
        for unit in units:
            activate(cv, act_ref, *unit)
        for s in slabs:
            tail_ref[f, :, s * LANES:(s + 1) * LANES] = us_ref[s, bm:bm + FFN_TAIL, :]

    @pl.when(i == 0)
    def _():
        x_copy(0).start()
        for cp in up_copies(0, 0):
            cp.start()

    @pl.when(i % per == 0)
    def _():
        tail_ref[...] = jnp.zeros_like(tail_ref)

    x_copy(i).wait()
    xb_ref[...] = x_buf[...].astype(BF16)
    o_ref[...] = alpha * x_buf[...]

    region(0, 0)

    def pair(j, carry):
        region(2 * j + 1, 1)
        region(2 * j + 2, 0)
        return carry

    lax.fori_loop(0, (nf - 3) // 2, pair, 0)
    region(nf - 2, 1)
    region(nf - 1, 0)

    last = lax.rem(c_first + nf - 1, 2)
    down_copy(nf - 1, last).wait()
    for n0 in (0, half):
        down_matmul(last, acts[0], last_w, n0)
    o_ref[...] = _layer_norm(o_ref[...], g_ref[...], b_ref[...])


def _ffn(x2, w1, w2, cp, g, b, bm, seq, alpha):
    t, d = x2.shape
    bf = FFN_CHUNK
    nf = cp.shape[0]
    hbm = pl.BlockSpec(memory_space=pl.ANY)
    return pl.pallas_call(
        functools.partial(_ffn_kernel, alpha=alpha, per=seq // bm),
        grid=(t // bm,),
        in_specs=[_resident(cp.shape), _resident((1, d)), _resident((1, d)), hbm, hbm, hbm],
        out_specs=pl.BlockSpec((bm, d), lambda i: (i, 0)),
        out_shape=jax.ShapeDtypeStruct((t, d), F32),
        scratch_shapes=[pltpu.VMEM((bm, d), F32), pltpu.VMEM((bm, d), BF16),
                        pltpu.VMEM((bm, bf), BF16), pltpu.VMEM((bm, bf), BF16),
                        pltpu.VMEM((2 * bf // LANES, FFN_TAIL + bm, LANES), F32),
                        pltpu.VMEM((nf, FFN_TAIL, 2 * bf), F32),
                        pltpu.VMEM((2, d, bf), BF16), pltpu.VMEM((2, d, bf), BF16),
                        pltpu.VMEM((2, bf, d), BF16), pltpu.SemaphoreType.DMA((3, 2)),
                        pltpu.SemaphoreType.DMA((1,))],
        compiler_params=_cparams("arbitrary"),
        name="conv_ffn_ln3",
    )(cp, g, b, x2, w1, w2)


def _pad_cols(a, n):
    return jnp.pad(a, ((0, 0), (0, n - a.shape[1])))


def kernel(x, mem, positions, w_in, gla_gate_w2, gla_gate_b, gla_norm_g, w_out, ln1_g, ln1_b,
           ca_wq, ca_wkv, ca_wo, ln2_g, ln2_b, ffn_w_in, ffn_conv_w, ffn_conv_b, ffn_w_out,
           ln3_g, ln3_b):
    bsz, seq, d = x.shape
    depth = w_in.shape[0]
    t = bsz * seq
    alpha = (2.0 * depth) ** 0.25
    d_ff = ffn_w_out.shape[1]
    dff_pad = -(-d_ff // FFN_CHUNK) * FFN_CHUNK

    nqk = GLA_HEADS * GLA_DK
    nv = GLA_HEADS * GLA_DV
    c_glr = 2 * nqk + 2 * nv
    c_dil = c_glr + GLA_GATE_RANK

    half = ROPE_HALF
    inv_freq = ROPE_THETA ** (-jnp.arange(0, ROPE_DIMS, 2, dtype=F32) / ROPE_DIMS)
    inv_row = jnp.concatenate([inv_freq, inv_freq, jnp.zeros((LANES - 2 * half,), F32)])[None, :]
    pos_col = positions.astype(F32).reshape(t, 1)
    cosf, sina, sinb = _rope_tables(pos_col, inv_row, min(t, 2048))

    x2d = x.reshape(t, d)
    for l in range(depth):
        wcat, wglr = _prep_win(jnp.swapaxes(w_in[l], 0, 1), c_glr, c_dil, 256)
        h, glr = _inproj(x2d, wcat, wglr, cosf, sina, sinb, min(t, 512))
        h3 = h.reshape(bsz, seq, h.shape[1])

        w2p = jnp.pad(gla_gate_w2[l], ((0, LANES - GLA_GATE_RANK), (0, 0))).astype(BF16)
        og, ffn_w1, ffn_w2 = _gla(h3, glr.reshape(bsz, seq, LANES), w2p, gla_gate_b[l][None, :],
                                  gla_norm_g[l][None, :], min(seq, 1024), ffn_w_in, ffn_w_out, l)
        od = _dil(h3)

        x1 = _outproj(og.reshape(t, nv), od.reshape(t, DIL_HEADS * DIL_HD), w_out[l],
                      x2d, ln1_g[l][None, :], ln1_b[l][None, :], min(t, 512), alpha)

        kv = _memkv(mem.reshape(-1, d), ca_wkv[l], 1024)
        x2 = _cross(x1, ca_wq[l], kv.reshape(bsz, -1, 2 * d), ca_wo[l],
                    ln2_g[l][None, :], ln2_b[l][None, :], min(seq, 512), seq, alpha)

        cw = ffn_conv_w[l]
        cb = ffn_conv_b[l][None, :]
        nf = dff_pad // FFN_CHUNK
        conv = jnp.concatenate([cw[:, :d_ff], cb[:, :d_ff], cw[:, d_ff:], cb[:, d_ff:]], axis=0)
        conv = _pad_cols(conv, dff_pad).reshape(conv.shape[0], nf, FFN_CHUNK).transpose(1, 0, 2)
        x2d = _ffn(x2, ffn_w1, ffn_w2, conv,
                   ln3_g[l][None, :], ln3_b[l][None, :], min(seq, 1024), seq, alpha)
    return x2d.reshape(bsz, seq, d)
```

```python
import functools

import jax
import jax.numpy as jnp
from jax import lax
from jax.experimental import pallas as pl
from jax.experimental.pallas import tpu as pltpu

F32 = jnp.float32
BF16 = jnp.bfloat16

LANES = 128
LN_EPS = 1e-5
LOG2E = 1.4426950408889634
GLA_HEADS = 4
GLA_DK = 128
GLA_DV = 256
GLA_GATE_RANK = 16
GLA_TAU = 16.0
GLA_CHUNK = 64
DIL_HD = 128
DIL_HEADS = 8
DIL_PATTERNS = ((128, 1), (512, 4), (2048, 16))
DIL_BAND = 128
DIL_UNROLL = 4
ROPE_THETA = 500000.0
ROPE_DIMS = 32
ROPE_HALF = ROPE_DIMS // 2
CA_HEADS = 4
CONV_W = 3
INPROJ_BLOCK = 1024
INPROJ_SUB = 256
WEIGHT_ROWS = 256
FFN_CHUNK = 512
FFN_SUB = 512
FFN_ROWS = 1024
FFN_DOWN_SPLIT = 2
FFN_TAIL = 8
VMEM_LIMIT = 56 * 1024 * 1024


def _cparams(*sem):
    return pltpu.CompilerParams(dimension_semantics=sem, vmem_limit_bytes=VMEM_LIMIT)


def _dot(a, b):
    return jnp.dot(a, b, preferred_element_type=F32)


def _dot_nt(a, b):
    return lax.dot_general(a, b, (((1,), (1,)), ((), ())), preferred_element_type=F32)


def _dot_tn(a, b):
    return lax.dot_general(a, b, (((0,), (0,)), ((), ())), preferred_element_type=F32)


def _layer_norm(y, g, b):
    mu = jnp.mean(y, axis=-1, keepdims=True)
    d = y - mu
    var = jnp.mean(d * d, axis=-1, keepdims=True)
    return d * lax.rsqrt(var + LN_EPS) * g + b


def _sigmoid(x):
    return 1.0 / (1.0 + jnp.exp2(x * -LOG2E))


def _rope_kernel(pos_ref, inv_ref, cos_ref, sa_ref, sb_ref):
    ang = pos_ref[...] * inv_ref[...]
    lane = lax.broadcasted_iota(jnp.int32, ang.shape, 1)
    c = jnp.cos(ang)
    s = jnp.sin(ang)
    cos_ref[...] = jnp.where(lane < ROPE_DIMS, c, 1.0)
    sa_ref[...] = jnp.where(lane < ROPE_HALF, -s, 0.0)
    sb_ref[...] = jnp.where(lane < ROPE_HALF, 0.0, jnp.where(lane < ROPE_DIMS, s, 0.0))


def _rope_tables(pos_col, inv_row, bs):
    t = pos_col.shape[0]
    out = jax.ShapeDtypeStruct((t, LANES), F32)
    spec = pl.BlockSpec((bs, LANES), lambda i: (i, 0))
    return pl.pallas_call(
        _rope_kernel,
        grid=(t // bs,),
        in_specs=[pl.BlockSpec((bs, 1), lambda i: (i, 0)),
                  pl.BlockSpec((1, LANES), lambda i: (0, 0))],
        out_specs=[spec, spec, spec],
        out_shape=[out, out, out],
        compiler_params=_cparams("parallel"),
        name="rope_tables",
    )(pos_col, inv_row)


def _prep_win_kernel(wt_hbm, wcat_ref, wglr_ref, buf, gbuf, sem, *, c_glr, c_dil):
    j = pl.program_id(0)
    bn = buf.shape[1]
    rank = c_dil - c_glr

    def copy(jj, slot):
        row = jj * bn
        row = pl.multiple_of(row + jnp.where(row >= c_glr, rank, 0), 8)
        return pltpu.make_async_copy(wt_hbm.at[pl.ds(row, bn), :], buf.at[slot], sem.at[slot])

    gate_copy = pltpu.make_async_copy(wt_hbm.at[pl.ds(c_glr, rank), :], gbuf.at[pl.ds(0, rank), :],
                                      sem.at[2])

    @pl.when(j == 0)
    def _():
        copy(0, 0).start()
        gate_copy.start()
        gbuf[rank:, :] = jnp.zeros((gbuf.shape[0] - rank, gbuf.shape[1]), F32)

    slot = lax.rem(j, 2)

    @pl.when(j + 1 < pl.num_programs(0))
    def _():
        copy(j + 1, 1 - slot).start()

    copy(j, slot).wait()
    wcat_ref[...] = buf[slot].T.astype(BF16)

    @pl.when(j == 0)
    def _():
        gate_copy.wait()
        wglr_ref[...] = gbuf[...].T.astype(BF16)


def _prep_win(wt, c_glr, c_dil, bn):
    ncols, d = wt.shape
    ncat = ncols - (c_dil - c_glr)
    assert c_glr % bn == 0 and ncat % bn == 0 and (c_dil - c_glr) % 8 == 0
    return pl.pallas_call(
        functools.partial(_prep_win_kernel, c_glr=c_glr, c_dil=c_dil),
        grid=(ncat // bn,),
        in_specs=[pl.BlockSpec(memory_space=pl.ANY)],
        out_specs=[pl.BlockSpec((d, bn), lambda j: (0, j)),
                   pl.BlockSpec((d, LANES), lambda j: (0, 0))],
        out_shape=[jax.ShapeDtypeStruct((d, ncat), BF16), jax.ShapeDtypeStruct((d, LANES), BF16)],
        scratch_shapes=[pltpu.VMEM((2, bn, d), F32), pltpu.VMEM((LANES, d), F32),
                        pltpu.SemaphoreType.DMA((3,))],
        compiler_params=_cparams("arbitrary"),
        name="in_projection_weights",
    )(wt)


def _rope(t, cos, sa, sb):
    return t * cos + pltpu.roll(t, LANES - ROPE_HALF, 1) * sa + pltpu.roll(t, ROPE_HALF, 1) * sb


def _inproj_kernel(x_ref, w_ref, wg_ref, cos_ref, sa_ref, sb_ref, h_ref, glr_ref, xb_ref):
    bn = INPROJ_BLOCK
    xb_ref[...] = x_ref[...].astype(BF16)
    glr_ref[...] = _dot(xb_ref[...], wg_ref[...])

    def rope(scale):
        def epilogue(acc, c0):
            cos, sa, sb = cos_ref[...], sa_ref[...], sb_ref[...]
            heads = [acc[:, j:j + LANES] for j in range(0, acc.shape[1], LANES)]
            if scale is not None:
                heads = [t * scale for t in heads]
            return jnp.concatenate([_rope(t, cos, sa, sb) for t in heads], axis=1)
        return epilogue

    def plain(acc, c0):
        return acc

    epilogues = (lambda acc, c0: acc * (GLA_DK ** -0.5) if c0 < bn // 2 else acc,
                 plain, plain, rope(DIL_HD ** -0.5), rope(None), plain)
    assert len(epilogues) * bn == w_ref.shape[1]

    for n, epilogue in enumerate(epilogues):
        for c0 in range(0, bn, INPROJ_SUB):
            cols = slice(n * bn + c0, n * bn + c0 + INPROJ_SUB)
            h_ref[:, cols] = epilogue(_dot(xb_ref[...], w_ref[:, cols]), c0).astype(h_ref.dtype)


def _inproj(x2d, wcat, wglr, cosf, sina, sinb, bm):
    t, d = x2d.shape
    ncols = wcat.shape[1]
    tab = pl.BlockSpec((bm, LANES), lambda i: (i, 0))
    return pl.pallas_call(
        _inproj_kernel,
        grid=(t // bm,),
        in_specs=[pl.BlockSpec((bm, d), lambda i: (i, 0)),
                  _resident((d, ncols)), _resident((d, LANES)), tab, tab, tab],
        out_specs=[pl.BlockSpec((bm, ncols), lambda i: (i, 0)),
                   pl.BlockSpec((bm, LANES), lambda i: (i, 0))],
        out_shape=[jax.ShapeDtypeStruct((t, ncols), BF16),
                   jax.ShapeDtypeStruct((t, LANES), F32)],
        scratch_shapes=[pltpu.VMEM((bm, d), BF16)],
        compiler_params=_cparams("parallel"),
        name="in_projection",
    )(x2d, wcat, wglr, cosf, sina, sinb)


def _split3(v):
    hi = v.astype(BF16)
    r1 = v - hi.astype(F32)
    mid = r1.astype(BF16)
    lo = (r1 - mid.astype(F32)).astype(BF16)
    return hi, mid, lo


def _side_cast_step(step, nsteps, srcs, dsts, in_bufs, out_bufs, sem):
    slot = lax.rem(step, 2)

    for k, (src, dst, ibuf, obuf) in enumerate(zip(srcs, dsts, in_bufs, out_bufs)):
        rows = ibuf.shape[1]
        nslice = src.shape[0] // rows
        assert src.shape[0] == nslice * rows and rows % 16 == 0 and 2 <= nslice <= nsteps

        def fetch(s, sl, src=src, ibuf=ibuf, rows=rows, k=k):
            at = pl.ds(pl.multiple_of(s * rows, 16), rows)
            return pltpu.make_async_copy(src.at[at, :], ibuf.at[sl], sem.at[0, k, sl])

        def write(s, sl, dst=dst, obuf=obuf, rows=rows, k=k):
            at = pl.ds(pl.multiple_of(s * rows, 16), rows)
            return pltpu.make_async_copy(obuf.at[sl], dst.at[at, :], sem.at[1, k, sl])

        @pl.when(step == 0)
        def _():
            fetch(0, 0).start()

        @pl.when(step + 1 < nslice)
        def _():
            fetch(step + 1, 1 - slot).start()

        @pl.when((step >= 2) & (step - 2 < nslice))
        def _():
            write(step - 2, slot).wait()

        @pl.when(step < nslice)
        def _():
            fetch(step, slot).wait()
            obuf[slot] = ibuf[slot].astype(BF16)
            write(step, slot).start()

        for s in range(max(nsteps - 2, 0), nsteps):
            if s < nslice:
                @pl.when(step == nsteps - 1)
                def _():
                    write(s, s % 2).wait()


def _side_cast_rows(total, nsteps):
    return next(r for r in range(16, total + 1, 16) if total % r == 0 and total // r <= nsteps)


def _gla_kernel(q_ref, k_ref, v_ref, r_ref, glr_ref, w2_ref, gb_ref, ng_ref, wa_hbm, wb_hbm,
                o_ref, wa_out, wb_out, st_ref, wa_in, wb_in, wa_cast, wb_cast, cast_sem,
                *, layer, nsteps):
    c = GLA_CHUNK
    step = (pl.program_id(0) * pl.num_programs(1) + pl.program_id(1)) * pl.num_programs(2) \
        + pl.program_id(2)
    _side_cast_step(step, nsteps, (wa_hbm.at[layer], wb_hbm.at[layer]), (wa_out, wb_out),
                    (wa_in, wb_in), (wa_cast, wb_cast), cast_sem)
    sb = q_ref.shape[1]
    grp = 4 * c

    @pl.when(pl.program_id(2) == 0)
    def _():
        st_ref[...] = jnp.zeros_like(st_ref)

    z = _dot(glr_ref[0].astype(BF16), w2_ref[...]) + gb_ref[...]
    lg = (jnp.minimum(z, 0.0) - jnp.log(1.0 + jnp.exp(-jnp.abs(z)))) / GLA_TAU

    row = lax.broadcasted_iota(jnp.int32, (grp, grp), 0)
    col = lax.broadcasted_iota(jnp.int32, (grp, grp), 1)
    shift = c.bit_length() - 1
    tril = jnp.where(((row >> shift) == (col >> shift)) & (col <= row), 1.0, 0.0).astype(BF16)
    b_parts = []
    for g0 in range(0, sb, grp):
        pieces = jnp.concatenate(_split3(lg[g0:g0 + grp]), axis=1)
        res = _dot(tril, pieces)
        b_parts.append(res[:, :LANES] + res[:, LANES:2 * LANES] + res[:, 2 * LANES:])
    b = jnp.concatenate(b_parts, axis=0)

    qf = q_ref[0].astype(F32)
    kf = k_ref[0].astype(F32)
    q_in = (qf * jnp.exp(b)).astype(BF16)
    k_in = (kf * jnp.exp(-b)).astype(BF16)
    v = v_ref[0]

    causal = ((row >> shift) == (col >> shift)) & (col <= row)
    intra = []
    for g0 in range(0, sb, grp):
        rows = slice(g0, g0 + grp)
        a = jnp.where(causal, _dot_nt(q_in[rows], k_in[rows]), 0.0).astype(BF16)
        intra.append(_dot(a, v[rows]))

    st = st_ref[...]
    outs = []
    for i in range(sb // c):
        rows = slice(i * c, (i + 1) * c)
        o = intra[i * c // grp][i * c % grp:i * c % grp + c] + _dot_nt(q_in[rows], st.astype(BF16))
        outs.append(o)
        decay = jnp.exp(b[i * c + c - 1:i * c + c, :])
        st = (st + _dot_tn(v[rows], k_in[rows])) * decay
    st_ref[...] = st

    o = jnp.concatenate(outs, axis=0)
    mu = jnp.mean(o, axis=-1, keepdims=True)
    d = o - mu
    var = jnp.mean(d * d, axis=-1, keepdims=True)
    rg = r_ref[0].astype(F32)
    y = d * lax.rsqrt(var + LN_EPS) * ng_ref[...] * (rg * _sigmoid(rg))
    o_ref[0] = y.astype(o_ref.dtype)


def _gla(h3, glr3, w2p, gb, ng, sb, side_a, side_b, layer):
    bsz, s, _ = h3.shape
    nsteps = bsz * GLA_HEADS * (s // sb)
    ra, rb_ = (_side_cast_rows(a.shape[1], nsteps) for a in (side_a, side_b))
    hbm = pl.BlockSpec(memory_space=pl.ANY)
    kb = GLA_HEADS * GLA_DK // LANES
    vb = 2 * GLA_HEADS * GLA_DK // GLA_DV
    rb = vb + GLA_HEADS
    return pl.pallas_call(
        functools.partial(_gla_kernel, layer=layer, nsteps=nsteps),
        grid=(bsz, GLA_HEADS, s // sb),
        in_specs=[pl.BlockSpec((1, sb, GLA_DK), lambda b, h, j: (b, j, h)),
                  pl.BlockSpec((1, sb, GLA_DK), lambda b, h, j: (b, j, kb + h)),
                  pl.BlockSpec((1, sb, GLA_DV), lambda b, h, j: (b, j, vb + h)),
                  pl.BlockSpec((1, sb, GLA_DV), lambda b, h, j: (b, j, rb + h)),
                  pl.BlockSpec((1, sb, LANES), lambda b, h, j: (b, j, 0)),
                  pl.BlockSpec((LANES, GLA_DK), lambda b, h, j: (0, h)),
                  pl.BlockSpec((1, GLA_DK), lambda b, h, j: (0, h)),
                  pl.BlockSpec((1, GLA_DV), lambda b, h, j: (0, h)), hbm, hbm],
        out_specs=[pl.BlockSpec((1, sb, GLA_DV), lambda b, h, j: (b, j, h)), hbm, hbm],
        out_shape=[jax.ShapeDtypeStruct((bsz, s, GLA_HEADS * GLA_DV), BF16),
                   jax.ShapeDtypeStruct(side_a.shape[1:], BF16),
                   jax.ShapeDtypeStruct(side_b.shape[1:], BF16)],
        scratch_shapes=[pltpu.VMEM((GLA_DV, GLA_DK), F32),
                        pltpu.VMEM((2, ra, side_a.shape[2]), F32), pltpu.VMEM((2, rb_, side_b.shape[2]), F32),
                        pltpu.VMEM((2, ra, side_a.shape[2]), BF16), pltpu.VMEM((2, rb_, side_b.shape[2]), BF16),
                        pltpu.SemaphoreType.DMA((2, 2, 2))],
        compiler_params=_cparams("arbitrary", "arbitrary", "arbitrary"),
        name="gla",
    )(h3, h3, h3, h3, glr3, w2p, gb, ng, side_a, side_b)


def _dil_kernel(q_ref, k_ref, v_ref, o_ref, qf, kf, vf, qg, kg, vg, qc, kc, vc, ob, db, mx,
                scb, eb):
    s = q_ref.shape[1]
    band = DIL_BAND
    unroll = DIL_UNROLL
    nblk = s // band
    qf[...] = q_ref[0].astype(F32)
    kf[...] = k_ref[0].astype(F32)
    vf[...] = v_ref[0].astype(F32)
    kc[:band, :] = jnp.zeros((band, DIL_HD), BF16)
    vc[:band, :] = jnp.zeros((band, DIL_HD), BF16)

    qi = lax.broadcasted_iota(jnp.int32, (band, 2 * band), 0)
    kj = lax.broadcasted_iota(jnp.int32, (band, 2 * band), 1)
    allowed = (kj >= qi) & (kj <= qi + band)
    bias = jnp.where(allowed, 0.0, -jnp.inf).astype(F32)
    bias0 = jnp.where(allowed & (kj >= band), 0.0, -jnp.inf).astype(F32)
    ones = jnp.ones((2 * band, LANES), BF16)

    for p, (window, dil) in enumerate(DIL_PATTERNS):
        assert window // dil == band
        cls = s // dil
        nb = cls // band
        span = band * dil
        assert nblk % unroll == 0 and (nb % unroll == 0 or unroll % nb == 0)

        if dil == 1:
            qc[...] = q_ref[0]
            kc[band:, :] = k_ref[0]
            vc[band:, :] = v_ref[0]
        else:
            prev = DIL_PATTERNS[p - 1][1]
            step = dil // prev
            assert step * prev == dil and step in (2, 4)
            keep = p + 1 < len(DIL_PATTERNS)
            srcs, dsts = ((qf, kf, vf), (qg, kg, vg)) if p % 2 == 1 else ((qg, kg, vg), (qf, kf, vf))
            for r in range(dil):
                rows = pl.ds((r % prev) * (s // prev) + r // prev, cls, stride=step)
                for src, dst, cm, off in zip(srcs, dsts, (qc, kc, vc), (0, band, band)):
                    x = src[rows, :]
                    if keep:
                        dst[r * cls:(r + 1) * cls, :] = x
                    cm[off + r * cls:off + (r + 1) * cls, :] = x.astype(BF16)

        def out_rows(g, lo=0, cnt=band, dil=dil, nb=nb, span=span):
            start = g // nb + (g % nb) * span + lo * dil
            return pl.ds(start, cnt) if dil == 1 else pl.ds(start, cnt, stride=dil)

        def scores(t, slot):
            for u in range(unroll):
                g = t * unroll + u
                k2 = kc[g * band:(g + 2) * band, :]
                scb[slot, u * band:(u + 1) * band, :] = _dot_nt(qc[g * band:(g + 1) * band, :], k2)

        def softmax(t, slot, p=p, nb=nb, out_rows=out_rows):
            for u in range(unroll):
                g = t * unroll + u
                bb = bias0 if g % nb == 0 else bias
                rows = slice(u * band, (u + 1) * band)
                m = jnp.max(scb[slot, rows, :] + bb, axis=-1, keepdims=True)
                mx[p, out_rows(g), :] = jnp.broadcast_to(m, (band, LANES))
                for half in range(2):
                    cols = slice(half * band, (half + 1) * band)
                    eb[slot, rows, cols] = jnp.exp(scb[slot, rows, cols] + bb[:, cols] - m).astype(BF16)

        def values(t, slot, p=p, out_rows=out_rows):
            for u in range(unroll):
                g = t * unroll + u
                v2 = vc[g * band:(g + 2) * band, :]
                oe = _dot(eb[slot, u * band:(u + 1) * band, :],
                          jnp.concatenate([v2, ones], axis=1))
                ob[p, out_rows(g), :] = oe[:, :DIL_HD]
                db[p, out_rows(g), :] = oe[:, DIL_HD:]

        ngrp = nblk // unroll
        for t in range(ngrp + 2):
            if t < ngrp:
                scores(t, t % 2)
            if 1 <= t <= ngrp:
                softmax(t - 1, (t - 1) % 2)
            if t >= 2:
                values(t - 2, t % 2)

    mb = 512

    def merge(i, carry):
        rs = pl.ds(pl.multiple_of(i * mb, mb), mb)
        m0, m1, m2 = mx[0, rs, :], mx[1, rs, :], mx[2, rs, :]
        m = jnp.maximum(jnp.maximum(m0, m1), m2)
        e0, e1, e2 = jnp.exp(m0 - m), jnp.exp(m1 - m), jnp.exp(m2 - m)
        num = e0 * ob[0, rs, :] + e1 * ob[1, rs, :] + e2 * ob[2, rs, :]
        den = e0 * db[0, rs, :] + e1 * db[1, rs, :] + e2 * db[2, rs, :]
        o_ref[0, rs, :] = (num / den).astype(o_ref.dtype)
        return carry

    lax.fori_loop(0, s // mb, merge, 0)


def _dil(h3):
    bsz, s, _ = h3.shape
    qb = (2 * GLA_HEADS * GLA_DK + 2 * GLA_HEADS * GLA_DV) // DIL_HD
    kb = qb + DIL_HEADS
    vb = kb + DIL_HEADS
    npat = len(DIL_PATTERNS)
    return pl.pallas_call(
        _dil_kernel,
        grid=(bsz, DIL_HEADS),
        in_specs=[pl.BlockSpec((1, s, DIL_HD), lambda b, h: (b, 0, qb + h)),
                  pl.BlockSpec((1, s, DIL_HD), lambda b, h: (b, 0, kb + h)),
                  pl.BlockSpec((1, s, DIL_HD), lambda b, h: (b, 0, vb + h))],
        out_specs=pl.BlockSpec((1, s, DIL_HD), lambda b, h: (b, 0, h)),
        out_shape=jax.ShapeDtypeStruct((bsz, s, DIL_HEADS * DIL_HD), BF16),
        scratch_shapes=[pltpu.VMEM((s, DIL_HD), F32)] * 6 + [
                        pltpu.VMEM((s, DIL_HD), BF16), pltpu.VMEM((s + DIL_BAND, DIL_HD), BF16),
                        pltpu.VMEM((s + DIL_BAND, DIL_HD), BF16),
                        pltpu.VMEM((npat, s, DIL_HD), F32), pltpu.VMEM((npat, s, LANES), F32),
                        pltpu.VMEM((npat, s, LANES), F32),
                        pltpu.VMEM((2, DIL_UNROLL * DIL_BAND, 2 * DIL_BAND), F32),
                        pltpu.VMEM((2, DIL_UNROLL * DIL_BAND, 2 * DIL_BAND), BF16)],
        compiler_params=_cparams("parallel", "parallel"),
        name="dilated_attention",
    )(h3, h3, h3)


def _resident(shape):
    return pl.BlockSpec(shape, lambda *_: (0,) * len(shape), pipeline_mode=pl.Buffered(1))


def _load_weight_bf16(w_hbm, dst_ref, stage_ref, sem):
    rows = stage_ref.shape[1]
    nchunk = w_hbm.shape[0] // rows
    copies = [pltpu.make_async_copy(w_hbm.at[pl.ds(c * rows, rows), :], stage_ref.at[c % 2],
                                    sem.at[c % 2]) for c in range(nchunk)]
    copies[0].start()
    for c in range(nchunk):
        if c + 1 < nchunk:
            copies[c + 1].start()
        copies[c].wait()
        dst_ref[c * rows:(c + 1) * rows, :] = stage_ref[c % 2].astype(BF16)


_WEIGHT_STAGE = [pltpu.VMEM((2, WEIGHT_ROWS, 2048), F32), pltpu.SemaphoreType.DMA((2,))]


def _outproj_kernel(og_ref, od_ref, x_ref, g_ref, b_ref, w_hbm, o_ref, w_ref, stage_ref, sem,
                    *, alpha):
    @pl.when(pl.program_id(0) == 0)
    def _():
        _load_weight_bf16(w_hbm, w_ref, stage_ref, sem)

    ka = og_ref.shape[1]
    acc = _dot(og_ref[...], w_ref[:ka, :]) + _dot(od_ref[...], w_ref[ka:, :])
    o_ref[...] = _layer_norm(alpha * x_ref[...] + acc, g_ref[...], b_ref[...])


def _outproj(og, od, w, x2d, g, b, bm, alpha):
    t, d = x2d.shape
    ka, kb = og.shape[1], od.shape[1]
    assert w.shape == (ka + kb, d) and d == _WEIGHT_STAGE[0].shape[2]
    return pl.pallas_call(
        functools.partial(_outproj_kernel, alpha=alpha),
        grid=(t // bm,),
        in_specs=[pl.BlockSpec((bm, ka), lambda i: (i, 0)),
                  pl.BlockSpec((bm, kb), lambda i: (i, 0)),
                  pl.BlockSpec((bm, d), lambda i: (i, 0)),
                  _resident((1, d)), _resident((1, d)),
                  pl.BlockSpec(memory_space=pl.ANY)],
        out_specs=pl.BlockSpec((bm, d), lambda i: (i, 0)),
        out_shape=jax.ShapeDtypeStruct((t, d), F32),
        scratch_shapes=[pltpu.VMEM((ka + kb, d), BF16)] + _WEIGHT_STAGE,
        compiler_params=_cparams("arbitrary"),
        name="out_projection_ln1",
    )(og, od, x2d, g, b, w)


def _memkv_kernel(m_ref, w_ref, o_ref):
    o_ref[...] = _dot(m_ref[...].astype(BF16), w_ref[...].astype(BF16)).astype(o_ref.dtype)


def _memkv(mem2d, wkv, bn):
    t, d = mem2d.shape
    n = wkv.shape[1]
    return pl.pallas_call(
        _memkv_kernel,
        grid=(n // bn,),
        in_specs=[pl.BlockSpec((t, d), lambda j: (0, 0)),
                  pl.BlockSpec((d, bn), lambda j: (0, j))],
        out_specs=pl.BlockSpec((t, bn), lambda j: (0, j)),
        out_shape=jax.ShapeDtypeStruct((t, n), BF16),
        compiler_params=_cparams("parallel"),
        name="memory_kv_projection",
    )(mem2d, wkv)


def _cross_kernel(x_ref, kv_ref, g_ref, b_ref, wq_hbm, wo_hbm, o_ref,
                  wq_ref, wo_ref, oc_ref, stage_ref, sem, *, alpha):
    @pl.when(pl.program_id(0) == 0)
    def _():
        _load_weight_bf16(wq_hbm, wq_ref, stage_ref, sem)
        _load_weight_bf16(wo_hbm, wo_ref, stage_ref, sem)

    d = x_ref.shape[1]
    hd = d // CA_HEADS
    x = x_ref[...]
    q = _dot(x.astype(BF16), wq_ref[...]).astype(BF16)
    for h in range(CA_HEADS):
        cols = slice(h * hd, (h + 1) * hd)
        mk = kv_ref[0, :, cols]
        mv = kv_ref[0, :, d + h * hd:d + (h + 1) * hd]
        sc = _dot_nt(q[:, cols], mk) * (hd ** -0.5)
        m = jnp.max(sc, axis=-1, keepdims=True)
        e = jnp.exp(sc - m)
        p = e / jnp.sum(e, axis=-1, keepdims=True)
        oc_ref[:, cols] = _dot(p.astype(BF16), mv).astype(BF16)
    y = alpha * x + _dot(oc_ref[...], wo_ref[...])
    o_ref[...] = _layer_norm(y, g_ref[...], b_ref[...])


def _cross(x1, wq, kv3, wo, g, b, bm, seq, alpha):
    t, d = x1.shape
    m = kv3.shape[1]
    per = seq // bm
    hbm = pl.BlockSpec(memory_space=pl.ANY)
    assert d == _WEIGHT_STAGE[0].shape[2]
    return pl.pallas_call(
        functools.partial(_cross_kernel, alpha=alpha),
        grid=(t // bm,),
        in_specs=[pl.BlockSpec((bm, d), lambda i: (i, 0)),
                  pl.BlockSpec((1, m, 2 * d), lambda i: (i // per, 0, 0)),
                  _resident((1, d)), _resident((1, d)), hbm, hbm],
        out_specs=pl.BlockSpec((bm, d), lambda i: (i, 0)),
        out_shape=jax.ShapeDtypeStruct((t, d), F32),
        scratch_shapes=[pltpu.VMEM((d, d), BF16), pltpu.VMEM((d, d), BF16),
                        pltpu.VMEM((bm, d), BF16)] + _WEIGHT_STAGE,
        compiler_params=_cparams("arbitrary"),
        name="cross_attention_ln2",
    )(x1, kv3, g, b, wq, wo)


def _causal_conv(u_ref, cw, cb, r0, n):
    h = FFN_TAIL + r0
    y = cb + cw[0:1, :] * u_ref[h - 2:h - 2 + n, :]
    y = y + cw[1:2, :] * u_ref[h - 1:h - 1 + n, :]
    return y + cw[2:3, :] * u_ref[h:h + n, :]


def _ffn_kernel(cp_ref, g_ref, b_ref, x_hbm, w1_hbm, w2_hbm, o_ref,
                x_buf, xb_ref, act0_ref, act1_ref, us_ref, tail_ref, wgu_buf, w2_buf,
                sem, x_sem, *, alpha, per):
    i = pl.program_id(0)
    d_ff = w2_hbm.shape[0]
    bm = x_buf.shape[0]
    bf = wgu_buf.shape[2] // 2
    nf = pl.cdiv(d_ff, bf)
    last_w = d_ff - (nf - 1) * bf
    assert nf % 2 == 1 and nf >= 3
    assert bf % FFN_SUB == 0 and last_w % LANES == 0 and bm % FFN_ROWS == 0
    c_first = i * nf
    acts = (act0_ref, act1_ref)
    nslab = bf // LANES
    half = o_ref.shape[1] // FFN_DOWN_SPLIT

    def width(f):
        return last_w if isinstance(f, int) and f == nf - 1 else bf

    def up_copies(f, slot):
        w = width(f)
        col = f * bf if isinstance(f, int) else pl.multiple_of(f * bf, bf)
        return (pltpu.make_async_copy(w1_hbm.at[:, pl.ds(col, w)],
                                      wgu_buf.at[slot, :, pl.ds(0, w)], sem.at[0, slot]),
                pltpu.make_async_copy(w1_hbm.at[:, pl.ds(d_ff + col, w)],
                                      wgu_buf.at[slot, :, pl.ds(bf, w)], sem.at[1, slot]))

    def down_copy(f, slot):
        w = width(f)
        row = f * bf if isinstance(f, int) else pl.multiple_of(f * bf, bf)
        return pltpu.make_async_copy(w2_hbm.at[pl.ds(row, w), :],
                                     w2_buf.at[slot, pl.ds(0, w), :], sem.at[2, slot])

    def x_copy(step):
        row = pl.multiple_of(step * bm, bm)
        return pltpu.make_async_copy(x_hbm.at[pl.ds(row, bm), :], x_buf, x_sem.at[0])

    def region_copies(f):
        slot = lax.rem(c_first + f, 2)
        for cp in up_copies(f, slot):
            cp.wait()
        if isinstance(f, int) and f == 0:
            @pl.when(i + 1 < pl.num_programs(0))
            def _():
                x_copy(i + 1).start()
        if not (isinstance(f, int) and f == 0):
            down_copy(f - 1, 1 - slot).wait()
        if isinstance(f, int) and f == nf - 1:
            @pl.when(i + 1 < pl.num_programs(0))
            def _():
                for cp in up_copies(0, 1 - slot):
                    cp.start()
        else:
            for cp in up_copies(f + 1, 1 - slot):
                cp.start()
        down_copy(f, slot).start()
        return slot

    def up_matmul(slot, c0, wcols, r0):
        xr = xb_ref[r0:r0 + FFN_ROWS, :]
        rows = slice(FFN_TAIL + r0, FFN_TAIL + r0 + FFN_ROWS)
        if wcols == bf:
            res = _dot(xr, wgu_buf[slot])
            for s in range(2 * nslab):
                us_ref[s, rows, :] = res[:, s * LANES:(s + 1) * LANES]
            return
        for base in (0, nslab):
            res = _dot(xr, wgu_buf[slot, :, base * LANES + c0:base * LANES + c0 + wcols])
            for k in range(wcols // LANES):
                us_ref[base + c0 // LANES + k, rows, :] = res[:, k * LANES:(k + 1) * LANES]

    def activate(cv, act_ref, c0, wcols, r0):
        for s in range(c0 // LANES, (c0 + wcols) // LANES):
            cols = slice(s * LANES, (s + 1) * LANES)
            gate = _causal_conv(us_ref.at[s], cv[0:3, cols], cv[3:4, cols], r0, FFN_ROWS)
            up = _causal_conv(us_ref.at[nslab + s], cv[4:7, cols], cv[7:8, cols], r0, FFN_ROWS)
            act_ref[r0:r0 + FFN_ROWS, cols] = (gate * _sigmoid(gate) * up).astype(BF16)

    def down_matmul(slot, act_ref, w, n0):
        o_ref[:, n0:n0 + half] += _dot(act_ref[:, :w], w2_buf[slot, :w, n0:n0 + half])

    def region(f, act_slot):
        slot = region_copies(f)
        w = width(f)
        cv = cp_ref[f]
        act_ref, prev_ref = acts[act_slot], acts[1 - act_slot]
        slabs = [s for base in (0, nslab) for s in range(base, base + w // LANES)]
        for s in slabs:
            us_ref[s, :FFN_TAIL, :] = tail_ref[f, :, s * LANES:(s + 1) * LANES]
        units = [(c0, min(FFN_SUB, w - c0), r0) for c0 in range(0, w, FFN_SUB)
                 for r0 in range(0, bm, FFN_ROWS)]
        for unit in units:
            up_matmul(slot, *unit)
        if not (isinstance(f, int) and f == 0):
            for n0 in range(0, o_ref.shape[1], half):
                down_matmul(1 - slot, prev_ref, width(f - 1) if isinstance(f, int) else bf, n0)
        for unit in units:
            activate(cv, act_ref, *unit)
        for s in slabs:
            tail_ref[f, :, s * LANES:(s + 1) * LANES] = us_ref[s, bm:bm + FFN_TAIL, :]

    @pl.when(i == 0)
    def _():
        x_copy(0).start()
        for cp in up_copies(0, 0):
            cp.start()

    @pl.when(i % per == 0)
    def _():
        tail_ref[...] = jnp.zeros_like(tail_ref)

    x_copy(i).wait()
    xb_ref[...] = x_buf[...].astype(BF16)
    o_ref[...] = alpha * x_buf[...]

    region(0, 0)

    def pair(j, carry):
        region(2 * j + 1, 1)
        region(2 * j + 2, 0)
        return carry

    lax.fori_loop(0, (nf - 3) // 2, pair, 0)
    region(nf - 2, 1)
    region(nf - 1, 0)

    last = lax.rem(c_first + nf - 1, 2)
    down_copy(nf - 1, last).wait()
    for n0 in range(0, o_ref.shape[1], half):
        down_matmul(last, acts[0], last_w, n0)
    o_ref[...] = _layer_norm(o_ref[...], g_ref[...], b_ref[...])


def _ffn(x2, w1, w2, cp, g, b, bm, seq, alpha):
    t, d = x2.shape
    bf = FFN_CHUNK
    nf = cp.shape[0]
    hbm = pl.BlockSpec(memory_space=pl.ANY)
    return pl.pallas_call(
        functools.partial(_ffn_kernel, alpha=alpha, per=seq // bm),
        grid=(t // bm,),
        in_specs=[_resident(cp.shape), _resident((1, d)), _resident((1, d)), hbm, hbm, hbm],
        out_specs=pl.BlockSpec((bm, d), lambda i: (i, 0)),
        out_shape=jax.ShapeDtypeStruct((t, d), F32),
        scratch_shapes=[pltpu.VMEM((bm, d), F32), pltpu.VMEM((bm, d), BF16),
                        pltpu.VMEM((bm, bf), BF16), pltpu.VMEM((bm, bf), BF16),
                        pltpu.VMEM((2 * bf // LANES, FFN_TAIL + bm, LANES), F32),
                        pltpu.VMEM((nf, FFN_TAIL, 2 * bf), F32),
                        pltpu.VMEM((2, d, 2 * bf), BF16),
                        pltpu.VMEM((2, bf, d), BF16), pltpu.SemaphoreType.DMA((3, 2)),
                        pltpu.SemaphoreType.DMA((1,))],
        compiler_params=_cparams("arbitrary"),
        name="conv_ffn_ln3",
    )(cp, g, b, x2, w1, w2)


def _pad_cols(a, n):
    return jnp.pad(a, ((0, 0), (0, n - a.shape[1])))


def kernel(x, mem, positions, w_in, gla_gate_w2, gla_gate_b, gla_norm_g, w_out, ln1_g, ln1_b,
           ca_wq, ca_wkv, ca_wo, ln2_g, ln2_b, ffn_w_in, ffn_conv_w, ffn_conv_b, ffn_w_out,
           ln3_g, ln3_b):
    bsz, seq, d = x.shape
    depth = w_in.shape[0]
    t = bsz * seq
    alpha = (2.0 * depth) ** 0.25
    d_ff = ffn_w_out.shape[1]
    dff_pad = -(-d_ff // FFN_CHUNK) * FFN_CHUNK

    nqk = GLA_HEADS * GLA_DK
    nv = GLA_HEADS * GLA_DV
    c_glr = 2 * nqk + 2 * nv
    c_dil = c_glr + GLA_GATE_RANK

    half = ROPE_HALF
    inv_freq = ROPE_THETA ** (-jnp.arange(0, ROPE_DIMS, 2, dtype=F32) / ROPE_DIMS)
    inv_row = jnp.concatenate([inv_freq, inv_freq, jnp.zeros((LANES - 2 * half,), F32)])[None, :]
    pos_col = positions.astype(F32).reshape(t, 1)
    cosf, sina, sinb = _rope_tables(pos_col, inv_row, min(t, 2048))

    x2d = x.reshape(t, d)
    for l in range(depth):
        wcat, wglr = _prep_win(jnp.swapaxes(w_in[l], 0, 1), c_glr, c_dil, 512)
        h, glr = _inproj(x2d, wcat, wglr, cosf, sina, sinb, min(t, 512))
        h3 = h.reshape(bsz, seq, h.shape[1])

        w2p = jnp.pad(gla_gate_w2[l], ((0, LANES - GLA_GATE_RANK), (0, 0))).astype(BF16)
        og, ffn_w1, ffn_w2 = _gla(h3, glr.reshape(bsz, seq, LANES), w2p, gla_gate_b[l][None, :],
                                  gla_norm_g[l][None, :], min(seq, 2048), ffn_w_in, ffn_w_out, l)
        od = _dil(h3)

        x1 = _outproj(og.reshape(t, nv), od.reshape(t, DIL_HEADS * DIL_HD), w_out[l],
                      x2d, ln1_g[l][None, :], ln1_b[l][None, :], min(t, 512), alpha)

        kv = _memkv(mem.reshape(-1, d), ca_wkv[l], 1024)
        x2 = _cross(x1, ca_wq[l], kv.reshape(bsz, -1, 2 * d), ca_wo[l],
                    ln2_g[l][None, :], ln2_b[l][None, :], min(seq, 512), seq, alpha)

        cw = ffn_conv_w[l]
        cb = ffn_conv_b[l][None, :]
        nf = dff_pad // FFN_CHUNK
        conv = jnp.concatenate([cw[:, :d_ff], cb[:, :d_ff], cw[:, d_ff:], cb[:, d_ff:]], axis=0)
        conv = _pad_cols(conv, dff_pad).reshape(conv.shape[0], nf, FFN_CHUNK).transpose(1, 0, 2)
        x2d = _ffn(x2, ffn_w1, ffn_w2, conv,
                   ln3_g[l][None, :], ln3_b[l][None, :], min(seq, 1024), seq, alpha)
    return x2d.reshape(bsz, seq, d)
```

```python
import functools

import jax
import jax.numpy as jnp
from jax import lax
from jax.experimental import pallas as pl
from jax.experimental.pallas import tpu as pltpu

F32 = jnp.float32
BF16 = jnp.bfloat16

LANES = 128
LN_EPS = 1e-5
LOG2E = 1.4426950408889634
GLA_HEADS = 4
GLA_DK = 128
GLA_DV = 256
GLA_GATE_RANK = 16
GLA_TAU = 16.0
GLA_CHUNK = 64
DIL_HD = 128
DIL_HEADS = 8
DIL_PATTERNS = ((128, 1), (512, 4), (2048, 16))
DIL_BAND = 128
DIL_UNROLL = 4
ROPE_THETA = 500000.0
ROPE_DIMS = 32
ROPE_HALF = ROPE_DIMS // 2
CA_HEADS = 4
CONV_W = 3
INPROJ_BLOCK = 1024
INPROJ_SUB = 256
WEIGHT_ROWS = 256
INPROJ_TOKENS = 512
GLA_TOKENS = 2048
OUTPROJ_TOKENS = 512
CROSS_TOKENS = 512
FFN_TOKENS = 1024
PREP_COLS = 512
MEMKV_COLS = 1024
FFN_CHUNK = 512
FFN_SUB = 512
FFN_ROWS = 1024
FFN_DOWN_SPLIT = 2
FFN_TAIL = 8
VMEM_LIMIT = 56 * 1024 * 1024


def _cparams(*sem):
    return pltpu.CompilerParams(dimension_semantics=sem, vmem_limit_bytes=VMEM_LIMIT)


def _dot(a, b):
    return jnp.dot(a, b, preferred_element_type=F32)


def _dot_nt(a, b):
    return lax.dot_general(a, b, (((1,), (1,)), ((), ())), preferred_element_type=F32)


def _dot_tn(a, b):
    return lax.dot_general(a, b, (((0,), (0,)), ((), ())), preferred_element_type=F32)


def _layer_norm(y, g, b):
    mu = jnp.mean(y, axis=-1, keepdims=True)
    d = y - mu
    var = jnp.mean(d * d, axis=-1, keepdims=True)
    return d * lax.rsqrt(var + LN_EPS) * g + b


def _sigmoid(x):
    return 1.0 / (1.0 + jnp.exp2(x * -LOG2E))


def _prep_win_kernel(wt_hbm, wcat_ref, wglr_ref, buf, gbuf, sem, *, c_glr, c_dil):
    j = pl.program_id(0)
    bn = buf.shape[1]
    rank = c_dil - c_glr

    def copy(jj, slot):
        row = jj * bn
        row = pl.multiple_of(row + jnp.where(row >= c_glr, rank, 0), 8)
        return pltpu.make_async_copy(wt_hbm.at[pl.ds(row, bn), :], buf.at[slot], sem.at[slot])

    gate_copy = pltpu.make_async_copy(wt_hbm.at[pl.ds(c_glr, rank), :], gbuf.at[pl.ds(0, rank), :],
                                      sem.at[2])

    @pl.when(j == 0)
    def _():
        copy(0, 0).start()
        gate_copy.start()
        gbuf[rank:, :] = jnp.zeros((gbuf.shape[0] - rank, gbuf.shape[1]), F32)

    slot = lax.rem(j, 2)

    @pl.when(j + 1 < pl.num_programs(0))
    def _():
        copy(j + 1, 1 - slot).start()

    copy(j, slot).wait()
    wcat_ref[...] = buf[slot].T.astype(BF16)

    @pl.when(j == 0)
    def _():
        gate_copy.wait()
        wglr_ref[...] = gbuf[...].T.astype(BF16)


def _prep_win(wt, c_glr, c_dil, bn):
    ncols, d = wt.shape
    ncat = ncols - (c_dil - c_glr)
    assert c_glr % bn == 0 and ncat % bn == 0 and (c_dil - c_glr) % 8 == 0
    return pl.pallas_call(
        functools.partial(_prep_win_kernel, c_glr=c_glr, c_dil=c_dil),
        grid=(ncat // bn,),
        in_specs=[pl.BlockSpec(memory_space=pl.ANY)],
        out_specs=[pl.BlockSpec((d, bn), lambda j: (0, j)),
                   pl.BlockSpec((d, LANES), lambda j: (0, 0))],
        out_shape=[jax.ShapeDtypeStruct((d, ncat), BF16), jax.ShapeDtypeStruct((d, LANES), BF16)],
        scratch_shapes=[pltpu.VMEM((2, bn, d), F32), pltpu.VMEM((LANES, d), F32),
                        pltpu.SemaphoreType.DMA((3,))],
        compiler_params=_cparams("arbitrary"),
        name="in_projection_weights",
    )(wt)


def _rope(t, cos, sa, sb):
    return t * cos + pltpu.roll(t, LANES - ROPE_HALF, 1) * sa + pltpu.roll(t, ROPE_HALF, 1) * sb


def _inproj_kernel(x_ref, w_ref, wg_ref, pos_ref, inv_ref, h_ref, glr_ref, xb_ref):
    bn = INPROJ_BLOCK
    xb_ref[...] = x_ref[...].astype(BF16)
    glr_ref[...] = _dot(xb_ref[...], wg_ref[...])

    ang = pos_ref[...] * inv_ref[...]
    lane = lax.broadcasted_iota(jnp.int32, ang.shape, 1)
    sin = jnp.sin(ang)
    cos = jnp.where(lane < ROPE_DIMS, jnp.cos(ang), 1.0)
    sa = jnp.where(lane < ROPE_HALF, -sin, 0.0)
    sb = jnp.where(lane < ROPE_HALF, 0.0, jnp.where(lane < ROPE_DIMS, sin, 0.0))

    def rope(scale):
        def epilogue(acc, c0):
            heads = [acc[:, j:j + LANES] for j in range(0, acc.shape[1], LANES)]
            if scale is not None:
                heads = [t * scale for t in heads]
            return jnp.concatenate([_rope(t, cos, sa, sb) for t in heads], axis=1)
        return epilogue

    def plain(acc, c0):
        return acc

    epilogues = (lambda acc, c0: acc * (GLA_DK ** -0.5) if c0 < bn // 2 else acc,
                 plain, plain, rope(DIL_HD ** -0.5), rope(None), plain)
    assert len(epilogues) * bn == w_ref.shape[1]

    for n, epilogue in enumerate(epilogues):
        for c0 in range(0, bn, INPROJ_SUB):
            cols = slice(n * bn + c0, n * bn + c0 + INPROJ_SUB)
            h_ref[:, cols] = epilogue(_dot(xb_ref[...], w_ref[:, cols]), c0).astype(h_ref.dtype)


def _inproj(x2d, wcat, wglr, pos_col, inv_row, bm):
    t, d = x2d.shape
    ncols = wcat.shape[1]
    return pl.pallas_call(
        _inproj_kernel,
        grid=(t // bm,),
        in_specs=[pl.BlockSpec((bm, d), lambda i: (i, 0)),
                  _resident((d, ncols)), _resident((d, LANES)),
                  pl.BlockSpec((bm, 1), lambda i: (i, 0)), _resident((1, LANES))],
        out_specs=[pl.BlockSpec((bm, ncols), lambda i: (i, 0)),
                   pl.BlockSpec((bm, LANES), lambda i: (i, 0))],
        out_shape=[jax.ShapeDtypeStruct((t, ncols), BF16),
                   jax.ShapeDtypeStruct((t, LANES), F32)],
        scratch_shapes=[pltpu.VMEM((bm, d), BF16)],
        compiler_params=_cparams("parallel"),
        name="in_projection",
    )(x2d, wcat, wglr, pos_col, inv_row)


def _split3(v):
    hi = v.astype(BF16)
    r1 = v - hi.astype(F32)
    mid = r1.astype(BF16)
    lo = (r1 - mid.astype(F32)).astype(BF16)
    return hi, mid, lo


def _side_cast_step(step, nsteps, srcs, dsts, in_bufs, out_bufs, sem):
    slot = lax.rem(step, 2)

    for k, (src, dst, ibuf, obuf) in enumerate(zip(srcs, dsts, in_bufs, out_bufs)):
        rows = ibuf.shape[1]
        nslice = src.shape[0] // rows
        assert src.shape[0] == nslice * rows and rows % 16 == 0 and 2 <= nslice <= nsteps

        def fetch(s, sl, src=src, ibuf=ibuf, rows=rows, k=k):
            at = pl.ds(pl.multiple_of(s * rows, 16), rows)
            return pltpu.make_async_copy(src.at[at, :], ibuf.at[sl], sem.at[0, k, sl])

        def write(s, sl, dst=dst, obuf=obuf, rows=rows, k=k):
            at = pl.ds(pl.multiple_of(s * rows, 16), rows)
            return pltpu.make_async_copy(obuf.at[sl], dst.at[at, :], sem.at[1, k, sl])

        @pl.when(step == 0)
        def _():
            fetch(0, 0).start()

        @pl.when(step + 1 < nslice)
        def _():
            fetch(step + 1, 1 - slot).start()

        @pl.when((step >= 2) & (step - 2 < nslice))
        def _():
            write(step - 2, slot).wait()

        @pl.when(step < nslice)
        def _():
            fetch(step, slot).wait()
            obuf[slot] = ibuf[slot].astype(BF16)
            write(step, slot).start()

        for s in range(max(nsteps - 2, 0), nsteps):
            if s < nslice:
                @pl.when(step == nsteps - 1)
                def _():
                    write(s, s % 2).wait()


def _side_cast_rows(total, nsteps):
    return next(r for r in range(16, total + 1, 16) if total % r == 0 and total // r <= nsteps)


def _gla_kernel(q_ref, k_ref, v_ref, r_ref, glr_ref, w2_ref, gb_ref, ng_ref, wa_hbm, wb_hbm,
                o_ref, wa_out, wb_out, st_ref, wa_in, wb_in, wa_cast, wb_cast, cast_sem,
                *, layer, nsteps):
    c = GLA_CHUNK
    step = (pl.program_id(0) * pl.num_programs(1) + pl.program_id(1)) * pl.num_programs(2) \
        + pl.program_id(2)
    _side_cast_step(step, nsteps, (wa_hbm.at[layer], wb_hbm.at[layer]), (wa_out, wb_out),
                    (wa_in, wb_in), (wa_cast, wb_cast), cast_sem)
    sb = q_ref.shape[1]
    grp = 4 * c

    @pl.when(pl.program_id(2) == 0)
    def _():
        st_ref[...] = jnp.zeros_like(st_ref)

    z = _dot(glr_ref[0].astype(BF16), w2_ref[...]) + gb_ref[...]
    lg = (jnp.minimum(z, 0.0) - jnp.log(1.0 + jnp.exp(-jnp.abs(z)))) / GLA_TAU

    row = lax.broadcasted_iota(jnp.int32, (grp, grp), 0)
    col = lax.broadcasted_iota(jnp.int32, (grp, grp), 1)
    shift = c.bit_length() - 1
    tril = jnp.where(((row >> shift) == (col >> shift)) & (col <= row), 1.0, 0.0).astype(BF16)
    b_parts = []
    for g0 in range(0, sb, grp):
        pieces = jnp.concatenate(_split3(lg[g0:g0 + grp]), axis=1)
        res = _dot(tril, pieces)
        b_parts.append(res[:, :LANES] + res[:, LANES:2 * LANES] + res[:, 2 * LANES:])
    b = jnp.concatenate(b_parts, axis=0)

    qf = q_ref[0].astype(F32)
    kf = k_ref[0].astype(F32)
    q_in = (qf * jnp.exp(b)).astype(BF16)
    k_in = (kf * jnp.exp(-b)).astype(BF16)
    v = v_ref[0]

    causal = ((row >> shift) == (col >> shift)) & (col <= row)
    intra = []
    for g0 in range(0, sb, grp):
        rows = slice(g0, g0 + grp)
        a = jnp.where(causal, _dot_nt(q_in[rows], k_in[rows]), 0.0).astype(BF16)
        intra.append(_dot(a, v[rows]))

    st = st_ref[...]
    outs = []
    for i in range(sb // c):
        rows = slice(i * c, (i + 1) * c)
        o = intra[i * c // grp][i * c % grp:i * c % grp + c] + _dot_nt(q_in[rows], st.astype(BF16))
        outs.append(o)
        decay = jnp.exp(b[i * c + c - 1:i * c + c, :])
        st = (st + _dot_tn(v[rows], k_in[rows])) * decay
    st_ref[...] = st

    o = jnp.concatenate(outs, axis=0)
    mu = jnp.mean(o, axis=-1, keepdims=True)
    d = o - mu
    var = jnp.mean(d * d, axis=-1, keepdims=True)
    rg = r_ref[0].astype(F32)
    y = d * lax.rsqrt(var + LN_EPS) * ng_ref[...] * (rg * _sigmoid(rg))
    o_ref[0] = y.astype(o_ref.dtype)


def _gla(h3, glr3, w2p, gb, ng, sb, side_a, side_b, layer):
    bsz, s, _ = h3.shape
    nsteps = bsz * GLA_HEADS * (s // sb)
    ra, rb_ = (_side_cast_rows(a.shape[1], nsteps) for a in (side_a, side_b))
    hbm = pl.BlockSpec(memory_space=pl.ANY)
    kb = GLA_HEADS * GLA_DK // LANES
    vb = 2 * GLA_HEADS * GLA_DK // GLA_DV
    rb = vb + GLA_HEADS
    return pl.pallas_call(
        functools.partial(_gla_kernel, layer=layer, nsteps=nsteps),
        grid=(bsz, GLA_HEADS, s // sb),
        in_specs=[pl.BlockSpec((1, sb, GLA_DK), lambda b, h, j: (b, j, h)),
                  pl.BlockSpec((1, sb, GLA_DK), lambda b, h, j: (b, j, kb + h)),
                  pl.BlockSpec((1, sb, GLA_DV), lambda b, h, j: (b, j, vb + h)),
                  pl.BlockSpec((1, sb, GLA_DV), lambda b, h, j: (b, j, rb + h)),
                  pl.BlockSpec((1, sb, LANES), lambda b, h, j: (b, j, 0)),
                  pl.BlockSpec((LANES, GLA_DK), lambda b, h, j: (0, h)),
                  pl.BlockSpec((1, GLA_DK), lambda b, h, j: (0, h)),
                  pl.BlockSpec((1, GLA_DV), lambda b, h, j: (0, h)), hbm, hbm],
        out_specs=[pl.BlockSpec((1, sb, GLA_DV), lambda b, h, j: (b, j, h)), hbm, hbm],
        out_shape=[jax.ShapeDtypeStruct((bsz, s, GLA_HEADS * GLA_DV), BF16),
                   jax.ShapeDtypeStruct(side_a.shape[1:], BF16),
                   jax.ShapeDtypeStruct(side_b.shape[1:], BF16)],
        scratch_shapes=[pltpu.VMEM((GLA_DV, GLA_DK), F32),
                        pltpu.VMEM((2, ra, side_a.shape[2]), F32), pltpu.VMEM((2, rb_, side_b.shape[2]), F32),
                        pltpu.VMEM((2, ra, side_a.shape[2]), BF16), pltpu.VMEM((2, rb_, side_b.shape[2]), BF16),
                        pltpu.SemaphoreType.DMA((2, 2, 2))],
        compiler_params=_cparams("arbitrary", "arbitrary", "arbitrary"),
        name="gla",
    )(h3, h3, h3, h3, glr3, w2p, gb, ng, side_a, side_b)


def _dil_kernel(q_ref, k_ref, v_ref, o_ref, qf, kf, vf, qg, kg, vg, qc, kc, vc, ob, db, mx,
                scb, eb):
    s = q_ref.shape[1]
    band = DIL_BAND
    unroll = DIL_UNROLL
    nblk = s // band
    qf[...] = q_ref[0].astype(F32)
    kf[...] = k_ref[0].astype(F32)
    vf[...] = v_ref[0].astype(F32)
    kc[:band, :] = jnp.zeros((band, DIL_HD), BF16)
    vc[:band, :] = jnp.zeros((band, DIL_HD), BF16)

    qi = lax.broadcasted_iota(jnp.int32, (band, 2 * band), 0)
    kj = lax.broadcasted_iota(jnp.int32, (band, 2 * band), 1)
    allowed = (kj >= qi) & (kj <= qi + band)
    bias = jnp.where(allowed, 0.0, -jnp.inf).astype(F32)
    bias0 = jnp.where(allowed & (kj >= band), 0.0, -jnp.inf).astype(F32)
    ones = jnp.ones((2 * band, LANES), BF16)

    for p, (window, dil) in enumerate(DIL_PATTERNS):
        assert window // dil == band
        cls = s // dil
        nb = cls // band
        span = band * dil
        assert nblk % unroll == 0 and (nb % unroll == 0 or unroll % nb == 0)

        if dil == 1:
            qc[...] = q_ref[0]
            kc[band:, :] = k_ref[0]
            vc[band:, :] = v_ref[0]
        else:
            prev = DIL_PATTERNS[p - 1][1]
            step = dil // prev
            assert step * prev == dil and step in (2, 4)
            keep = p + 1 < len(DIL_PATTERNS)
            srcs, dsts = ((qf, kf, vf), (qg, kg, vg)) if p % 2 == 1 else ((qg, kg, vg), (qf, kf, vf))
            for r in range(dil):
                rows = pl.ds((r % prev) * (s // prev) + r // prev, cls, stride=step)
                for src, dst, cm, off in zip(srcs, dsts, (qc, kc, vc), (0, band, band)):
                    x = src[rows, :]
                    if keep:
                        dst[r * cls:(r + 1) * cls, :] = x
                    cm[off + r * cls:off + (r + 1) * cls, :] = x.astype(BF16)

        def out_rows(g, lo=0, cnt=band, dil=dil, nb=nb, span=span):
            start = g // nb + (g % nb) * span + lo * dil
            return pl.ds(start, cnt) if dil == 1 else pl.ds(start, cnt, stride=dil)

        def scores(t, slot):
            for u in range(unroll):
                g = t * unroll + u
                k2 = kc[g * band:(g + 2) * band, :]
                scb[slot, u * band:(u + 1) * band, :] = _dot_nt(qc[g * band:(g + 1) * band, :], k2)

        def softmax(t, slot, p=p, nb=nb, out_rows=out_rows):
            for u in range(unroll):
                g = t * unroll + u
                bb = bias0 if g % nb == 0 else bias
                rows = slice(u * band, (u + 1) * band)
                m = jnp.max(scb[slot, rows, :] + bb, axis=-1, keepdims=True)
                mx[p, out_rows(g), :] = jnp.broadcast_to(m, (band, LANES))
                for half in range(2):
                    cols = slice(half * band, (half + 1) * band)
                    eb[slot, rows, cols] = jnp.exp(scb[slot, rows, cols] + bb[:, cols] - m).astype(BF16)

        def values(t, slot, p=p, out_rows=out_rows):
            for u in range(unroll):
                g = t * unroll + u
                v2 = vc[g * band:(g + 2) * band, :]
                oe = _dot(eb[slot, u * band:(u + 1) * band, :],
                          jnp.concatenate([v2, ones], axis=1))
                ob[p, out_rows(g), :] = oe[:, :DIL_HD]
                db[p, out_rows(g), :] = oe[:, DIL_HD:]

        ngrp = nblk // unroll
        for t in range(ngrp + 2):
            if t < ngrp:
                scores(t, t % 2)
            if 1 <= t <= ngrp:
                softmax(t - 1, (t - 1) % 2)
            if t >= 2:
                values(t - 2, t % 2)

    mb = 512

    def merge(i, carry):
        rs = pl.ds(pl.multiple_of(i * mb, mb), mb)
        m0, m1, m2 = mx[0, rs, :], mx[1, rs, :], mx[2, rs, :]
        m = jnp.maximum(jnp.maximum(m0, m1), m2)
        e0, e1, e2 = jnp.exp(m0 - m), jnp.exp(m1 - m), jnp.exp(m2 - m)
        num = e0 * ob[0, rs, :] + e1 * ob[1, rs, :] + e2 * ob[2, rs, :]
        den = e0 * db[0, rs, :] + e1 * db[1, rs, :] + e2 * db[2, rs, :]
        o_ref[0, rs, :] = (num / den).astype(o_ref.dtype)
        return carry

    lax.fori_loop(0, s // mb, merge, 0)


def _dil(h3):
    bsz, s, _ = h3.shape
    qb = (2 * GLA_HEADS * GLA_DK + 2 * GLA_HEADS * GLA_DV) // DIL_HD
    kb = qb + DIL_HEADS
    vb = kb + DIL_HEADS
    npat = len(DIL_PATTERNS)
    return pl.pallas_call(
        _dil_kernel,
        grid=(bsz, DIL_HEADS),
        in_specs=[pl.BlockSpec((1, s, DIL_HD), lambda b, h: (b, 0, qb + h)),
                  pl.BlockSpec((1, s, DIL_HD), lambda b, h: (b, 0, kb + h)),
                  pl.BlockSpec((1, s, DIL_HD), lambda b, h: (b, 0, vb + h))],
        out_specs=pl.BlockSpec((1, s, DIL_HD), lambda b, h: (b, 0, h)),
        out_shape=jax.ShapeDtypeStruct((bsz, s, DIL_HEADS * DIL_HD), BF16),
        scratch_shapes=[pltpu.VMEM((s, DIL_HD), F32)] * 6 + [
                        pltpu.VMEM((s, DIL_HD), BF16), pltpu.VMEM((s + DIL_BAND, DIL_HD), BF16),
                        pltpu.VMEM((s + DIL_BAND, DIL_HD), BF16),
                        pltpu.VMEM((npat, s, DIL_HD), F32), pltpu.VMEM((npat, s, LANES), F32),
                        pltpu.VMEM((npat, s, LANES), F32),
                        pltpu.VMEM((2, DIL_UNROLL * DIL_BAND, 2 * DIL_BAND), F32),
                        pltpu.VMEM((2, DIL_UNROLL * DIL_BAND, 2 * DIL_BAND), BF16)],
        compiler_params=_cparams("parallel", "parallel"),
        name="dilated_attention",
    )(h3, h3, h3)


def _resident(shape):
    return pl.BlockSpec(shape, lambda *_: (0,) * len(shape), pipeline_mode=pl.Buffered(1))


def _load_weight_bf16(w_hbm, dst_ref, stage_ref, sem):
    rows = stage_ref.shape[1]
    nchunk = w_hbm.shape[0] // rows
    copies = [pltpu.make_async_copy(w_hbm.at[pl.ds(c * rows, rows), :], stage_ref.at[c % 2],
                                    sem.at[c % 2]) for c in range(nchunk)]
    copies[0].start()
    for c in range(nchunk):
        if c + 1 < nchunk:
            copies[c + 1].start()
        copies[c].wait()
        dst_ref[c * rows:(c + 1) * rows, :] = stage_ref[c % 2].astype(BF16)


def _weight_stage(cols):
    return [pltpu.VMEM((2, WEIGHT_ROWS, cols), F32), pltpu.SemaphoreType.DMA((2,))]


def _outproj_kernel(og_ref, od_ref, x_ref, g_ref, b_ref, w_hbm, o_ref, w_ref, stage_ref, sem,
                    *, alpha):
    @pl.when(pl.program_id(0) == 0)
    def _():
        _load_weight_bf16(w_hbm, w_ref, stage_ref, sem)

    ka = og_ref.shape[1]
    acc = _dot(og_ref[...], w_ref[:ka, :]) + _dot(od_ref[...], w_ref[ka:, :])
    o_ref[...] = _layer_norm(alpha * x_ref[...] + acc, g_ref[...], b_ref[...])


def _outproj(og, od, w, x2d, g, b, bm, alpha):
    t, d = x2d.shape
    ka, kb = og.shape[1], od.shape[1]
    assert w.shape == (ka + kb, d)
    return pl.pallas_call(
        functools.partial(_outproj_kernel, alpha=alpha),
        grid=(t // bm,),
        in_specs=[pl.BlockSpec((bm, ka), lambda i: (i, 0)),
                  pl.BlockSpec((bm, kb), lambda i: (i, 0)),
                  pl.BlockSpec((bm, d), lambda i: (i, 0)),
                  _resident((1, d)), _resident((1, d)),
                  pl.BlockSpec(memory_space=pl.ANY)],
        out_specs=pl.BlockSpec((bm, d), lambda i: (i, 0)),
        out_shape=jax.ShapeDtypeStruct((t, d), F32),
        scratch_shapes=[pltpu.VMEM((ka + kb, d), BF16)] + _weight_stage(d),
        compiler_params=_cparams("arbitrary"),
        name="out_projection_ln1",
    )(og, od, x2d, g, b, w)


def _memkv_kernel(m_ref, w_ref, o_ref):
    o_ref[...] = _dot(m_ref[...].astype(BF16), w_ref[...].astype(BF16)).astype(o_ref.dtype)


def _memkv(mem2d, wkv, bn):
    t, d = mem2d.shape
    n = wkv.shape[1]
    return pl.pallas_call(
        _memkv_kernel,
        grid=(n // bn,),
        in_specs=[pl.BlockSpec((t, d), lambda j: (0, 0)),
                  pl.BlockSpec((d, bn), lambda j: (0, j))],
        out_specs=pl.BlockSpec((t, bn), lambda j: (0, j)),
        out_shape=jax.ShapeDtypeStruct((t, n), BF16),
        compiler_params=_cparams("parallel"),
        name="memory_kv_projection",
    )(mem2d, wkv)


def _cross_kernel(x_ref, kv_ref, g_ref, b_ref, wq_hbm, wo_hbm, o_ref,
                  wq_ref, wo_ref, oc_ref, stage_ref, sem, *, alpha):
    @pl.when(pl.program_id(0) == 0)
    def _():
        _load_weight_bf16(wq_hbm, wq_ref, stage_ref, sem)
        _load_weight_bf16(wo_hbm, wo_ref, stage_ref, sem)

    d = x_ref.shape[1]
    hd = d // CA_HEADS
    x = x_ref[...]
    q = _dot(x.astype(BF16), wq_ref[...]).astype(BF16)
    for h in range(CA_HEADS):
        cols = slice(h * hd, (h + 1) * hd)
        mk = kv_ref[0, :, cols]
        mv = kv_ref[0, :, d + h * hd:d + (h + 1) * hd]
        sc = _dot_nt(q[:, cols], mk) * (hd ** -0.5)
        m = jnp.max(sc, axis=-1, keepdims=True)
        e = jnp.exp(sc - m)
        p = e / jnp.sum(e, axis=-1, keepdims=True)
        oc_ref[:, cols] = _dot(p.astype(BF16), mv).astype(BF16)
    y = alpha * x + _dot(oc_ref[...], wo_ref[...])
    o_ref[...] = _layer_norm(y, g_ref[...], b_ref[...])


def _cross(x1, wq, kv3, wo, g, b, bm, seq, alpha):
    t, d = x1.shape
    m = kv3.shape[1]
    per = seq // bm
    hbm = pl.BlockSpec(memory_space=pl.ANY)
    return pl.pallas_call(
        functools.partial(_cross_kernel, alpha=alpha),
        grid=(t // bm,),
        in_specs=[pl.BlockSpec((bm, d), lambda i: (i, 0)),
                  pl.BlockSpec((1, m, 2 * d), lambda i: (i // per, 0, 0)),
                  _resident((1, d)), _resident((1, d)), hbm, hbm],
        out_specs=pl.BlockSpec((bm, d), lambda i: (i, 0)),
        out_shape=jax.ShapeDtypeStruct((t, d), F32),
        scratch_shapes=[pltpu.VMEM((d, d), BF16), pltpu.VMEM((d, d), BF16),
                        pltpu.VMEM((bm, d), BF16)] + _weight_stage(d),
        compiler_params=_cparams("arbitrary"),
        name="cross_attention_ln2",
    )(x1, kv3, g, b, wq, wo)


def _causal_conv(u_ref, cw, cb, r0, n):
    h = FFN_TAIL + r0
    y = cb + cw[0:1, :] * u_ref[h - 2:h - 2 + n, :]
    y = y + cw[1:2, :] * u_ref[h - 1:h - 1 + n, :]
    return y + cw[2:3, :] * u_ref[h:h + n, :]


def _ffn_kernel(cp_ref, g_ref, b_ref, x_hbm, w1_hbm, w2_hbm, o_ref,
                x_buf, xb_ref, act0_ref, act1_ref, us_ref, tail_ref, wgu_buf, w2_buf,
                sem, x_sem, *, alpha, per):
    i = pl.program_id(0)
    d_ff = w2_hbm.shape[0]
    bm = x_buf.shape[0]
    bf = wgu_buf.shape[2] // 2
    nf = pl.cdiv(d_ff, bf)
    last_w = d_ff - (nf - 1) * bf
    assert nf % 2 == 1 and nf >= 3
    assert bf % FFN_SUB == 0 and last_w % LANES == 0 and bm % FFN_ROWS == 0
    c_first = i * nf
    acts = (act0_ref, act1_ref)
    nslab = bf // LANES
    half = o_ref.shape[1] // FFN_DOWN_SPLIT

    def width(f):
        return last_w if isinstance(f, int) and f == nf - 1 else bf

    def up_copies(f, slot):
        w = width(f)
        col = f * bf if isinstance(f, int) else pl.multiple_of(f * bf, bf)
        return (pltpu.make_async_copy(w1_hbm.at[:, pl.ds(col, w)],
                                      wgu_buf.at[slot, :, pl.ds(0, w)], sem.at[0, slot]),
                pltpu.make_async_copy(w1_hbm.at[:, pl.ds(d_ff + col, w)],
                                      wgu_buf.at[slot, :, pl.ds(bf, w)], sem.at[1, slot]))

    def down_copy(f, slot):
        w = width(f)
        row = f * bf if isinstance(f, int) else pl.multiple_of(f * bf, bf)
        return pltpu.make_async_copy(w2_hbm.at[pl.ds(row, w), :],
                                     w2_buf.at[slot, pl.ds(0, w), :], sem.at[2, slot])

    def x_copy(step):
        row = pl.multiple_of(step * bm, bm)
        return pltpu.make_async_copy(x_hbm.at[pl.ds(row, bm), :], x_buf, x_sem.at[0])

    def region_copies(f):
        slot = lax.rem(c_first + f, 2)
        for cp in up_copies(f, slot):
            cp.wait()
        if isinstance(f, int) and f == 0:
            @pl.when(i + 1 < pl.num_programs(0))
            def _():
                x_copy(i + 1).start()
        if not (isinstance(f, int) and f == 0):
            down_copy(f - 1, 1 - slot).wait()
        if isinstance(f, int) and f == nf - 1:
            @pl.when(i + 1 < pl.num_programs(0))
            def _():
                for cp in up_copies(0, 1 - slot):
                    cp.start()
        else:
            for cp in up_copies(f + 1, 1 - slot):
                cp.start()
        down_copy(f, slot).start()
        return slot

    def up_matmul(slot, c0, wcols, r0):
        xr = xb_ref[r0:r0 + FFN_ROWS, :]
        rows = slice(FFN_TAIL + r0, FFN_TAIL + r0 + FFN_ROWS)
        if wcols == bf:
            res = _dot(xr, wgu_buf[slot])
            for s in range(2 * nslab):
                us_ref[s, rows, :] = res[:, s * LANES:(s + 1) * LANES]
            return
        for base in (0, nslab):
            res = _dot(xr, wgu_buf[slot, :, base * LANES + c0:base * LANES + c0 + wcols])
            for k in range(wcols // LANES):
                us_ref[base + c0 // LANES + k, rows, :] = res[:, k * LANES:(k + 1) * LANES]

    def activate(cv, act_ref, c0, wcols, r0):
        for s in range(c0 // LANES, (c0 + wcols) // LANES):
            cols = slice(s * LANES, (s + 1) * LANES)
            gate = _causal_conv(us_ref.at[s], cv[0:3, cols], cv[3:4, cols], r0, FFN_ROWS)
            up = _causal_conv(us_ref.at[nslab + s], cv[4:7, cols], cv[7:8, cols], r0, FFN_ROWS)
            act_ref[r0:r0 + FFN_ROWS, cols] = (gate * _sigmoid(gate) * up).astype(BF16)

    def down_matmul(slot, act_ref, w, n0):
        o_ref[:, n0:n0 + half] += _dot(act_ref[:, :w], w2_buf[slot, :w, n0:n0 + half])

    def region(f, act_slot):
        slot = region_copies(f)
        w = width(f)
        cv = cp_ref[f]
        act_ref, prev_ref = acts[act_slot], acts[1 - act_slot]
        slabs = [s for base in (0, nslab) for s in range(base, base + w // LANES)]
        for s in slabs:
            us_ref[s, :FFN_TAIL, :] = tail_ref[f, :, s * LANES:(s + 1) * LANES]
        units = [(c0, min(FFN_SUB, w - c0), r0) for c0 in range(0, w, FFN_SUB)
                 for r0 in range(0, bm, FFN_ROWS)]
        for unit in units:
            up_matmul(slot, *unit)
        if not (isinstance(f, int) and f == 0):
            for n0 in range(0, o_ref.shape[1], half):
                down_matmul(1 - slot, prev_ref, width(f - 1) if isinstance(f, int) else bf, n0)
        for unit in units:
            activate(cv, act_ref, *unit)
        for s in slabs:
            tail_ref[f, :, s * LANES:(s + 1) * LANES] = us_ref[s, bm:bm + FFN_TAIL, :]

    @pl.when(i == 0)
    def _():
        x_copy(0).start()
        for cp in up_copies(0, 0):
            cp.start()

    @pl.when(i % per == 0)
    def _():
        tail_ref[...] = jnp.zeros_like(tail_ref)

    x_copy(i).wait()
    xb_ref[...] = x_buf[...].astype(BF16)
    o_ref[...] = alpha * x_buf[...]

    region(0, 0)

    def pair(j, carry):
        region(2 * j + 1, 1)
        region(2 * j + 2, 0)
        return carry

    lax.fori_loop(0, (nf - 3) // 2, pair, 0)
    region(nf - 2, 1)
    region(nf - 1, 0)

    last = lax.rem(c_first + nf - 1, 2)
    down_copy(nf - 1, last).wait()
    for n0 in range(0, o_ref.shape[1], half):
        down_matmul(last, acts[0], last_w, n0)
    o_ref[...] = _layer_norm(o_ref[...], g_ref[...], b_ref[...])


def _ffn(x2, w1, w2, cp, g, b, bm, seq, alpha):
    t, d = x2.shape
    bf = FFN_CHUNK
    nf = cp.shape[0]
    hbm = pl.BlockSpec(memory_space=pl.ANY)
    return pl.pallas_call(
        functools.partial(_ffn_kernel, alpha=alpha, per=seq // bm),
        grid=(t // bm,),
        in_specs=[_resident(cp.shape), _resident((1, d)), _resident((1, d)), hbm, hbm, hbm],
        out_specs=pl.BlockSpec((bm, d), lambda i: (i, 0)),
        out_shape=jax.ShapeDtypeStruct((t, d), F32),
        scratch_shapes=[pltpu.VMEM((bm, d), F32), pltpu.VMEM((bm, d), BF16),
                        pltpu.VMEM((bm, bf), BF16), pltpu.VMEM((bm, bf), BF16),
                        pltpu.VMEM((2 * bf // LANES, FFN_TAIL + bm, LANES), F32),
                        pltpu.VMEM((nf, FFN_TAIL, 2 * bf), F32),
                        pltpu.VMEM((2, d, 2 * bf), BF16),
                        pltpu.VMEM((2, bf, d), BF16), pltpu.SemaphoreType.DMA((3, 2)),
                        pltpu.SemaphoreType.DMA((1,))],
        compiler_params=_cparams("arbitrary"),
        name="conv_ffn_ln3",
    )(cp, g, b, x2, w1, w2)


def _pad_cols(a, n):
    return jnp.pad(a, ((0, 0), (0, n - a.shape[1])))


def kernel(x, mem, positions, w_in, gla_gate_w2, gla_gate_b, gla_norm_g, w_out, ln1_g, ln1_b,
           ca_wq, ca_wkv, ca_wo, ln2_g, ln2_b, ffn_w_in, ffn_conv_w, ffn_conv_b, ffn_w_out,
           ln3_g, ln3_b):
    bsz, seq, d = x.shape
    depth = w_in.shape[0]
    t = bsz * seq
    alpha = (2.0 * depth) ** 0.25
    d_ff = ffn_w_out.shape[1]
    dff_pad = -(-d_ff // FFN_CHUNK) * FFN_CHUNK

    nqk = GLA_HEADS * GLA_DK
    nv = GLA_HEADS * GLA_DV
    c_glr = 2 * nqk + 2 * nv
    c_dil = c_glr + GLA_GATE_RANK

    half = ROPE_HALF
    inv_freq = ROPE_THETA ** (-jnp.arange(0, ROPE_DIMS, 2, dtype=F32) / ROPE_DIMS)
    inv_row = jnp.concatenate([inv_freq, inv_freq, jnp.zeros((LANES - 2 * half,), F32)])[None, :]
    pos_col = positions.astype(F32).reshape(t, 1)

    x2d = x.reshape(t, d)
    for l in range(depth):
        wcat, wglr = _prep_win(jnp.swapaxes(w_in[l], 0, 1), c_glr, c_dil, PREP_COLS)
        h, glr = _inproj(x2d, wcat, wglr, pos_col, inv_row, min(t, INPROJ_TOKENS))
        h3 = h.reshape(bsz, seq, h.shape[1])

        w2p = jnp.pad(gla_gate_w2[l], ((0, LANES - GLA_GATE_RANK), (0, 0))).astype(BF16)
        og, ffn_w1, ffn_w2 = _gla(h3, glr.reshape(bsz, seq, LANES), w2p, gla_gate_b[l][None, :],
                                  gla_norm_g[l][None, :], min(seq, GLA_TOKENS), ffn_w_in, ffn_w_out, l)
        od = _dil(h3)

        x1 = _outproj(og.reshape(t, nv), od.reshape(t, DIL_HEADS * DIL_HD), w_out[l],
                      x2d, ln1_g[l][None, :], ln1_b[l][None, :], min(t, OUTPROJ_TOKENS), alpha)

        kv = _memkv(mem.reshape(-1, d), ca_wkv[l], MEMKV_COLS)
        x2 = _cross(x1, ca_wq[l], kv.reshape(bsz, -1, 2 * d), ca_wo[l],
                    ln2_g[l][None, :], ln2_b[l][None, :], min(seq, CROSS_TOKENS), seq, alpha)

        cw = ffn_conv_w[l]
        cb = ffn_conv_b[l][None, :]
        nf = dff_pad // FFN_CHUNK
        conv = jnp.concatenate([cw[:, :d_ff], cb[:, :d_ff], cw[:, d_ff:], cb[:, d_ff:]], axis=0)
        conv = _pad_cols(conv, dff_pad).reshape(conv.shape[0], nf, FFN_CHUNK).transpose(1, 0, 2)
        x2d = _ffn(x2, ffn_w1, ffn_w2, conv,
                   ln3_g[l][None, :], ln3_b[l][None, :], min(seq, FFN_TOKENS), seq, alpha)
    return x2d.reshape(bsz, seq, d)
```

```python
import functools

import jax
import jax.numpy as jnp
from jax import lax
from jax.experimental import pallas as pl
from jax.experimental.pallas import tpu as pltpu

F32 = jnp.float32
BF16 = jnp.bfloat16

LANES = 128
LN_EPS = 1e-5
LOG2E = 1.4426950408889634
GLA_HEADS = 4
GLA_DK = 128
GLA_DV = 256
GLA_GATE_RANK = 16
GLA_TAU = 16.0
GLA_CHUNK = 64
DIL_HD = 128
DIL_HEADS = 8
DIL_PATTERNS = ((128, 1), (512, 4), (2048, 16))
DIL_BAND = 128
DIL_UNROLL = 4
ROPE_THETA = 500000.0
ROPE_DIMS = 32
ROPE_HALF = ROPE_DIMS // 2
CA_HEADS = 4
CONV_W = 3
INPROJ_BLOCK = 1024
INPROJ_SUB = 256
WEIGHT_ROWS = 256
INPROJ_TOKENS = 512
GLA_TOKENS = 2048
OUTPROJ_TOKENS = 512
CROSS_TOKENS = 512
FFN_TOKENS = 1024
PREP_COLS = 512
MEMKV_COLS = 1024
FFN_CHUNK = 512
FFN_SUB = 512
FFN_ROWS = 1024
FFN_DOWN_SPLIT = 2
FFN_TAIL = 8
VMEM_LIMIT = 56 * 1024 * 1024


def _cparams(*sem):
    return pltpu.CompilerParams(dimension_semantics=sem, vmem_limit_bytes=VMEM_LIMIT)


def _dot(a, b):
    return jnp.dot(a, b, preferred_element_type=F32)


def _dot_nt(a, b):
    return lax.dot_general(a, b, (((1,), (1,)), ((), ())), preferred_element_type=F32)


def _dot_tn(a, b):
    return lax.dot_general(a, b, (((0,), (0,)), ((), ())), preferred_element_type=F32)


def _layer_norm(y, g, b):
    mu = jnp.mean(y, axis=-1, keepdims=True)
    d = y - mu
    var = jnp.mean(d * d, axis=-1, keepdims=True)
    return d * lax.rsqrt(var + LN_EPS) * g + b


def _sigmoid(x):
    return 1.0 / (1.0 + jnp.exp2(x * -LOG2E))


def _prep_win_kernel(wt_hbm, wcat_ref, wglr_ref, buf, gbuf, sem, *, c_glr, c_dil):
    j = pl.program_id(0)
    bn = buf.shape[1]
    rank = c_dil - c_glr

    def copy(jj, slot):
        row = jj * bn
        row = pl.multiple_of(row + jnp.where(row >= c_glr, rank, 0), 8)
        return pltpu.make_async_copy(wt_hbm.at[pl.ds(row, bn), :], buf.at[slot], sem.at[slot])

    gate_copy = pltpu.make_async_copy(wt_hbm.at[pl.ds(c_glr, rank), :], gbuf.at[pl.ds(0, rank), :],
                                      sem.at[2])

    @pl.when(j == 0)
    def _():
        copy(0, 0).start()
        gate_copy.start()
        gbuf[rank:, :] = jnp.zeros((gbuf.shape[0] - rank, gbuf.shape[1]), F32)

    slot = lax.rem(j, 2)

    @pl.when(j + 1 < pl.num_programs(0))
    def _():
        copy(j + 1, 1 - slot).start()

    copy(j, slot).wait()
    wcat_ref[...] = buf[slot].T.astype(BF16)

    @pl.when(j == 0)
    def _():
        gate_copy.wait()
        wglr_ref[...] = gbuf[...].T.astype(BF16)


def _prep_win(wt, c_glr, c_dil, bn):
    ncols, d = wt.shape
    ncat = ncols - (c_dil - c_glr)
    assert c_glr % bn == 0 and ncat % bn == 0 and (c_dil - c_glr) % 8 == 0
    return pl.pallas_call(
        functools.partial(_prep_win_kernel, c_glr=c_glr, c_dil=c_dil),
        grid=(ncat // bn,),
        in_specs=[pl.BlockSpec(memory_space=pl.ANY)],
        out_specs=[pl.BlockSpec((d, bn), lambda j: (0, j)),
                   pl.BlockSpec((d, LANES), lambda j: (0, 0))],
        out_shape=[jax.ShapeDtypeStruct((d, ncat), BF16), jax.ShapeDtypeStruct((d, LANES), BF16)],
        scratch_shapes=[pltpu.VMEM((2, bn, d), F32), pltpu.VMEM((LANES, d), F32),
                        pltpu.SemaphoreType.DMA((3,))],
        compiler_params=_cparams("arbitrary"),
        name="in_projection_weights",
    )(wt)


def _rope(t, cos, sa, sb):
    return t * cos + pltpu.roll(t, LANES - ROPE_HALF, 1) * sa + pltpu.roll(t, ROPE_HALF, 1) * sb


def _inproj_kernel(x_ref, w_ref, wg_ref, pos_ref, inv_ref, h_ref, glr_ref, xb_ref):
    bn = INPROJ_BLOCK
    xb_ref[...] = x_ref[...].astype(BF16)
    glr_ref[...] = _dot(xb_ref[...], wg_ref[...])

    bm = x_ref.shape[0]
    angp = pos_ref[...] * inv_ref[...]

    def spread(p):
        rep = jnp.broadcast_to(p[:, None, :], (bm // 8, 8, LANES)).reshape(bm, LANES)
        first = pltpu.roll(rep, 0, 1, stride=ROPE_HALF, stride_axis=0)
        return first, pltpu.roll(first, ROPE_HALF, 1)

    lane = lax.broadcasted_iota(jnp.int32, (bm, LANES), 1)
    c_lo, c_hi = spread(jnp.cos(angp))
    s_lo, s_hi = spread(jnp.sin(angp))
    cos = jnp.where(lane < ROPE_HALF, c_lo, jnp.where(lane < ROPE_DIMS, c_hi, 1.0))
    sa = jnp.where(lane < ROPE_HALF, -s_lo, 0.0)
    sb = jnp.where(lane < ROPE_HALF, 0.0, jnp.where(lane < ROPE_DIMS, s_hi, 0.0))

    def rope(scale):
        def epilogue(acc, c0):
            heads = [acc[:, j:j + LANES] for j in range(0, acc.shape[1], LANES)]
            if scale is not None:
                heads = [t * scale for t in heads]
            return jnp.concatenate([_rope(t, cos, sa, sb) for t in heads], axis=1)
        return epilogue

    def plain(acc, c0):
        return acc

    epilogues = (lambda acc, c0: acc * (GLA_DK ** -0.5) if c0 < bn // 2 else acc,
                 plain, plain, rope(DIL_HD ** -0.5), rope(None), plain)
    assert len(epilogues) * bn == w_ref.shape[1]

    for n, epilogue in enumerate(epilogues):
        for c0 in range(0, bn, INPROJ_SUB):
            cols = slice(n * bn + c0, n * bn + c0 + INPROJ_SUB)
            h_ref[:, cols] = epilogue(_dot(xb_ref[...], w_ref[:, cols]), c0).astype(h_ref.dtype)


def _inproj(x2d, wcat, wglr, pos_col, inv_row, bm):
    t, d = x2d.shape
    ncols = wcat.shape[1]
    return pl.pallas_call(
        _inproj_kernel,
        grid=(t // bm,),
        in_specs=[pl.BlockSpec((bm, d), lambda i: (i, 0)),
                  _resident((d, ncols)), _resident((d, LANES)),
                  pl.BlockSpec((bm // 8, LANES), lambda i: (i, 0)), _resident((1, LANES))],
        out_specs=[pl.BlockSpec((bm, ncols), lambda i: (i, 0)),
                   pl.BlockSpec((bm, LANES), lambda i: (i, 0))],
        out_shape=[jax.ShapeDtypeStruct((t, ncols), BF16),
                   jax.ShapeDtypeStruct((t, LANES), F32)],
        scratch_shapes=[pltpu.VMEM((bm, d), BF16)],
        compiler_params=_cparams("parallel"),
        name="in_projection",
    )(x2d, wcat, wglr, pos_col, inv_row)


def _split3(v):
    hi = v.astype(BF16)
    r1 = v - hi.astype(F32)
    mid = r1.astype(BF16)
    lo = (r1 - mid.astype(F32)).astype(BF16)
    return hi, mid, lo


def _side_cast_step(step, nsteps, srcs, dsts, in_bufs, out_bufs, sem):
    slot = lax.rem(step, 2)

    for k, (src, dst, ibuf, obuf) in enumerate(zip(srcs, dsts, in_bufs, out_bufs)):
        rows = ibuf.shape[1]
        nslice = src.shape[0] // rows
        assert src.shape[0] == nslice * rows and rows % 16 == 0 and 2 <= nslice <= nsteps

        def fetch(s, sl, src=src, ibuf=ibuf, rows=rows, k=k):
            at = pl.ds(pl.multiple_of(s * rows, 16), rows)
            return pltpu.make_async_copy(src.at[at, :], ibuf.at[sl], sem.at[0, k, sl])

        def write(s, sl, dst=dst, obuf=obuf, rows=rows, k=k):
            at = pl.ds(pl.multiple_of(s * rows, 16), rows)
            return pltpu.make_async_copy(obuf.at[sl], dst.at[at, :], sem.at[1, k, sl])

        @pl.when(step == 0)
        def _():
            fetch(0, 0).start()

        @pl.when(step + 1 < nslice)
        def _():
            fetch(step + 1, 1 - slot).start()

        @pl.when((step >= 2) & (step - 2 < nslice))
        def _():
            write(step - 2, slot).wait()

        @pl.when(step < nslice)
        def _():
            fetch(step, slot).wait()
            obuf[slot] = ibuf[slot].astype(BF16)
            write(step, slot).start()

        for s in range(max(nsteps - 2, 0), nsteps):
            if s < nslice:
                @pl.when(step == nsteps - 1)
                def _():
                    write(s, s % 2).wait()


def _side_cast_rows(total, nsteps):
    return next(r for r in range(16, total + 1, 16) if total % r == 0 and total // r <= nsteps)


def _gla_kernel(q_ref, k_ref, v_ref, r_ref, glr_ref, w2_ref, gb_ref, ng_ref, wa_hbm, wb_hbm,
                o_ref, wa_out, wb_out, st_ref, wa_in, wb_in, wa_cast, wb_cast, cast_sem,
                *, layer, nsteps):
    c = GLA_CHUNK
    step = (pl.program_id(0) * pl.num_programs(1) + pl.program_id(1)) * pl.num_programs(2) \
        + pl.program_id(2)
    _side_cast_step(step, nsteps, (wa_hbm.at[layer], wb_hbm.at[layer]), (wa_out, wb_out),
                    (wa_in, wb_in), (wa_cast, wb_cast), cast_sem)
    sb = q_ref.shape[1]
    grp = 4 * c

    @pl.when(pl.program_id(2) == 0)
    def _():
        st_ref[...] = jnp.zeros_like(st_ref)

    z = _dot(glr_ref[0].astype(BF16), w2_ref[...]) + gb_ref[...]
    lg = (jnp.minimum(z, 0.0) - jnp.log(1.0 + jnp.exp(-jnp.abs(z)))) / GLA_TAU

    row = lax.broadcasted_iota(jnp.int32, (grp, grp), 0)
    col = lax.broadcasted_iota(jnp.int32, (grp, grp), 1)
    shift = c.bit_length() - 1
    tril = jnp.where(((row >> shift) == (col >> shift)) & (col <= row), 1.0, 0.0).astype(BF16)
    b_parts = []
    for g0 in range(0, sb, grp):
        pieces = jnp.concatenate(_split3(lg[g0:g0 + grp]), axis=1)
        res = _dot(tril, pieces)
        b_parts.append(res[:, :LANES] + res[:, LANES:2 * LANES] + res[:, 2 * LANES:])
    b = jnp.concatenate(b_parts, axis=0)

    qf = q_ref[0].astype(F32)
    kf = k_ref[0].astype(F32)
    q_in = (qf * jnp.exp(b)).astype(BF16)
    k_in = (kf * jnp.exp(-b)).astype(BF16)
    v = v_ref[0]

    causal = ((row >> shift) == (col >> shift)) & (col <= row)
    intra = []
    for g0 in range(0, sb, grp):
        rows = slice(g0, g0 + grp)
        a = jnp.where(causal, _dot_nt(q_in[rows], k_in[rows]), 0.0).astype(BF16)
        intra.append(_dot(a, v[rows]))

    st = st_ref[...]
    outs = []
    for i in range(sb // c):
        rows = slice(i * c, (i + 1) * c)
        o = intra[i * c // grp][i * c % grp:i * c % grp + c] + _dot_nt(q_in[rows], st.astype(BF16))
        outs.append(o)
        decay = jnp.exp(b[i * c + c - 1:i * c + c, :])
        st = (st + _dot_tn(v[rows], k_in[rows])) * decay
    st_ref[...] = st

    o = jnp.concatenate(outs, axis=0)
    mu = jnp.mean(o, axis=-1, keepdims=True)
    d = o - mu
    var = jnp.mean(d * d, axis=-1, keepdims=True)
    rg = r_ref[0].astype(F32)
    y = d * lax.rsqrt(var + LN_EPS) * ng_ref[...] * (rg * _sigmoid(rg))
    o_ref[0] = y.astype(o_ref.dtype)


def _gla(h3, glr3, w2p, gb, ng, sb, side_a, side_b, layer):
    bsz, s, _ = h3.shape
    nsteps = bsz * GLA_HEADS * (s // sb)
    ra, rb_ = (_side_cast_rows(a.shape[1], nsteps) for a in (side_a, side_b))
    hbm = pl.BlockSpec(memory_space=pl.ANY)
    kb = GLA_HEADS * GLA_DK // LANES
    vb = 2 * GLA_HEADS * GLA_DK // GLA_DV
    rb = vb + GLA_HEADS
    return pl.pallas_call(
        functools.partial(_gla_kernel, layer=layer, nsteps=nsteps),
        grid=(bsz, GLA_HEADS, s // sb),
        in_specs=[pl.BlockSpec((1, sb, GLA_DK), lambda b, h, j: (b, j, h)),
                  pl.BlockSpec((1, sb, GLA_DK), lambda b, h, j: (b, j, kb + h)),
                  pl.BlockSpec((1, sb, GLA_DV), lambda b, h, j: (b, j, vb + h)),
                  pl.BlockSpec((1, sb, GLA_DV), lambda b, h, j: (b, j, rb + h)),
                  pl.BlockSpec((1, sb, LANES), lambda b, h, j: (b, j, 0)),
                  pl.BlockSpec((LANES, GLA_DK), lambda b, h, j: (0, h)),
                  pl.BlockSpec((1, GLA_DK), lambda b, h, j: (0, h)),
                  pl.BlockSpec((1, GLA_DV), lambda b, h, j: (0, h)), hbm, hbm],
        out_specs=[pl.BlockSpec((1, sb, GLA_DV), lambda b, h, j: (b, j, h)), hbm, hbm],
        out_shape=[jax.ShapeDtypeStruct((bsz, s, GLA_HEADS * GLA_DV), BF16),
                   jax.ShapeDtypeStruct(side_a.shape[1:], BF16),
                   jax.ShapeDtypeStruct(side_b.shape[1:], BF16)],
        scratch_shapes=[pltpu.VMEM((GLA_DV, GLA_DK), F32),
                        pltpu.VMEM((2, ra, side_a.shape[2]), F32), pltpu.VMEM((2, rb_, side_b.shape[2]), F32),
                        pltpu.VMEM((2, ra, side_a.shape[2]), BF16), pltpu.VMEM((2, rb_, side_b.shape[2]), BF16),
                        pltpu.SemaphoreType.DMA((2, 2, 2))],
        compiler_params=_cparams("arbitrary", "arbitrary", "arbitrary"),
        name="gla",
    )(h3, h3, h3, h3, glr3, w2p, gb, ng, side_a, side_b)


def _dil_kernel(q_ref, k_ref, v_ref, o_ref, qf, kf, vf, qg, kg, vg, qc, kc, vc, ob, db, mx,
                scb, eb):
    s = q_ref.shape[1]
    band = DIL_BAND
    unroll = DIL_UNROLL
    nblk = s // band
    qf[...] = q_ref[0].astype(F32)
    kf[...] = k_ref[0].astype(F32)
    vf[...] = v_ref[0].astype(F32)
    kc[:band, :] = jnp.zeros((band, DIL_HD), BF16)
    vc[:band, :] = jnp.zeros((band, DIL_HD), BF16)

    qi = lax.broadcasted_iota(jnp.int32, (band, 2 * band), 0)
    kj = lax.broadcasted_iota(jnp.int32, (band, 2 * band), 1)
    allowed = (kj >= qi) & (kj <= qi + band)
    bias = jnp.where(allowed, 0.0, -jnp.inf).astype(F32)
    bias0 = jnp.where(allowed & (kj >= band), 0.0, -jnp.inf).astype(F32)
    ones = jnp.ones((2 * band, LANES), BF16)

    for p, (window, dil) in enumerate(DIL_PATTERNS):
        assert window // dil == band
        cls = s // dil
        nb = cls // band
        span = band * dil
        assert nblk % unroll == 0 and (nb % unroll == 0 or unroll % nb == 0)

        if dil == 1:
            qc[...] = q_ref[0]
            kc[band:, :] = k_ref[0]
            vc[band:, :] = v_ref[0]
        else:
            prev = DIL_PATTERNS[p - 1][1]
            step = dil // prev
            assert step * prev == dil and step in (2, 4)
            keep = p + 1 < len(DIL_PATTERNS)
            srcs, dsts = ((qf, kf, vf), (qg, kg, vg)) if p % 2 == 1 else ((qg, kg, vg), (qf, kf, vf))
            for r in range(dil):
                rows = pl.ds((r % prev) * (s // prev) + r // prev, cls, stride=step)
                for src, dst, cm, off in zip(srcs, dsts, (qc, kc, vc), (0, band, band)):
                    x = src[rows, :]
                    if keep:
                        dst[r * cls:(r + 1) * cls, :] = x
                    cm[off + r * cls:off + (r + 1) * cls, :] = x.astype(BF16)

        def out_rows(g, lo=0, cnt=band, dil=dil, nb=nb, span=span):
            start = g // nb + (g % nb) * span + lo * dil
            return pl.ds(start, cnt) if dil == 1 else pl.ds(start, cnt, stride=dil)

        def scores(t, slot):
            for u in range(unroll):
                g = t * unroll + u
                k2 = kc[g * band:(g + 2) * band, :]
                scb[slot, u * band:(u + 1) * band, :] = _dot_nt(qc[g * band:(g + 1) * band, :], k2)

        def softmax(t, slot, p=p, nb=nb, out_rows=out_rows):
            for u in range(unroll):
                g = t * unroll + u
                bb = bias0 if g % nb == 0 else bias
                rows = slice(u * band, (u + 1) * band)
                m = jnp.max(scb[slot, rows, :] + bb, axis=-1, keepdims=True)
                mx[p, out_rows(g), :] = jnp.broadcast_to(m, (band, LANES))
                for half in range(2):
                    cols = slice(half * band, (half + 1) * band)
                    eb[slot, rows, cols] = jnp.exp(scb[slot, rows, cols] + bb[:, cols] - m).astype(BF16)

        def values(t, slot, p=p, out_rows=out_rows):
            for u in range(unroll):
                g = t * unroll + u
                v2 = vc[g * band:(g + 2) * band, :]
                oe = _dot(eb[slot, u * band:(u + 1) * band, :],
                          jnp.concatenate([v2, ones], axis=1))
                ob[p, out_rows(g), :] = oe[:, :DIL_HD]
                db[p, out_rows(g), :] = oe[:, DIL_HD:]

        ngrp = nblk // unroll
        for t in range(ngrp + 2):
            if t < ngrp:
                scores(t, t % 2)
            if 1 <= t <= ngrp:
                softmax(t - 1, (t - 1) % 2)
            if t >= 2:
                values(t - 2, t % 2)

    mb = 512

    def merge(i, carry):
        rs = pl.ds(pl.multiple_of(i * mb, mb), mb)
        m0, m1, m2 = mx[0, rs, :], mx[1, rs, :], mx[2, rs, :]
        m = jnp.maximum(jnp.maximum(m0, m1), m2)
        e0, e1, e2 = jnp.exp(m0 - m), jnp.exp(m1 - m), jnp.exp(m2 - m)
        num = e0 * ob[0, rs, :] + e1 * ob[1, rs, :] + e2 * ob[2, rs, :]
        den = e0 * db[0, rs, :] + e1 * db[1, rs, :] + e2 * db[2, rs, :]
        o_ref[0, rs, :] = (num / den).astype(o_ref.dtype)
        return carry

    lax.fori_loop(0, s // mb, merge, 0)


def _dil(h3):
    bsz, s, _ = h3.shape
    qb = (2 * GLA_HEADS * GLA_DK + 2 * GLA_HEADS * GLA_DV) // DIL_HD
    kb = qb + DIL_HEADS
    vb = kb + DIL_HEADS
    npat = len(DIL_PATTERNS)
    return pl.pallas_call(
        _dil_kernel,
        grid=(bsz, DIL_HEADS),
        in_specs=[pl.BlockSpec((1, s, DIL_HD), lambda b, h: (b, 0, qb + h)),
                  pl.BlockSpec((1, s, DIL_HD), lambda b, h: (b, 0, kb + h)),
                  pl.BlockSpec((1, s, DIL_HD), lambda b, h: (b, 0, vb + h))],
        out_specs=pl.BlockSpec((1, s, DIL_HD), lambda b, h: (b, 0, h)),
        out_shape=jax.ShapeDtypeStruct((bsz, s, DIL_HEADS * DIL_HD), BF16),
        scratch_shapes=[pltpu.VMEM((s, DIL_HD), F32)] * 6 + [
                        pltpu.VMEM((s, DIL_HD), BF16), pltpu.VMEM((s + DIL_BAND, DIL_HD), BF16),
                        pltpu.VMEM((s + DIL_BAND, DIL_HD), BF16),
                        pltpu.VMEM((npat, s, DIL_HD), F32), pltpu.VMEM((npat, s, LANES), F32),
                        pltpu.VMEM((npat, s, LANES), F32),
                        pltpu.VMEM((2, DIL_UNROLL * DIL_BAND, 2 * DIL_BAND), F32),
                        pltpu.VMEM((2, DIL_UNROLL * DIL_BAND, 2 * DIL_BAND), BF16)],
        compiler_params=_cparams("parallel", "parallel"),
        name="dilated_attention",
    )(h3, h3, h3)


def _resident(shape):
    return pl.BlockSpec(shape, lambda *_: (0,) * len(shape), pipeline_mode=pl.Buffered(1))


def _load_weight_bf16(w_hbm, dst_ref, stage_ref, sem):
    rows = stage_ref.shape[1]
    nchunk = w_hbm.shape[0] // rows
    copies = [pltpu.make_async_copy(w_hbm.at[pl.ds(c * rows, rows), :], stage_ref.at[c % 2],
                                    sem.at[c % 2]) for c in range(nchunk)]
    copies[0].start()
    for c in range(nchunk):
        if c + 1 < nchunk:
            copies[c + 1].start()
        copies[c].wait()
        dst_ref[c * rows:(c + 1) * rows, :] = stage_ref[c % 2].astype(BF16)


def _weight_stage(cols):
    return [pltpu.VMEM((2, WEIGHT_ROWS, cols), F32), pltpu.SemaphoreType.DMA((2,))]


def _outproj_kernel(og_ref, od_ref, x_ref, g_ref, b_ref, w_hbm, o_ref, w_ref, stage_ref, sem,
                    *, alpha):
    @pl.when(pl.program_id(0) == 0)
    def _():
        _load_weight_bf16(w_hbm, w_ref, stage_ref, sem)

    ka = og_ref.shape[1]
    acc = _dot(og_ref[...], w_ref[:ka, :]) + _dot(od_ref[...], w_ref[ka:, :])
    o_ref[...] = _layer_norm(alpha * x_ref[...] + acc, g_ref[...], b_ref[...])


def _outproj(og, od, w, x2d, g, b, bm, alpha):
    t, d = x2d.shape
    ka, kb = og.shape[1], od.shape[1]
    assert w.shape == (ka + kb, d)
    return pl.pallas_call(
        functools.partial(_outproj_kernel, alpha=alpha),
        grid=(t // bm,),
        in_specs=[pl.BlockSpec((bm, ka), lambda i: (i, 0)),
                  pl.BlockSpec((bm, kb), lambda i: (i, 0)),
                  pl.BlockSpec((bm, d), lambda i: (i, 0)),
                  _resident((1, d)), _resident((1, d)),
                  pl.BlockSpec(memory_space=pl.ANY)],
        out_specs=pl.BlockSpec((bm, d), lambda i: (i, 0)),
        out_shape=jax.ShapeDtypeStruct((t, d), F32),
        scratch_shapes=[pltpu.VMEM((ka + kb, d), BF16)] + _weight_stage(d),
        compiler_params=_cparams("arbitrary"),
        name="out_projection_ln1",
    )(og, od, x2d, g, b, w)


def _memkv_kernel(m_ref, w_ref, o_ref):
    o_ref[...] = _dot(m_ref[...].astype(BF16), w_ref[...].astype(BF16)).astype(o_ref.dtype)


def _memkv(mem2d, wkv, bn):
    t, d = mem2d.shape
    n = wkv.shape[1]
    return pl.pallas_call(
        _memkv_kernel,
        grid=(n // bn,),
        in_specs=[pl.BlockSpec((t, d), lambda j: (0, 0)),
                  pl.BlockSpec((d, bn), lambda j: (0, j))],
        out_specs=pl.BlockSpec((t, bn), lambda j: (0, j)),
        out_shape=jax.ShapeDtypeStruct((t, n), BF16),
        compiler_params=_cparams("parallel"),
        name="memory_kv_projection",
    )(mem2d, wkv)


def _cross_kernel(x_ref, kv_ref, g_ref, b_ref, wq_hbm, wo_hbm, o_ref,
                  wq_ref, wo_ref, oc_ref, stage_ref, sem, *, alpha):
    @pl.when(pl.program_id(0) == 0)
    def _():
        _load_weight_bf16(wq_hbm, wq_ref, stage_ref, sem)
        _load_weight_bf16(wo_hbm, wo_ref, stage_ref, sem)

    d = x_ref.shape[1]
    hd = d // CA_HEADS
    x = x_ref[...]
    q = _dot(x.astype(BF16), wq_ref[...]).astype(BF16)
    for h in range(CA_HEADS):
        cols = slice(h * hd, (h + 1) * hd)
        mk = kv_ref[0, :, cols]
        mv = kv_ref[0, :, d + h * hd:d + (h + 1) * hd]
        sc = _dot_nt(q[:, cols], mk) * (hd ** -0.5)
        m = jnp.max(sc, axis=-1, keepdims=True)
        e = jnp.exp(sc - m)
        p = e / jnp.sum(e, axis=-1, keepdims=True)
        oc_ref[:, cols] = _dot(p.astype(BF16), mv).astype(BF16)
    y = alpha * x + _dot(oc_ref[...], wo_ref[...])
    o_ref[...] = _layer_norm(y, g_ref[...], b_ref[...])


def _cross(x1, wq, kv3, wo, g, b, bm, seq, alpha):
    t, d = x1.shape
    m = kv3.shape[1]
    per = seq // bm
    hbm = pl.BlockSpec(memory_space=pl.ANY)
    return pl.pallas_call(
        functools.partial(_cross_kernel, alpha=alpha),
        grid=(t // bm,),
        in_specs=[pl.BlockSpec((bm, d), lambda i: (i, 0)),
                  pl.BlockSpec((1, m, 2 * d), lambda i: (i // per, 0, 0)),
                  _resident((1, d)), _resident((1, d)), hbm, hbm],
        out_specs=pl.BlockSpec((bm, d), lambda i: (i, 0)),
        out_shape=jax.ShapeDtypeStruct((t, d), F32),
        scratch_shapes=[pltpu.VMEM((d, d), BF16), pltpu.VMEM((d, d), BF16),
                        pltpu.VMEM((bm, d), BF16)] + _weight_stage(d),
        compiler_params=_cparams("arbitrary"),
        name="cross_attention_ln2",
    )(x1, kv3, g, b, wq, wo)


def _causal_conv(u_ref, cw, cb, r0, n):
    h = FFN_TAIL + r0
    y = cb + cw[0:1, :] * u_ref[h - 2:h - 2 + n, :]
    y = y + cw[1:2, :] * u_ref[h - 1:h - 1 + n, :]
    return y + cw[2:3, :] * u_ref[h:h + n, :]


def _ffn_kernel(cp_ref, g_ref, b_ref, x_hbm, w1_hbm, w2_hbm, o_ref,
                x_buf, xb_ref, act0_ref, act1_ref, us_ref, tail_ref, wgu_buf, w2_buf,
                sem, x_sem, *, alpha, per):
    i = pl.program_id(0)
    d_ff = w2_hbm.shape[0]
    bm = x_buf.shape[0]
    bf = wgu_buf.shape[2] // 2
    nf = pl.cdiv(d_ff, bf)
    last_w = d_ff - (nf - 1) * bf
    assert nf % 2 == 1 and nf >= 3
    assert bf % FFN_SUB == 0 and last_w % LANES == 0 and bm % FFN_ROWS == 0
    c_first = i * nf
    acts = (act0_ref, act1_ref)
    nslab = bf // LANES
    half = o_ref.shape[1] // FFN_DOWN_SPLIT

    def width(f):
        return last_w if isinstance(f, int) and f == nf - 1 else bf

    def up_copies(f, slot):
        w = width(f)
        col = f * bf if isinstance(f, int) else pl.multiple_of(f * bf, bf)
        return (pltpu.make_async_copy(w1_hbm.at[:, pl.ds(col, w)],
                                      wgu_buf.at[slot, :, pl.ds(0, w)], sem.at[0, slot]),
                pltpu.make_async_copy(w1_hbm.at[:, pl.ds(d_ff + col, w)],
                                      wgu_buf.at[slot, :, pl.ds(bf, w)], sem.at[1, slot]))

    def down_copy(f, slot):
        w = width(f)
        row = f * bf if isinstance(f, int) else pl.multiple_of(f * bf, bf)
        return pltpu.make_async_copy(w2_hbm.at[pl.ds(row, w), :],
                                     w2_buf.at[slot, pl.ds(0, w), :], sem.at[2, slot])

    def x_copy(step):
        row = pl.multiple_of(step * bm, bm)
        return pltpu.make_async_copy(x_hbm.at[pl.ds(row, bm), :], x_buf, x_sem.at[0])

    def region_copies(f):
        slot = lax.rem(c_first + f, 2)
        for cp in up_copies(f, slot):
            cp.wait()
        if isinstance(f, int) and f == 0:
            @pl.when(i + 1 < pl.num_programs(0))
            def _():
                x_copy(i + 1).start()
        if not (isinstance(f, int) and f == 0):
            down_copy(f - 1, 1 - slot).wait()
        if isinstance(f, int) and f == nf - 1:
            @pl.when(i + 1 < pl.num_programs(0))
            def _():
                for cp in up_copies(0, 1 - slot):
                    cp.start()
        else:
            for cp in up_copies(f + 1, 1 - slot):
                cp.start()
        down_copy(f, slot).start()
        return slot

    def up_matmul(slot, c0, wcols, r0):
        xr = xb_ref[r0:r0 + FFN_ROWS, :]
        rows = slice(FFN_TAIL + r0, FFN_TAIL + r0 + FFN_ROWS)
        if wcols == bf:
            res = _dot(xr, wgu_buf[slot])
            for s in range(2 * nslab):
                us_ref[s, rows, :] = res[:, s * LANES:(s + 1) * LANES]
            return
        for base in (0, nslab):
            res = _dot(xr, wgu_buf[slot, :, base * LANES + c0:base * LANES + c0 + wcols])
            for k in range(wcols // LANES):
                us_ref[base + c0 // LANES + k, rows, :] = res[:, k * LANES:(k + 1) * LANES]

    def activate(cv, act_ref, c0, wcols, r0):
        for s in range(c0 // LANES, (c0 + wcols) // LANES):
            cols = slice(s * LANES, (s + 1) * LANES)
            gate = _causal_conv(us_ref.at[s], cv[0:3, cols], cv[3:4, cols], r0, FFN_ROWS)
            up = _causal_conv(us_ref.at[nslab + s], cv[4:7, cols], cv[7:8, cols], r0, FFN_ROWS)
            act_ref[r0:r0 + FFN_ROWS, cols] = (gate * _sigmoid(gate) * up).astype(BF16)

    def down_matmul(slot, act_ref, w, n0):
        o_ref[:, n0:n0 + half] += _dot(act_ref[:, :w], w2_buf[slot, :w, n0:n0 + half])

    def region(f, act_slot):
        slot = region_copies(f)
        w = width(f)
        cv = cp_ref[f]
        act_ref, prev_ref = acts[act_slot], acts[1 - act_slot]
        slabs = [s for base in (0, nslab) for s in range(base, base + w // LANES)]
        for s in slabs:
            us_ref[s, :FFN_TAIL, :] = tail_ref[f, :, s * LANES:(s + 1) * LANES]
        units = [(c0, min(FFN_SUB, w - c0), r0) for c0 in range(0, w, FFN_SUB)
                 for r0 in range(0, bm, FFN_ROWS)]
        for unit in units:
            up_matmul(slot, *unit)
        if not (isinstance(f, int) and f == 0):
            for n0 in range(0, o_ref.shape[1], half):
                down_matmul(1 - slot, prev_ref, width(f - 1) if isinstance(f, int) else bf, n0)
        for unit in units:
            activate(cv, act_ref, *unit)
        for s in slabs:
            tail_ref[f, :, s * LANES:(s + 1) * LANES] = us_ref[s, bm:bm + FFN_TAIL, :]

    @pl.when(i == 0)
    def _():
        x_copy(0).start()
        for cp in up_copies(0, 0):
            cp.start()

    @pl.when(i % per == 0)
    def _():
        tail_ref[...] = jnp.zeros_like(tail_ref)

    x_copy(i).wait()
    xb_ref[...] = x_buf[...].astype(BF16)
    o_ref[...] = alpha * x_buf[...]

    region(0, 0)

    def pair(j, carry):
        region(2 * j + 1, 1)
        region(2 * j + 2, 0)
        return carry

    lax.fori_loop(0, (nf - 3) // 2, pair, 0)
    region(nf - 2, 1)
    region(nf - 1, 0)

    last = lax.rem(c_first + nf - 1, 2)
    down_copy(nf - 1, last).wait()
    for n0 in range(0, o_ref.shape[1], half):
        down_matmul(last, acts[0], last_w, n0)
    o_ref[...] = _layer_norm(o_ref[...], g_ref[...], b_ref[...])


def _ffn(x2, w1, w2, cp, g, b, bm, seq, alpha):
    t, d = x2.shape
    bf = FFN_CHUNK
    nf = cp.shape[0]
    hbm = pl.BlockSpec(memory_space=pl.ANY)
    return pl.pallas_call(
        functools.partial(_ffn_kernel, alpha=alpha, per=seq // bm),
        grid=(t // bm,),
        in_specs=[_resident(cp.shape), _resident((1, d)), _resident((1, d)), hbm, hbm, hbm],
        out_specs=pl.BlockSpec((bm, d), lambda i: (i, 0)),
        out_shape=jax.ShapeDtypeStruct((t, d), F32),
        scratch_shapes=[pltpu.VMEM((bm, d), F32), pltpu.VMEM((bm, d), BF16),
                        pltpu.VMEM((bm, bf), BF16), pltpu.VMEM((bm, bf), BF16),
                        pltpu.VMEM((2 * bf // LANES, FFN_TAIL + bm, LANES), F32),
                        pltpu.VMEM((nf, FFN_TAIL, 2 * bf), F32),
                        pltpu.VMEM((2, d, 2 * bf), BF16),
                        pltpu.VMEM((2, bf, d), BF16), pltpu.SemaphoreType.DMA((3, 2)),
                        pltpu.SemaphoreType.DMA((1,))],
        compiler_params=_cparams("arbitrary"),
        name="conv_ffn_ln3",
    )(cp, g, b, x2, w1, w2)


def _pad_cols(a, n):
    return jnp.pad(a, ((0, 0), (0, n - a.shape[1])))


def kernel(x, mem, positions, w_in, gla_gate_w2, gla_gate_b, gla_norm_g, w_out, ln1_g, ln1_b,
           ca_wq, ca_wkv, ca_wo, ln2_g, ln2_b, ffn_w_in, ffn_conv_w, ffn_conv_b, ffn_w_out,
           ln3_g, ln3_b):
    bsz, seq, d = x.shape
    depth = w_in.shape[0]
    t = bsz * seq
    alpha = (2.0 * depth) ** 0.25
    d_ff = ffn_w_out.shape[1]
    dff_pad = -(-d_ff // FFN_CHUNK) * FFN_CHUNK

    nqk = GLA_HEADS * GLA_DK
    nv = GLA_HEADS * GLA_DV
    c_glr = 2 * nqk + 2 * nv
    c_dil = c_glr + GLA_GATE_RANK

    half = ROPE_HALF
    inv_freq = ROPE_THETA ** (-jnp.arange(0, ROPE_DIMS, 2, dtype=F32) / ROPE_DIMS)
    inv_row = jnp.tile(inv_freq, LANES // half)[None, :]
    pos8 = positions.astype(F32).reshape(t // 8, 8)
    pos8 = jnp.concatenate([pos8[:, :1], pos8[:, :0:-1]], axis=1)
    pos_col = jnp.repeat(pos8, half, axis=1)

    x2d = x.reshape(t, d)
    for l in range(depth):
        wcat, wglr = _prep_win(jnp.swapaxes(w_in[l], 0, 1), c_glr, c_dil, PREP_COLS)
        h, glr = _inproj(x2d, wcat, wglr, pos_col, inv_row, min(t, INPROJ_TOKENS))
        h3 = h.reshape(bsz, seq, h.shape[1])

        w2p = jnp.pad(gla_gate_w2[l], ((0, LANES - GLA_GATE_RANK), (0, 0))).astype(BF16)
        og, ffn_w1, ffn_w2 = _gla(h3, glr.reshape(bsz, seq, LANES), w2p, gla_gate_b[l][None, :],
                                  gla_norm_g[l][None, :], min(seq, GLA_TOKENS), ffn_w_in, ffn_w_out, l)
        od = _dil(h3)

        x1 = _outproj(og.reshape(t, nv), od.reshape(t, DIL_HEADS * DIL_HD), w_out[l],
                      x2d, ln1_g[l][None, :], ln1_b[l][None, :], min(t, OUTPROJ_TOKENS), alpha)

        kv = _memkv(mem.reshape(-1, d), ca_wkv[l], MEMKV_COLS)
        x2 = _cross(x1, ca_wq[l], kv.reshape(bsz, -1, 2 * d), ca_wo[l],
                    ln2_g[l][None, :], ln2_b[l][None, :], min(seq, CROSS_TOKENS), seq, alpha)

        cw = ffn_conv_w[l]
        cb = ffn_conv_b[l][None, :]
        nf = dff_pad // FFN_CHUNK
        conv = jnp.concatenate([cw[:, :d_ff], cb[:, :d_ff], cw[:, d_ff:], cb[:, d_ff:]], axis=0)
        conv = _pad_cols(conv, dff_pad).reshape(conv.shape[0], nf, FFN_CHUNK).transpose(1, 0, 2)
        x2d = _ffn(x2, ffn_w1, ffn_w2, conv,
                   ln3_g[l][None, :], ln3_b[l][None, :], min(seq, FFN_TOKENS), seq, alpha)
    return x2d.reshape(bsz, seq, d)
```
